```python
import math
import jax, jax.numpy as jnp
from jax import lax
import numpy as np

D_MODEL = 1024
BATCH = 32
SEQ = 2048
DEPTH = 4
DEC_BATCH = 8
DEC_SEQ = 2048
PAST_LEN = 128

MIX = D_MODEL
GROUP_W = MIX // 4
HEAD_DIM = 64
POOL_WINDOWS = (2, 4, 8, 16)
POOL_GROUP = GROUP_W // len(POOL_WINDOWS)
ATTN_HEADS = GROUP_W // HEAD_DIM
DILATED_PATTERNS = ((128, 1), (512, 4), (2048, 16))
DIL_BLOCK = 64
SSM_CH = 16
SSM_GROUPS = GROUP_W // SSM_CH
SSM_STATE = 64
NA_HEADS = GROUP_W // HEAD_DIM
GRID_W = 64
NA_ROWS_MAX = 8
NA_COLS = 16
ROPE_THETA = 10000.0
EPS = 1e-6
N_BLOCKS = 12
PROJ_W = N_BLOCKS * GROUP_W
NEG = -1e30

kernel_name = 'hybrid_parallel_encoder'


def rmsnorm(x, g):
    xf = x.astype(jnp.float32)
    y = xf * lax.rsqrt(jnp.mean(xf * xf, axis=-1, keepdims=True) + EPS)
    return (y * g.astype(jnp.float32)).astype(x.dtype)


def rope(x):
    s = x.shape[1]
    inv = ROPE_THETA ** (-jnp.arange(0, HEAD_DIM, 2, dtype=jnp.float32) / HEAD_DIM)
    ang = jnp.arange(s, dtype=jnp.float32)[:, None] * inv[None, :]
    cos = jnp.cos(ang)[None, :, None, :]
    sin = jnp.sin(ang)[None, :, None, :]
    xf = x.astype(jnp.float32)
    x1, x2 = xf[..., :HEAD_DIM // 2], xf[..., HEAD_DIM // 2:]
    return jnp.concatenate([x1 * cos - x2 * sin, x2 * cos + x1 * sin], axis=-1).astype(x.dtype)


def pool_mixer(u, pool_w, pool_scale):
    s = u.shape[1]
    t = jnp.arange(s)
    outs = []
    for gi, w in enumerate(POOL_WINDOWS):
        ug = u[..., gi * POOL_GROUP:(gi + 1) * POOL_GROUP].astype(jnp.float32)
        cs = jnp.pad(jnp.cumsum(ug, axis=1), ((0, 0), (1, 0), (0, 0)))
        lo = jnp.clip(t - w // 2, 0, s)
        hi = jnp.clip(t + w // 2, 0, s)
        cnt = (hi - lo).astype(jnp.float32)[None, :, None]
        mean = (cs[:, hi] - cs[:, lo]) / cnt
        outs.append(jnp.einsum('bsc,cd->bsd', mean - ug, pool_w[gi].astype(jnp.float32)))
    return (jnp.concatenate(outs, axis=-1) * pool_scale.astype(jnp.float32)).astype(u.dtype)


def dilated_attention(q, k, v):
    bsz, s, h, e = q.shape
    f32 = jnp.float32
    outs, lses = [], []
    for window, d in DILATED_PATTERNS:
        r = window // (2 * d)
        n_sub = s // d
        nb = -(-n_sub // DIL_BLOCK)
        lp = nb * DIL_BLOCK

        def sub(a):
            return a.reshape(bsz, n_sub, d, h, e).transpose(0, 2, 1, 3, 4)

        qs = jnp.pad(sub(q), ((0, 0), (0, 0), (0, lp - n_sub), (0, 0), (0, 0)))
        qs = qs.reshape(bsz, d, nb, DIL_BLOCK, h, e)
        kpad = ((0, 0), (0, 0), (r, r + lp - n_sub), (0, 0), (0, 0))
        ks = jnp.pad(sub(k), kpad)
        vs = jnp.pad(sub(v), kpad)
        idx = jnp.arange(nb)[:, None] * DIL_BLOCK + jnp.arange(DIL_BLOCK + 2 * r)[None, :]
        kb = ks[:, :, idx]
        vb = vs[:, :, idx].astype(f32)
        sc = jnp.einsum('bdnqhe,bdnkhe->bdhnqk', qs, kb, preferred_element_type=f32) * (e ** -0.5)
        lq = jnp.arange(nb)[:, None] * DIL_BLOCK + jnp.arange(DIL_BLOCK)[None, :]
        lk = idx - r
        rel = lk[:, None, :] - lq[:, :, None]
        ok = (jnp.abs(rel) <= r) & (lk[:, None, :] >= 0) & (lk[:, None, :] < n_sub)
        sc = jnp.where(ok, sc, NEG)
        m = jnp.max(sc, axis=-1, keepdims=True)
        p = jnp.exp(sc - m)
        den = jnp.sum(p, axis=-1, keepdims=True)
        o = jnp.einsum('bdhnqk,bdnkhe->bdnqhe', p / den, vb)
        lse = (m + jnp.log(den))[..., 0]
        o = o.reshape(bsz, d, lp, h, e)[:, :, :n_sub].transpose(0, 2, 1, 3, 4).reshape(bsz, s, h, e)
        lse = lse.transpose(0, 1, 3, 4, 2).reshape(bsz, d, lp, h)[:, :, :n_sub]
        lse = lse.transpose(0, 2, 1, 3).reshape(bsz, s, h)
        outs.append(o)
        lses.append(lse)
    wts = jax.nn.softmax(jnp.stack(lses, axis=0), axis=0)
    out = jnp.sum(wts[..., None] * jnp.stack(outs, axis=0), axis=0)
    return out.astype(q.dtype)


def _complex_linear_combine(e1, e2):
    a1r, a1i, b1r, b1i = e1
    a2r, a2i, b2r, b2i = e2
    return (a2r * a1r - a2i * a1i,
            a2r * a1i + a2i * a1r,
            a2r * b1r - a2i * b1i + b2r,
            a2r * b1i + a2i * b1r + b2i)


def s5_mixer(u, a_re, a_im, log_dt, b_re, b_im, c_re, c_im, d_skip, glu_w, glu_b):
    f32 = jnp.float32
    bsz, s, _ = u.shape
    uf = u.astype(f32)
    ug = uf.reshape(bsz, s, SSM_GROUPS, SSM_CH)
    y = uf * d_skip.astype(f32)
    br = b_re.astype(f32)
    bi = b_im.astype(f32)
    for direction in range(2):
        lr = a_re[direction].astype(f32)
        li = a_im[direction].astype(f32)
        dt = jnp.exp(log_dt[direction].astype(f32))[:, None]
        mag = jnp.exp(lr * dt)
        abr = mag * jnp.cos(li * dt)
        abi = mag * jnp.sin(li * dt)
        den = lr * lr + li * li
        gr = ((abr - 1.0) * lr + abi * li) / den
        gi = (abi * lr - (abr - 1.0) * li) / den
        bbr = gr[..., None] * br - gi[..., None] * bi
        bbi = gr[..., None] * bi + gi[..., None] * br
        xr = jnp.einsum('bsgc,gpc->bsgp', ug, bbr)
        xi = jnp.einsum('bsgc,gpc->bsgp', ug, bbi)
        ar = jnp.broadcast_to(abr, xr.shape)
        ai = jnp.broadcast_to(abi, xi.shape)
        _, _, hr, hi = lax.associative_scan(_complex_linear_combine, (ar, ai, xr, xi),
                                            axis=1, reverse=(direction == 1))
        out = (jnp.einsum('bsgp,gcp->bsgc', hr, c_re[direction].astype(f32))
               - jnp.einsum('bsgp,gcp->bsgc', hi, c_im[direction].astype(f32)))
        y = y + out.reshape(bsz, s, GROUP_W)
    g = jax.nn.gelu(y)
    g = g * jax.nn.sigmoid(jnp.einsum('bsc,ce->bse', g, glu_w.astype(f32)) + glu_b.astype(f32))
    return g.astype(u.dtype)


def neighbourhood_attention(q, k, v, rpb):
    f32 = jnp.float32
    bsz, s, h, e = q.shape
    rows = s // GRID_W
    wr = min(NA_ROWS_MAX, rows)
    r = jnp.arange(rows)
    c = jnp.arange(GRID_W)
    rs = jnp.clip(r - wr // 2, 0, rows - wr)
    row_idx = rs[:, None] + jnp.arange(wr)[None, :]
    qg = q.reshape(bsz, rows, GRID_W, h, e)
    kr = k.reshape(bsz, rows, GRID_W, h, e)[:, row_idx]
    vr = v.reshape(bsz, rows, GRID_W, h, e)[:, row_idx].astype(f32)
    sc = jnp.einsum('brchd,briwhd->bhrciw', qg, kr, preferred_element_type=f32) * (e ** -0.5)
    cs = jnp.clip(c - NA_COLS // 2, 0, GRID_W - NA_COLS)
    col_ok = (c[None, :] >= cs[:, None]) & (c[None, :] < cs[:, None] + NA_COLS)
    drow = row_idx - r[:, None] + NA_ROWS_MAX - 1
    dcol = jnp.clip(c[None, :] - c[:, None] + NA_COLS - 1, 0, 2 * NA_COLS - 2)
    bias = rpb.astype(f32)[:, drow[:, None, :, None], dcol[None, :, None, :]]
    bias = jnp.where(col_ok[None, None, :, None, :], bias, NEG)
    sc = (sc + bias[None]).reshape(bsz, h, rows, GRID_W, wr * GRID_W)
    p = jax.nn.softmax(sc, axis=-1)
    o = jnp.einsum('bhrcn,brnhd->brchd', p, vr.reshape(bsz, rows, wr * GRID_W, h, e))
    return o.reshape(bsz, s, h, e).astype(q.dtype)


def mixer_layer(x, norm_g, w_in, w_out, pool_w, pool_scale, a_re, a_im, log_dt,
                b_re, b_im, c_re, c_im, ssm_d, glu_w, glu_b, na_rpb):
    bsz, s, _ = x.shape
    hdn = rmsnorm(x, norm_g)
    z = jnp.einsum('bsd,dp->bsp', hdn, w_in)
    (a_v, a_g, b_q, b_k, b_v, b_g, c_u, c_g, d_q, d_k, d_v, d_g) = jnp.split(z, N_BLOCKS, axis=-1)

    def heads(t):
        return t.reshape(bsz, s, ATTN_HEADS, HEAD_DIM)

    y_a = pool_mixer(a_v, pool_w, pool_scale) * jax.nn.silu(a_g)
    y_b = dilated_attention(rope(heads(b_q)), rope(heads(b_k)), heads(b_v)).reshape(bsz, s, GROUP_W)
    y_b = y_b * jax.nn.silu(b_g)
    y_c = s5_mixer(c_u, a_re, a_im, log_dt, b_re, b_im, c_re, c_im, ssm_d, glu_w, glu_b) * jax.nn.silu(c_g)
    y_d = neighbourhood_attention(heads(d_q), heads(d_k), heads(d_v), na_rpb).reshape(bsz, s, GROUP_W)
    y_d = y_d * jax.nn.silu(d_g)
    y = jnp.concatenate([y_a, y_b, y_c, y_d], axis=-1)
    return x + jnp.einsum('bsm,md->bsd', y, w_out)


def setup_inputs(seed: int = 0) -> dict:
    key = jax.random.key(seed)
    ks = jax.random.split(key, 19)
    nrm = jax.random.normal
    f32 = jnp.float32
    return {
        'x_prompt': nrm(ks[0], (BATCH, SEQ, D_MODEL), f32),
        'x_sample': nrm(ks[1], (DEC_BATCH, DEC_SEQ, D_MODEL), f32),
        'norm_g': 1.0 + 0.02 * nrm(ks[2], (DEPTH, D_MODEL), f32),
        'w_in': nrm(ks[3], (DEPTH, D_MODEL, PROJ_W), f32) * D_MODEL ** -0.5,
        'w_out': nrm(ks[4], (DEPTH, MIX, D_MODEL), f32) * (0.5 * MIX ** -0.5),
        'pool_w': nrm(ks[5], (DEPTH, len(POOL_WINDOWS), POOL_GROUP, POOL_GROUP), f32) * POOL_GROUP ** -0.5,
        'pool_scale': 1.0 + 0.02 * nrm(ks[6], (DEPTH, GROUP_W), f32),
        'ssm_a_re': -0.5 + 0.01 * nrm(ks[7], (DEPTH, 2, SSM_GROUPS, SSM_STATE), f32),
        'ssm_a_im': math.pi * jnp.arange(SSM_STATE, dtype=f32) + 0.01 * nrm(ks[8], (DEPTH, 2, SSM_GROUPS, SSM_STATE), f32),
        'ssm_log_dt': jax.random.uniform(ks[9], (DEPTH, 2, SSM_GROUPS), f32, math.log(1e-3), math.log(1e-1)),
        'ssm_b_re': nrm(ks[10], (DEPTH, SSM_GROUPS, SSM_STATE, SSM_CH), f32) * (2 * SSM_CH) ** -0.5,
        'ssm_b_im': nrm(ks[11], (DEPTH, SSM_GROUPS, SSM_STATE, SSM_CH), f32) * (2 * SSM_CH) ** -0.5,
        'ssm_c_re': nrm(ks[12], (DEPTH, 2, SSM_GROUPS, SSM_CH, SSM_STATE), f32) * SSM_STATE ** -0.5,
        'ssm_c_im': nrm(ks[13], (DEPTH, 2, SSM_GROUPS, SSM_CH, SSM_STATE), f32) * SSM_STATE ** -0.5,
        'ssm_d': nrm(ks[14], (DEPTH, GROUP_W), f32),
        'glu_w': nrm(ks[15], (DEPTH, GROUP_W, GROUP_W), f32) * GROUP_W ** -0.5,
        'glu_b': 0.01 * nrm(ks[16], (DEPTH, GROUP_W), f32),
        'na_rpb': 0.02 * nrm(ks[17], (DEPTH, NA_HEADS, 2 * NA_ROWS_MAX - 1, 2 * NA_COLS - 1), f32),
        'final_g': 1.0 + 0.02 * nrm(ks[18], (D_MODEL,), f32),
    }


def reference(x_prompt, x_sample, norm_g, w_in, w_out, pool_w, pool_scale, ssm_a_re, ssm_a_im,
              ssm_log_dt, ssm_b_re, ssm_b_im, ssm_c_re, ssm_c_im, ssm_d, glu_w, glu_b, na_rpb, final_g):
    def trunk(x):
        for l in range(DEPTH):
            x = mixer_layer(x, norm_g[l], w_in[l], w_out[l], pool_w[l], pool_scale[l],
                            ssm_a_re[l], ssm_a_im[l], ssm_log_dt[l], ssm_b_re[l], ssm_b_im[l],
                            ssm_c_re[l], ssm_c_im[l], ssm_d[l], glu_w[l], glu_b[l], na_rpb[l])
        return rmsnorm(x, final_g)

    y_prompt = trunk(x_prompt)
    y_sample = trunk(x_sample)
    return (y_prompt, y_sample)
```

```python
import functools
import math

import numpy as np
import jax
import jax.numpy as jnp
from jax import lax
from jax.experimental import pallas as pl
from jax.experimental.pallas import tpu as pltpu

D_MODEL = 1024
SEQ = 2048
DEPTH = 4
GROUP_W = 256
HEAD_DIM = 64
POOL_WINDOWS = (2, 4, 8, 16)
POOL_GROUP = 64
DILATED_PATTERNS = ((128, 1), (512, 4), (2048, 16))
SSM_CH = 16
SSM_GROUPS = 16
SSM_STATE = 64
GRID_W = 64
NA_ROWS = 8
NA_COLS = 16
ROPE_THETA = 10000.0
EPS = 1e-6
PROJ_W = 12 * GROUP_W
NEG = -1e30

F32 = jnp.float32
BF16 = jnp.bfloat16
HI = lax.Precision.HIGHEST

ROW_TILE = 512
POOL_PAD = 16
DIL_Q = 128
DIL_R = 64
SSM_L = 32
SSM_NB = 8
VMEM_LIMIT = 56 * 1024 * 1024


def _sigmoid(x):
    return 1.0 / (1.0 + jnp.exp(-x))


def _silu(x):
    return x * _sigmoid(x)


def _gelu_tanh(x):
    return 0.5 * x * (1.0 + jnp.tanh(math.sqrt(2.0 / math.pi) * (x + 0.044715 * (x * x * x))))


def _params(n_parallel=1, n_arbitrary=0):
    sem = ("parallel",) * n_parallel + ("arbitrary",) * n_arbitrary
    return pltpu.CompilerParams(dimension_semantics=sem, vmem_limit_bytes=VMEM_LIMIT)


def _in_proj_kernel(x_ref, g_ref, w_ref, z_ref):
    x = x_ref[...]
    ms = jnp.mean(x * x, axis=-1, keepdims=True)
    h = (x * lax.rsqrt(ms + EPS) * g_ref[...]).astype(BF16)
    nch = 512
    for n in range(PROJ_W // nch):
        z_ref[:, n * nch:(n + 1) * nch] = jnp.dot(
            h, w_ref[:, n * nch:(n + 1) * nch], preferred_element_type=F32).astype(BF16)


def _in_proj(x2, g, w_bf):
    rows = x2.shape[0]
    return pl.pallas_call(
        _in_proj_kernel,
        grid=(rows // ROW_TILE,),
        in_specs=[
            pl.BlockSpec((ROW_TILE, D_MODEL), lambda i: (i, 0)),
            pl.BlockSpec((1, D_MODEL), lambda i: (0, 0)),
            pl.BlockSpec((D_MODEL, PROJ_W), lambda i: (0, 0)),
        ],
        out_specs=pl.BlockSpec((ROW_TILE, PROJ_W), lambda i: (i, 0)),
        out_shape=jax.ShapeDtypeStruct((rows, PROJ_W), BF16),
        compiler_params=_params(),
        name="in_proj",
    )(x2, g, w_bf)


def _pool_kernel(z_ref, inv_ref, w_ref, scale_ref, o_ref, ub, s2b, s4b, s8b):
    s = SEQ
    p = POOL_PAD
    n = s + p
    u = z_ref[:, 0:GROUP_W].astype(F32)
    zero_p = jnp.zeros((p, GROUP_W), F32)
    zero_8 = jnp.zeros((8, GROUP_W), F32)
    ub[0:p, :] = zero_p
    ub[p:p + s, :] = u
    ub[p + s:p + s + p, :] = zero_p
    for buf in (s2b, s4b, s8b):
        buf[0:8, :] = zero_8
        buf[s + 24:s + 32, :] = zero_8
    s2b[8:8 + n, :] = ub[7:7 + n, :] + ub[8:8 + n, :]
    s4b[8:8 + n, :] = s2b[7:7 + n, :] + s2b[9:9 + n, :]
    s8b[8:8 + n, :] = s4b[6:6 + n, :] + s4b[10:10 + n, :]
    s16 = s8b[p - 4:p - 4 + s, :] + s8b[p + 4:p + 4 + s, :]
    lane = lax.broadcasted_iota(jnp.int32, (s, GROUP_W), 1)
    win = jnp.where(lane < 64, s2b[p:p + s, :],
                    jnp.where(lane < 128, s4b[p:p + s, :],
                              jnp.where(lane < 192, s8b[p:p + s, :], s16)))
    diff = (win * inv_ref[...] - u).astype(BF16)
    y = jnp.dot(diff, w_ref[...], preferred_element_type=F32) * scale_ref[...]
    gate = z_ref[:, GROUP_W:2 * GROUP_W].astype(F32)
    o_ref[...] = (y * _silu(gate)).astype(BF16)


def _pool_inv_count():
    t = np.arange(SEQ)
    cols = []
    for w in POOL_WINDOWS:
        lo = np.clip(t - w // 2, 0, SEQ)
        hi = np.clip(t + w // 2, 0, SEQ)
        cols.append(np.repeat((1.0 / (hi - lo))[:, None], POOL_GROUP, axis=1))
    return jnp.asarray(np.concatenate(cols, axis=1), F32)


def _pool(z3, inv_cnt, w_bd, scale):
    b = z3.shape[0]
    return pl.pallas_call(
        _pool_kernel,
        grid=(b,),
        in_specs=[
            pl.BlockSpec((None, SEQ, 2 * GROUP_W), lambda i: (i, 0, 0)),
            pl.BlockSpec((SEQ, GROUP_W), lambda i: (0, 0)),
            pl.BlockSpec((GROUP_W, GROUP_W), lambda i: (0, 0)),
            pl.BlockSpec((1, GROUP_W), lambda i: (0, 0)),
        ],
        out_specs=pl.BlockSpec((None, SEQ, GROUP_W), lambda i: (i, 0, 0)),
        out_shape=jax.ShapeDtypeStruct((b, SEQ, GROUP_W), BF16),
        scratch_shapes=[pltpu.VMEM((SEQ + 2 * POOL_PAD, GROUP_W), F32) for _ in range(4)],
        compiler_params=_params(),
        name="pool_mixer",
    )(z3, inv_cnt, w_bd, scale)


def _rope_tables():
    inv = ROPE_THETA ** (-np.arange(0, HEAD_DIM, 2, dtype=np.float64) / HEAD_DIM)
    ang = np.arange(SEQ, dtype=np.float64)[:, None] * inv[None, :]
    cos = np.concatenate([np.cos(ang), np.cos(ang)], axis=1)
    sin = np.concatenate([-np.sin(ang), np.sin(ang)], axis=1)
    return (jnp.asarray(np.tile(cos, (1, 2)), F32), jnp.asarray(np.tile(sin, (1, 2)), F32))


def _softmax_rows(sc):
    m = jnp.max(sc, axis=-1, keepdims=True)
    p = jnp.exp(sc - m)
    den = jnp.sum(p, axis=-1, keepdims=True)
    return m, p, den


def _stack_heads(q):
    lane = lax.broadcasted_iota(jnp.int32, q.shape, 1)
    zero = jnp.zeros_like(q)
    return jnp.concatenate([jnp.where(lane < HEAD_DIM, q, zero),
                            jnp.where(lane < HEAD_DIM, zero, q)], axis=0)


def _unstack_heads(o, n):
    lane = lax.broadcasted_iota(jnp.int32, (n, 2 * HEAD_DIM), 1)
    return jnp.where(lane < HEAD_DIM, o[0:n], o[n:2 * n])


def _dilated_kernel(qk_ref, vg_ref, cos_ref, sin_ref, o_ref, qf, kf, vf, acc_o, acc_l):
    s = SEQ
    lane2 = lax.broadcasted_iota(jnp.int32, (s, 2 * HEAD_DIM), 1)
    first_half = (lane2 % HEAD_DIM) < (HEAD_DIM // 2)
    cos = cos_ref[...]
    sin = sin_ref[...]
    for pair in range(2):
        lo = pair * 2 * HEAD_DIM
        for src, dst, scale in ((0, qf, HEAD_DIM ** -0.5), (GROUP_W, kf, 1.0)):
            x = qk_ref[:, src + lo:src + lo + 2 * HEAD_DIM].astype(F32)
            swapped = jnp.where(first_half,
                                pltpu.roll(x, 2 * HEAD_DIM - HEAD_DIM // 2, 1),
                                pltpu.roll(x, HEAD_DIM // 2, 1))
            dst[pair] = (x * cos + swapped * sin) * scale
        vf[pair] = vg_ref[:, lo:lo + 2 * HEAD_DIM].astype(F32)

    for pi, (window, d) in enumerate(DILATED_PATTERNS):
        n_sub = s // d
        kw = min(2 * DIL_Q, n_sub)
        nb = n_sub // DIL_Q
        q_pos = lax.broadcasted_iota(jnp.int32, (2 * DIL_Q, kw), 0) % DIL_Q
        k_pos = lax.broadcasted_iota(jnp.int32, (2 * DIL_Q, kw), 1)
        rel0 = k_pos - q_pos

        def rows(rho, start, size, d=d):
            if d == 1:
                return pl.ds(pl.multiple_of(start, DIL_R), size)
            return pl.ds(rho + d * start, size, stride=d)

        def block(it, carry, d=d, n_sub=n_sub, kw=kw, nb=nb, rel0=rel0, pi=pi, rows=rows):
            rho = it // nb
            q0 = (it % nb) * DIL_Q
            ks = jnp.clip(q0 - DIL_R, 0, n_sub - kw)
            ok = jnp.abs(rel0 + (ks - q0)) <= DIL_R
            q_rows = rows(rho, q0, DIL_Q)
            k_rows = rows(rho, ks, kw)
            for pair in range(2):
                q = _stack_heads(qf[pair, q_rows, :].astype(BF16))
                k = kf[pair, k_rows, :].astype(BF16)
                v = vf[pair, k_rows, :].astype(BF16)
                sc = lax.dot_general(q, k, (((1,), (1,)), ((), ())), preferred_element_type=F32)
                sc = jnp.where(ok, sc, NEG)
                m, p, den = _softmax_rows(sc)
                o = jnp.dot(p.astype(BF16), v, preferred_element_type=F32) / den
                lse = jnp.broadcast_to(m + jnp.log(den), (2 * DIL_Q, 2 * HEAD_DIM))
                acc_o[pi, pair, q_rows, :] = _unstack_heads(o, DIL_Q)
                acc_l[pi, pair, q_rows, :] = _unstack_heads(lse, DIL_Q)
            return carry

        lax.fori_loop(0, d * nb, block, 0)

    ch = 256
    for c in range(s // ch):
        r = slice(c * ch, (c + 1) * ch)
        for pair in range(2):
            lo = pair * 2 * HEAD_DIM
            l0, l1, l2 = acc_l[0, pair, r, :], acc_l[1, pair, r, :], acc_l[2, pair, r, :]
            top = jnp.maximum(jnp.maximum(l0, l1), l2)
            w0, w1, w2 = jnp.exp(l0 - top), jnp.exp(l1 - top), jnp.exp(l2 - top)
            out = ((w0 * acc_o[0, pair, r, :] + w1 * acc_o[1, pair, r, :] + w2 * acc_o[2, pair, r, :])
                   / (w0 + w1 + w2))
            gate = vg_ref[r, GROUP_W + lo:GROUP_W + lo + 2 * HEAD_DIM].astype(F32)
            o_ref[r, lo:lo + 2 * HEAD_DIM] = (out * _silu(gate)).astype(BF16)


def _dilated(z3, cos_t, sin_t):
    b = z3.shape[0]
    return pl.pallas_call(
        _dilated_kernel,
        grid=(b,),
        in_specs=[
            pl.BlockSpec((None, SEQ, 2 * GROUP_W), lambda i: (i, 0, 1)),
            pl.BlockSpec((None, SEQ, 2 * GROUP_W), lambda i: (i, 0, 2)),
            pl.BlockSpec((SEQ, 2 * HEAD_DIM), lambda i: (0, 0)),
            pl.BlockSpec((SEQ, 2 * HEAD_DIM), lambda i: (0, 0)),
        ],
        out_specs=pl.BlockSpec((None, SEQ, GROUP_W), lambda i: (i, 0, 0)),
        out_shape=jax.ShapeDtypeStruct((b, SEQ, GROUP_W), BF16),
        scratch_shapes=[pltpu.VMEM((2, SEQ, 2 * HEAD_DIM), F32) for _ in range(3)]
        + [pltpu.VMEM((len(DILATED_PATTERNS), 2, SEQ, 2 * HEAD_DIM), F32) for _ in range(2)],
        compiler_params=_params(),
        name="dilated_attention",
    )(z3, z3, cos_t, sin_t)


def _na_bias_table(rpb):
    c = np.arange(GRID_W)
    cs = np.clip(c - NA_COLS // 2, 0, GRID_W - NA_COLS)
    col_ok = (c[None, :] >= cs[:, None]) & (c[None, :] < cs[:, None] + NA_COLS)
    dcol = np.clip(c[None, :] - c[:, None] + NA_COLS - 1, 0, 2 * NA_COLS - 2)
    off = np.arange(NA_ROWS)
    drow = np.arange(NA_ROWS)[None, :] - off[:, None] + NA_ROWS - 1
    tab = rpb.astype(F32)[:, drow[:, None, :, None], dcol[None, :, None, :]]
    tab = jnp.where(col_ok[None, None, :, None, :], tab, NEG)
    h = rpb.shape[0]
    tab = tab.reshape(h // 2, 2, NA_ROWS, GRID_W, NA_ROWS * GRID_W)
    return tab.transpose(0, 2, 1, 3, 4).reshape(h // 2, NA_ROWS, 2 * GRID_W, NA_ROWS * GRID_W)


def _na_kernel(qk_ref, vg_ref, bias_ref, o_ref):
    n_rows = SEQ // GRID_W
    nk = NA_ROWS * GRID_W

    def row(r, carry):
        rs = jnp.clip(r - NA_ROWS // 2, 0, n_rows - NA_ROWS)
        off = r - rs
        q_rows = pl.ds(pl.multiple_of(r * GRID_W, GRID_W), GRID_W)
        k_rows = pl.ds(pl.multiple_of(rs * GRID_W, GRID_W), nk)
        for pair in range(2):
            lo = pair * 2 * HEAD_DIM
            q = qk_ref[q_rows, lo:lo + 2 * HEAD_DIM]
            q = _stack_heads(q * jnp.asarray(HEAD_DIM ** -0.5, BF16))
            k = qk_ref[k_rows, GROUP_W + lo:GROUP_W + lo + 2 * HEAD_DIM]
            v = vg_ref[k_rows, lo:lo + 2 * HEAD_DIM]
            sc = lax.dot_general(q, k, (((1,), (1,)), ((), ())), preferred_element_type=F32)
            sc = sc + bias_ref[pair, off]
            _, p, den = _softmax_rows(sc)
            o = jnp.dot(p.astype(BF16), v, preferred_element_type=F32) / den
            gate = vg_ref[q_rows, GROUP_W + lo:GROUP_W + lo + 2 * HEAD_DIM].astype(F32)
            o_ref[q_rows, lo:lo + 2 * HEAD_DIM] = (_unstack_heads(o, GRID_W) * _silu(gate)).astype(BF16)
        return carry

    lax.fori_loop(0, n_rows, row, 0)


def _neighbourhood(z3, bias_tab):
    b = z3.shape[0]
    return pl.pallas_call(
        _na_kernel,
        grid=(b,),
        in_specs=[
            pl.BlockSpec((None, SEQ, 2 * GROUP_W), lambda i: (i, 0, 4)),
            pl.BlockSpec((None, SEQ, 2 * GROUP_W), lambda i: (i, 0, 5)),
            pl.BlockSpec(bias_tab.shape, lambda i: (0, 0, 0, 0)),
        ],
        out_specs=pl.BlockSpec((None, SEQ, GROUP_W), lambda i: (i, 0, 0)),
        out_shape=jax.ShapeDtypeStruct((b, SEQ, GROUP_W), BF16),
        compiler_params=_params(),
        name="neighbourhood_attention",
    )(z3, z3, bias_tab)


def _ssm_weights(a_re, a_im, log_dt, b_re, b_im, c_re, c_im):
    L, G, P, C = SSM_L, SSM_GROUPS, SSM_STATE, SSM_CH
    a_re, a_im, log_dt = a_re.astype(F32), a_im.astype(F32), log_dt.astype(F32)
    dt = jnp.exp(log_dt)[..., None]
    ks = jnp.arange(L + 1, dtype=F32)
    mag = jnp.exp((a_re * dt)[..., None] * ks)
    ang = (a_im * dt)[..., None] * ks
    pr, pim = mag * jnp.cos(ang), mag * jnp.sin(ang)
    abr, abi = pr[..., 1], pim[..., 1]
    den = a_re * a_re + a_im * a_im
    gr = ((abr - 1.0) * a_re + abi * a_im) / den
    gi = (abi * a_re - (abr - 1.0) * a_im) / den
    br, bi = b_re.astype(F32), b_im.astype(F32)
    bbr = gr[..., None] * br - gi[..., None] * bi
    bbi = gr[..., None] * bi + gi[..., None] * br
    er = pr[..., None] * bbr[..., None, :] - pim[..., None] * bbi[..., None, :]
    ei = pr[..., None] * bbi[..., None, :] + pim[..., None] * bbr[..., None, :]
    cr, ci = c_re.astype(F32), c_im.astype(F32)
    kern = (jnp.einsum('dgcp,dgpkx->dgkcx', cr, er, precision=HI)
            - jnp.einsum('dgcp,dgpkx->dgkcx', ci, ei, precision=HI))
    kf, kb = kern[0], kern[1]
    k_all = jnp.concatenate([kb[:, 1:L][:, ::-1], (kf[:, 0] + kb[:, 0])[:, None], kf[:, 1:L]], axis=1)
    lag = np.arange(L)[None, :] - np.arange(L)[:, None] + L - 1
    w_t = k_all[:, lag]
    w_t = w_t.transpose(0, 1, 4, 2, 3).reshape(G, L * C, L * C)
    kf_idx = L - 1 - np.arange(L)
    kb_idx = np.arange(L)

    def s_cols(e, d, idx):
        return e[d][:, :, idx, :].transpose(0, 2, 3, 1)

    w_s = jnp.concatenate([s_cols(er, 0, kf_idx), s_cols(er, 1, kb_idx),
                           s_cols(ei, 0, kf_idx), s_cols(ei, 1, kb_idx)], axis=-1)
    w_s = w_s.reshape(G, L * C, 4 * P)
    fr = cr[..., None] * pr[:, :, None] - ci[..., None] * pim[:, :, None]
    fi = cr[..., None] * pim[:, :, None] + ci[..., None] * pr[:, :, None]
    cf_idx = np.arange(L) + 1
    cb_idx = L - np.arange(L)

    def c_rows(f, d, idx):
        return f[d][..., idx].transpose(0, 2, 3, 1)

    w_c = jnp.concatenate([c_rows(fr, 0, cf_idx), c_rows(fr, 1, cb_idx),
                           -c_rows(fi, 0, cf_idx), -c_rows(fi, 1, cb_idx)], axis=1)
    w_c = w_c.reshape(G, 4 * P, L * C)
    a_l = jnp.concatenate([pr[0, ..., L], pr[1, ..., L], pim[0, ..., L], pim[1, ..., L]], axis=-1)
    a_l = jnp.broadcast_to(a_l[:, None, :], (G, SSM_NB, 4 * P))
    return w_t.astype(BF16), w_s.astype(BF16), w_c.astype(BF16), a_l


def _ssm_kernel(u_ref, wt_ref, ws_ref, wc_ref, al_ref, y_ref, s_scr, h_scr):
    nc = SEQ // SSM_L
    nb = SSM_NB
    half = 2 * SSM_STATE
    u = u_ref[...]
    s_scr[...] = jnp.dot(u, ws_ref[...], preferred_element_type=F32)
    a_r = al_ref[:, 0:half]
    a_i = al_ref[:, half:2 * half]
    fwd = lax.broadcasted_iota(jnp.int32, (nb, half), 1) < SSM_STATE
    h_r = jnp.zeros((nb, half), F32)
    h_i = jnp.zeros((nb, half), F32)
    for k in range(nc):
        rf = slice(k * nb, (k + 1) * nb)
        rb = slice((nc - 1 - k) * nb, (nc - k) * nb)
        h_scr[rf, :] = jnp.concatenate([h_r, h_i], axis=1)
        s_r = jnp.where(fwd, s_scr[rf, 0:half], s_scr[rb, 0:half])
        s_i = jnp.where(fwd, s_scr[rf, half:2 * half], s_scr[rb, half:2 * half])
        h_r, h_i = a_r * h_r - a_i * h_i + s_r, a_r * h_i + a_i * h_r + s_i
    fwd2 = (lax.broadcasted_iota(jnp.int32, (nb, 2 * half), 1) % half) < SSM_STATE
    for k in range(nc // 2):
        rf = slice(k * nb, (k + 1) * nb)
        rb = slice((nc - 1 - k) * nb, (nc - k) * nb)
        top, bot = h_scr[rf, :], h_scr[rb, :]
        h_scr[rf, :] = jnp.where(fwd2, top, bot)
        h_scr[rb, :] = jnp.where(fwd2, bot, top)
    y = jnp.dot(u, wt_ref[...], preferred_element_type=F32)
    y = y + jnp.dot(h_scr[...].astype(BF16), wc_ref[...], preferred_element_type=F32)
    y_ref[...] = y


def _ssm_core(u3, w_t, w_s, w_c, a_l):
    b = u3.shape[0]
    L, G, C = SSM_L, SSM_GROUPS, SSM_CH
    nc = SEQ // L
    nblk = b // SSM_NB
    rows = nc * SSM_NB
    ug = u3.reshape(nblk, SSM_NB, nc, L, G, C).transpose(4, 0, 2, 1, 3, 5).reshape(G, nblk, rows, L * C)
    yg = pl.pallas_call(
        _ssm_kernel,
        grid=(G, nblk),
        in_specs=[
            pl.BlockSpec((None, None, rows, L * C), lambda g, i: (g, i, 0, 0)),
            pl.BlockSpec((None, L * C, L * C), lambda g, i: (g, 0, 0)),
            pl.BlockSpec((None, L * C, 4 * SSM_STATE), lambda g, i: (g, 0, 0)),
            pl.BlockSpec((None, 4 * SSM_STATE, L * C), lambda g, i: (g, 0, 0)),
            pl.BlockSpec((None, SSM_NB, 4 * SSM_STATE), lambda g, i: (g, 0, 0)),
        ],
        out_specs=pl.BlockSpec((None, None, rows, L * C), lambda g, i: (g, i, 0, 0)),
        out_shape=jax.ShapeDtypeStruct((G, nblk, rows, L * C), F32),
        scratch_shapes=[pltpu.VMEM((rows, 4 * SSM_STATE), F32) for _ in range(2)],
        compiler_params=_params(2),
        name="ssm_core",
    )(ug, w_t, w_s, w_c, a_l)
    y = yg.reshape(G, nblk, nc, SSM_NB, L, C).transpose(1, 3, 2, 4, 0, 5)
    return y.reshape(b, SEQ, G * C)


def _out_proj_kernel(x_ref, ya_ref, yb_ref, yc_ref, yd_ref, z_ref, d_ref, gw_ref, gb_ref,
                     w_ref, fg_ref, o_ref, *, final):
    u = z_ref[:, 0:GROUP_W].astype(F32)
    y = u * d_ref[...] + yc_ref[...]
    g = _gelu_tanh(y)
    lin = jnp.dot(g.astype(BF16), gw_ref[...], preferred_element_type=F32) + gb_ref[...]
    g = g * _sigmoid(lin)
    y_c = (g * _silu(z_ref[:, GROUP_W:2 * GROUP_W].astype(F32))).astype(BF16)
    y_all = jnp.concatenate([ya_ref[...], yb_ref[...], y_c, yd_ref[...]], axis=1)
    x = x_ref[...] + jnp.dot(y_all, w_ref[...], preferred_element_type=F32)
    if final:
        ms = jnp.mean(x * x, axis=-1, keepdims=True)
        x = x * lax.rsqrt(ms + EPS) * fg_ref[...]
    o_ref[...] = x


def _out_proj(x2, ya, yb, yc, yd, z2, d_skip, glu_w, glu_b, w_out, final_g, final):
    rows = x2.shape[0]
    tile = lambda w: pl.BlockSpec((ROW_TILE, w), lambda i: (i, 0))
    const = lambda r, c: pl.BlockSpec((r, c), lambda i: (0, 0))
    return pl.pallas_call(
        functools.partial(_out_proj_kernel, final=final),
        grid=(rows // ROW_TILE,),
        in_specs=[
            tile(D_MODEL), tile(GROUP_W), tile(GROUP_W), tile(GROUP_W), tile(GROUP_W),
            pl.BlockSpec((ROW_TILE, 2 * GROUP_W), lambda i: (i, 3)),
            const(1, GROUP_W), const(GROUP_W, GROUP_W), const(1, GROUP_W),
            const(D_MODEL, D_MODEL), const(1, D_MODEL),
        ],
        out_specs=tile(D_MODEL),
        out_shape=jax.ShapeDtypeStruct((rows, D_MODEL), F32),
        compiler_params=_params(),
        name="out_proj",
    )(x2, ya, yb, yc, yd, z2, d_skip, glu_w, glu_b, w_out, final_g)


def _block_diag(w):
    n, k = w.shape[0], w.shape[1]
    out = jnp.zeros((n * k, n * k), w.dtype)
    for i in range(n):
        out = lax.dynamic_update_slice(out, w[i], (i * k, i * k))
    return out


def kernel(x_prompt, x_sample, norm_g, w_in, w_out, pool_w, pool_scale, ssm_a_re, ssm_a_im,
           ssm_log_dt, ssm_b_re, ssm_b_im, ssm_c_re, ssm_c_im, ssm_d, glu_w, glu_b, na_rpb, final_g):
    inv_cnt = _pool_inv_count()
    cos_t, sin_t = _rope_tables()
    layers = []
    for l in range(DEPTH):
        layers.append(dict(
            norm_g=norm_g[l].reshape(1, D_MODEL).astype(F32),
            w_in=w_in[l].astype(BF16),
            w_out=w_out[l].astype(BF16),
            pool_w=_block_diag(pool_w[l]).astype(BF16),
            pool_scale=pool_scale[l].reshape(1, GROUP_W).astype(F32),
            ssm=_ssm_weights(ssm_a_re[l], ssm_a_im[l], ssm_log_dt[l], ssm_b_re[l], ssm_b_im[l],
                             ssm_c_re[l], ssm_c_im[l]),
            ssm_d=ssm_d[l].reshape(1, GROUP_W).astype(F32),
            glu_w=glu_w[l].astype(BF16),
            glu_b=glu_b[l].reshape(1, GROUP_W).astype(F32),
            na_bias=_na_bias_table(na_rpb[l]),
        ))
    fg = final_g.reshape(1, D_MODEL).astype(F32)

    def trunk(x):
        b = x.shape[0]
        x2 = x.reshape(b * SEQ, D_MODEL)
        for l, p in enumerate(layers):
            z2 = _in_proj(x2, p['norm_g'], p['w_in'])
            z3 = z2.reshape(b, SEQ, PROJ_W)
            y_a = _pool(z3, inv_cnt, p['pool_w'], p['pool_scale'])
            y_b = _dilated(z3, cos_t, sin_t)
            y_c = _ssm_core(z3[:, :, 6 * GROUP_W:7 * GROUP_W], *p['ssm'])
            y_d = _neighbourhood(z3, p['na_bias'])
            flat = lambda a: a.reshape(b * SEQ, GROUP_W)
            x2 = _out_proj(x2, flat(y_a), flat(y_b), flat(y_c), flat(y_d), z2, p['ssm_d'],
                           p['glu_w'], p['glu_b'], p['w_out'], fg, final=(l == DEPTH - 1))
        return x2.reshape(b, SEQ, D_MODEL)

    return (trunk(x_prompt), trunk(x_sample))
```

```python
import functools
import math

import numpy as np
import jax
import jax.numpy as jnp
from jax import lax
from jax.experimental import pallas as pl
from jax.experimental.pallas import tpu as pltpu

D_MODEL = 1024
SEQ = 2048
DEPTH = 4
GROUP_W = 256
HEAD_DIM = 64
POOL_WINDOWS = (2, 4, 8, 16)
POOL_GROUP = 64
DILATED_PATTERNS = ((128, 1), (512, 4), (2048, 16))
SSM_CH = 16
SSM_GROUPS = 16
SSM_STATE = 64
GRID_W = 64
NA_ROWS = 8
NA_COLS = 16
ROPE_THETA = 10000.0
EPS = 1e-6
PROJ_W = 12 * GROUP_W
NEG = -1e30

F32 = jnp.float32
BF16 = jnp.bfloat16
HI = lax.Precision.HIGHEST

ROW_TILE = 512
POOL_PAD = 16
DIL_Q = 128
DIL_R = 64
STRIP = 32
ATT_UNROLL = 4
LOG2E = math.log2(math.e)
LN2 = math.log(2.0)
SSM_L = 128
SSM_NB_MAX = 16
SSM_PITCH = 24
TOK_W = 10 * GROUP_W
VMEM_LIMIT = 56 * 1024 * 1024


def _sigmoid(x):
    return 1.0 / (1.0 + jnp.exp(-x))


def _silu(x):
    return x * _sigmoid(x)


def _gelu_tanh(x):
    return 0.5 * x * (1.0 + jnp.tanh(math.sqrt(2.0 / math.pi) * (x + 0.044715 * (x * x * x))))


def _toeplitz(k, axis, a, b):
    k = jnp.moveaxis(k, axis, -1)
    w = a + b - 1
    lead = k.shape[:-1]
    kp = jnp.concatenate([k, jnp.zeros(lead + (1,), k.dtype)], axis=-1)
    t = jnp.broadcast_to(kp[..., None, :], lead + (a, w + 1)).reshape(lead + (a * (w + 1),))
    t = t[..., :a * w].reshape(lead + (a, w))[..., a - 1:a - 1 + b]
    return jnp.moveaxis(t, (-2, -1), (axis, axis + 1))


def _params(n_parallel=1, n_arbitrary=0):
    sem = ("parallel",) * n_parallel + ("arbitrary",) * n_arbitrary
    return pltpu.CompilerParams(dimension_semantics=sem, vmem_limit_bytes=VMEM_LIMIT)


def _in_proj_kernel(x_ref, g_ref, w_ref, wt_ref, z_ref, zt_ref):
    x = x_ref[...]
    ms = jnp.mean(x * x, axis=-1, keepdims=True)
    h = (x * lax.rsqrt(ms + EPS) * g_ref[...]).astype(BF16)
    nch = 512
    for n in range(TOK_W // nch):
        z_ref[:, n * nch:(n + 1) * nch] = jnp.dot(
            h, w_ref[:, n * nch:(n + 1) * nch], preferred_element_type=F32).astype(BF16)
    zt_ref[...] = lax.dot_general(wt_ref[...], h, (((1,), (1,)), ((), ())),
                                  preferred_element_type=F32).astype(BF16)


def _in_proj(x2, g, w_tok, w_ssm_t):
    rows = x2.shape[0]
    per_seq = SEQ // ROW_TILE
    return pl.pallas_call(
        _in_proj_kernel,
        grid=(rows // ROW_TILE,),
        in_specs=[
            pl.BlockSpec((ROW_TILE, D_MODEL), lambda i: (i, 0)),
            pl.BlockSpec((1, D_MODEL), lambda i: (0, 0)),
            pl.BlockSpec((D_MODEL, TOK_W), lambda i: (0, 0)),
            pl.BlockSpec((2 * GROUP_W, D_MODEL), lambda i: (0, 0)),
        ],
        out_specs=[
            pl.BlockSpec((ROW_TILE, TOK_W), lambda i: (i, 0)),
            pl.BlockSpec((None, 2 * GROUP_W, ROW_TILE), lambda i: (i // per_seq, 0, i % per_seq)),
        ],
        out_shape=[jax.ShapeDtypeStruct((rows, TOK_W), BF16),
                   jax.ShapeDtypeStruct((rows // SEQ, 2 * GROUP_W, SEQ), BF16)],
        compiler_params=_params(),
        name="in_proj",
    )(x2, g, w_tok, w_ssm_t)


def _pool_kernel(z_ref, inv_ref, w_ref, scale_ref, o_ref, ub, s2b, s4b, s8b):
    s = SEQ
    p = POOL_PAD
    n = s + p
    u = z_ref[:, 0:GROUP_W].astype(F32)
    zero_p = jnp.zeros((p, GROUP_W), F32)
    zero_8 = jnp.zeros((8, GROUP_W), F32)
    ub[0:p, :] = zero_p
    ub[p:p + s, :] = u
    ub[p + s:p + s + p, :] = zero_p
    for buf in (s2b, s4b, s8b):
        buf[0:8, :] = zero_8
        buf[s + 24:s + 32, :] = zero_8
    s2b[8:8 + n, :] = ub[7:7 + n, :] + ub[8:8 + n, :]
    s4b[8:8 + n, :] = s2b[7:7 + n, :] + s2b[9:9 + n, :]
    s8b[8:8 + n, :] = s4b[6:6 + n, :] + s4b[10:10 + n, :]
    s16 = s8b[p - 4:p - 4 + s, :] + s8b[p + 4:p + 4 + s, :]
    lane = lax.broadcasted_iota(jnp.int32, (s, GROUP_W), 1)
    win = jnp.where(lane < 64, s2b[p:p + s, :],
                    jnp.where(lane < 128, s4b[p:p + s, :],
                              jnp.where(lane < 192, s8b[p:p + s, :], s16)))
    diff = (win * inv_ref[...] - u).astype(BF16)
    y = jnp.dot(diff, w_ref[...], preferred_element_type=F32) * scale_ref[...]
    gate = z_ref[:, GROUP_W:2 * GROUP_W].astype(F32)
    o_ref[...] = (y * _silu(gate)).astype(BF16)


def _pool_inv_count():
    t = np.arange(SEQ)
    cols = []
    for w in POOL_WINDOWS:
        lo = np.clip(t - w // 2, 0, SEQ)
        hi = np.clip(t + w // 2, 0, SEQ)
        cols.append(np.repeat((1.0 / (hi - lo))[:, None], POOL_GROUP, axis=1))
    return jnp.asarray(np.concatenate(cols, axis=1), F32)


def _pool(z3, inv_cnt, w_bd, scale):
    b = z3.shape[0]
    return pl.pallas_call(
        _pool_kernel,
        grid=(b,),
        in_specs=[
            pl.BlockSpec((None, SEQ, 2 * GROUP_W), lambda i: (i, 0, 0)),
            pl.BlockSpec((SEQ, GROUP_W), lambda i: (0, 0)),
            pl.BlockSpec((GROUP_W, GROUP_W), lambda i: (0, 0)),
            pl.BlockSpec((1, GROUP_W), lambda i: (0, 0)),
        ],
        out_specs=pl.BlockSpec((None, SEQ, GROUP_W), lambda i: (i, 0, 0)),
        out_shape=jax.ShapeDtypeStruct((b, SEQ, GROUP_W), BF16),
        scratch_shapes=[pltpu.VMEM((SEQ + 2 * POOL_PAD, GROUP_W), F32) for _ in range(4)],
        compiler_params=_params(),
        name="pool_mixer",
    )(z3, inv_cnt, w_bd, scale)


def _rope_tables():
    inv = ROPE_THETA ** (-np.arange(0, HEAD_DIM, 2, dtype=np.float64) / HEAD_DIM)
    ang = np.arange(SEQ, dtype=np.float64)[:, None] * inv[None, :]
    cos = np.concatenate([np.cos(ang), np.cos(ang)], axis=1)
    sin = np.concatenate([-np.sin(ang), np.sin(ang)], axis=1)
    return (jnp.asarray(np.tile(cos, (1, 2)), F32), jnp.asarray(np.tile(sin, (1, 2)), F32))


def _stack_heads(q):
    lane = lax.broadcasted_iota(jnp.int32, q.shape, 1)
    zero = jnp.zeros_like(q)
    return jnp.concatenate([jnp.where(lane < HEAD_DIM, q, zero),
                            jnp.where(lane < HEAD_DIM, zero, q)], axis=0)


def _unstack_heads(o, n):
    lane = lax.broadcasted_iota(jnp.int32, (n, 2 * HEAD_DIM), 1)
    return jnp.where(lane < HEAD_DIM, o[0:n], o[n:2 * n])


def _dilated_mask_table():
    q = np.arange(DIL_Q)[None, :, None]
    k = np.arange(2 * DIL_Q)[None, None, :]
    shift = (np.arange(3) * DIL_R)[:, None, None]
    return jnp.asarray(np.where(np.abs(k - q - shift) <= DIL_R, 0.0, NEG), F32)


def _dilated_kernel(qk_ref, vg_ref, cos_ref, sin_ref, mask_ref, o_ref, qf, kf, vf, acc_o, acc_l,
                    s_scr, p_scr, m_scr):
    s = SEQ
    hd2 = 2 * HEAD_DIM
    lane2 = lax.broadcasted_iota(jnp.int32, (s, 2 * HEAD_DIM), 1)
    first_half = (lane2 % HEAD_DIM) < (HEAD_DIM // 2)
    cos = cos_ref[...]
    sin = sin_ref[...]
    for pair in range(2):
        lo = pair * 2 * HEAD_DIM
        for src, dst, scale in ((0, qf, HEAD_DIM ** -0.5 * LOG2E), (GROUP_W, kf, 1.0)):
            x = qk_ref[:, src + lo:src + lo + 2 * HEAD_DIM].astype(F32)
            swapped = jnp.where(first_half,
                                pltpu.roll(x, 2 * HEAD_DIM - HEAD_DIM // 2, 1),
                                pltpu.roll(x, HEAD_DIM // 2, 1))
            dst[pair] = (x * cos + swapped * sin) * scale
        vf[pair] = vg_ref[:, lo:lo + 2 * HEAD_DIM].astype(F32)

    for pi, (window, d) in enumerate(DILATED_PATTERNS):
        n_sub = s // d
        kw = min(2 * DIL_Q, n_sub)
        nb = n_sub // DIL_Q

        def rows(rho, start, size, d=d):
            if d == 1:
                return pl.ds(pl.multiple_of(start, DIL_R), size)
            return pl.ds(rho + d * start, size, stride=d)

        def block(trip, carry, d=d, n_sub=n_sub, kw=kw, nb=nb, pi=pi, rows=rows):
            for sub in range(ATT_UNROLL):
                it = trip * ATT_UNROLL + sub
                rho = it // nb
                q0 = (it % nb) * DIL_Q
                ks = jnp.clip(q0 - DIL_R, 0, n_sub - kw)
                case = (q0 - ks) // DIL_R
                q_rows = rows(rho, q0, DIL_Q)
                k_rows = rows(rho, ks, kw)
                for pair in range(2):
                    slot = 2 * sub + pair
                    q = _stack_heads(qf[pair, q_rows, :].astype(BF16))
                    k = kf[pair, k_rows, :].astype(BF16)
                    v = _with_ones(vf[pair, k_rows, :].astype(BF16))
                    s_scr[slot, :, 0:kw] = lax.dot_general(q, k, (((1,), (1,)), ((), ())),
                                                           preferred_element_type=F32)
                    _softmax_strips(
                        s_scr.at[slot], p_scr.at[slot],
                        lambda r0, r1, kw=kw, case=case: mask_ref[
                            case, r0 % DIL_Q:r0 % DIL_Q + (r1 - r0), 0:kw],
                        2 * DIL_Q, kw, m_scr.at[slot])
                    oa = jnp.dot(p_scr[slot, :, 0:kw], v, preferred_element_type=F32)
                    den = oa[:, hd2:2 * hd2]
                    o = oa[:, 0:hd2] / den
                    lse = (m_scr[slot] + jnp.log2(den)) * LN2
                    acc_o[pi, pair, q_rows, :] = _unstack_heads(o, DIL_Q)
                    acc_l[pi, pair, q_rows, :] = _unstack_heads(lse, DIL_Q)
            return carry

        lax.fori_loop(0, d * nb // ATT_UNROLL, block, 0)

    ch = 256
    for c in range(s // ch):
        r = slice(c * ch, (c + 1) * ch)
        for pair in range(2):
            lo = pair * 2 * HEAD_DIM
            l0, l1, l2 = acc_l[0, pair, r, :], acc_l[1, pair, r, :], acc_l[2, pair, r, :]
            top = jnp.maximum(jnp.maximum(l0, l1), l2)
            w0, w1, w2 = jnp.exp(l0 - top), jnp.exp(l1 - top), jnp.exp(l2 - top)
            out = ((w0 * acc_o[0, pair, r, :] + w1 * acc_o[1, pair, r, :] + w2 * acc_o[2, pair, r, :])
                   / (w0 + w1 + w2))
            gate = vg_ref[r, GROUP_W + lo:GROUP_W + lo + 2 * HEAD_DIM].astype(F32)
            o_ref[r, lo:lo + 2 * HEAD_DIM] = (out * _silu(gate)).astype(BF16)


def _dilated(z3, cos_t, sin_t, mask_t):
    b = z3.shape[0]
    return pl.pallas_call(
        _dilated_kernel,
        grid=(b,),
        in_specs=[
            pl.BlockSpec((None, SEQ, 2 * GROUP_W), lambda i: (i, 0, 1)),
            pl.BlockSpec((None, SEQ, 2 * GROUP_W), lambda i: (i, 0, 2)),
            pl.BlockSpec((SEQ, 2 * HEAD_DIM), lambda i: (0, 0)),
            pl.BlockSpec((SEQ, 2 * HEAD_DIM), lambda i: (0, 0)),
            pl.BlockSpec(mask_t.shape, lambda i: (0, 0, 0)),
        ],
        out_specs=pl.BlockSpec((None, SEQ, GROUP_W), lambda i: (i, 0, 0)),
        out_shape=jax.ShapeDtypeStruct((b, SEQ, GROUP_W), BF16),
        scratch_shapes=[pltpu.VMEM((2, SEQ, 2 * HEAD_DIM), F32) for _ in range(3)]
        + [pltpu.VMEM((len(DILATED_PATTERNS), 2, SEQ, 2 * HEAD_DIM), F32) for _ in range(2)]
        + [pltpu.VMEM((2 * ATT_UNROLL, 2 * DIL_Q, 2 * DIL_Q), F32),
           pltpu.VMEM((2 * ATT_UNROLL, 2 * DIL_Q, 2 * DIL_Q), BF16),
           pltpu.VMEM((2 * ATT_UNROLL, 2 * DIL_Q, 2 * HEAD_DIM), F32)],
        compiler_params=_params(),
        name="dilated_attention",
    )(z3, z3, cos_t, sin_t, mask_t)


def _na_bias_table(rpb):
    c = np.arange(GRID_W)
    cs = np.clip(c - NA_COLS // 2, 0, GRID_W - NA_COLS)
    col_ok = (c[None, :] >= cs[:, None]) & (c[None, :] < cs[:, None] + NA_COLS)
    pad = GRID_W - NA_COLS
    tab = jnp.pad(rpb.astype(F32), ((0, 0), (0, 0), (pad, pad)))
    tab = _toeplitz(tab, 1, NA_ROWS, NA_ROWS)
    tab = _toeplitz(tab, 3, GRID_W, GRID_W)
    tab = jnp.where(col_ok[None, None, None, :, :], tab * LOG2E, NEG)
    h = rpb.shape[0]
    tab = tab.reshape(h // 2, 2, NA_ROWS, NA_ROWS, GRID_W, GRID_W)
    return tab.transpose(0, 2, 1, 4, 3, 5).reshape(h // 2, NA_ROWS, 2 * GRID_W, NA_ROWS * GRID_W)


def _softmax_strips(s_scr, p_scr, bias, n_rows, n_keys, m_scr=None):
    for r0 in range(0, n_rows, STRIP):
        t = s_scr[r0:r0 + STRIP, 0:n_keys] + bias(r0, r0 + STRIP)
        m = jnp.max(t, axis=1, keepdims=True)
        p_scr[r0:r0 + STRIP, 0:n_keys] = jnp.exp2(t - m).astype(BF16)
        if m_scr is not None:
            m_scr[r0:r0 + STRIP, :] = jnp.broadcast_to(m, (STRIP, 2 * HEAD_DIM))


def _with_ones(v):
    return jnp.concatenate([v, jnp.ones_like(v)], axis=1)


def _na_kernel(qk_ref, vg_ref, bias_ref, o_ref, s_scr, p_scr):
    n_rows = SEQ // GRID_W
    nk = NA_ROWS * GRID_W
    hd2 = 2 * HEAD_DIM

    def rows_trip(trip, carry):
        for sub in range(ATT_UNROLL):
            r = trip * ATT_UNROLL + sub
            rs = jnp.clip(r - NA_ROWS // 2, 0, n_rows - NA_ROWS)
            off = r - rs
            q_rows = pl.ds(pl.multiple_of(r * GRID_W, GRID_W), GRID_W)
            k_rows = pl.ds(pl.multiple_of(rs * GRID_W, GRID_W), nk)
            for pair in range(2):
                slot = 2 * sub + pair
                lo = pair * hd2
                q = qk_ref[q_rows, lo:lo + hd2].astype(F32) * (HEAD_DIM ** -0.5 * LOG2E)
                q = _stack_heads(q.astype(BF16))
                k = qk_ref[k_rows, GROUP_W + lo:GROUP_W + lo + hd2]
                v = _with_ones(vg_ref[k_rows, lo:lo + hd2])
                s_scr[slot] = lax.dot_general(q, k, (((1,), (1,)), ((), ())), preferred_element_type=F32)
                _softmax_strips(s_scr.at[slot], p_scr.at[slot],
                                lambda r0, r1, pair=pair, off=off: bias_ref[pair, off, r0:r1, :],
                                2 * GRID_W, nk)
                oa = jnp.dot(p_scr[slot], v, preferred_element_type=F32)
                o = oa[:, 0:hd2] / oa[:, hd2:2 * hd2]
                gate = vg_ref[q_rows, GROUP_W + lo:GROUP_W + lo + hd2].astype(F32)
                o_ref[q_rows, lo:lo + hd2] = (_unstack_heads(o, GRID_W) * _silu(gate)).astype(BF16)
        return carry

    lax.fori_loop(0, n_rows // ATT_UNROLL, rows_trip, 0)


def _neighbourhood(z3, bias_tab):
    b = z3.shape[0]
    return pl.pallas_call(
        _na_kernel,
        grid=(b,),
        in_specs=[
            pl.BlockSpec((None, SEQ, 2 * GROUP_W), lambda i: (i, 0, 3)),
            pl.BlockSpec((None, SEQ, 2 * GROUP_W), lambda i: (i, 0, 4)),
            pl.BlockSpec(bias_tab.shape, lambda i: (0, 0, 0, 0)),
        ],
        out_specs=pl.BlockSpec((None, SEQ, GROUP_W), lambda i: (i, 0, 0)),
        out_shape=jax.ShapeDtypeStruct((b, SEQ, GROUP_W), BF16),
        scratch_shapes=[pltpu.VMEM((2 * ATT_UNROLL, 2 * GRID_W, NA_ROWS * GRID_W), F32),
                        pltpu.VMEM((2 * ATT_UNROLL, 2 * GRID_W, NA_ROWS * GRID_W), BF16)],
        compiler_params=_params(),
        name="neighbourhood_attention",
    )(z3, z3, bias_tab)


def _ssm_weights(a_re, a_im, log_dt, b_re, b_im, c_re, c_im):
    L, G, P, C = SSM_L, SSM_GROUPS, SSM_STATE, SSM_CH
    a_re, a_im, log_dt = a_re.astype(F32), a_im.astype(F32), log_dt.astype(F32)
    dt = jnp.exp(log_dt)[..., None]
    ks = jnp.arange(L + 1, dtype=F32)
    mag = jnp.exp((a_re * dt)[..., None] * ks)
    ang = (a_im * dt)[..., None] * ks
    pr, pim = mag * jnp.cos(ang), mag * jnp.sin(ang)
    abr, abi = pr[..., 1], pim[..., 1]
    den = a_re * a_re + a_im * a_im
    gr = ((abr - 1.0) * a_re + abi * a_im) / den
    gi = (abi * a_re - (abr - 1.0) * a_im) / den
    br, bi = b_re.astype(F32), b_im.astype(F32)
    bbr = gr[..., None] * br - gi[..., None] * bi
    bbi = gr[..., None] * bi + gi[..., None] * br
    er = pr[..., None] * bbr[..., None, :] - pim[..., None] * bbi[..., None, :]
    ei = pr[..., None] * bbi[..., None, :] + pim[..., None] * bbr[..., None, :]
    cr, ci = c_re.astype(F32), c_im.astype(F32)
    kern = (jnp.einsum('dgcp,dgpkx->dgxck', cr, er, precision=HI)
            - jnp.einsum('dgcp,dgpkx->dgxck', ci, ei, precision=HI))
    kf, kb = kern[0], kern[1]
    k_lag = jnp.concatenate([kb[..., 1:L][..., ::-1], (kf[..., 0] + kb[..., 0])[..., None],
                             kf[..., 1:L], jnp.zeros(kf.shape[:-1] + (1,), F32)], axis=-1)
    k_lag = k_lag.reshape(G, C * C, 2 * L)

    def state_in(d, idx):
        a_r = pr[d][:, :, idx].transpose(0, 2, 1)[:, None]
        a_i = pim[d][:, :, idx].transpose(0, 2, 1)[:, None]
        b_r = bbr[d].transpose(0, 2, 1)[:, :, None]
        b_i = bbi[d].transpose(0, 2, 1)[:, :, None]
        return a_r * b_r - a_i * b_i, a_r * b_i + a_i * b_r

    sf_r, sf_i = state_in(0, slice(L - 1, None, -1))
    sb_r, sb_i = state_in(1, slice(0, L))
    w_s = jnp.concatenate([sf_r, sb_r, sf_i, sb_i], axis=-1).reshape(G, C * L, 4 * P)

    def state_out(d, idx):
        c_r = cr[d].transpose(0, 2, 1)[..., None]
        c_i = ci[d].transpose(0, 2, 1)[..., None]
        a_r = pr[d][:, :, idx][:, :, None, :]
        a_i = pim[d][:, :, idx][:, :, None, :]
        return c_r * a_r - c_i * a_i, c_r * a_i + c_i * a_r

    ff_r, ff_i = state_out(0, slice(1, L + 1))
    fb_r, fb_i = state_out(1, slice(L, 0, -1))
    w_c = jnp.concatenate([ff_r, fb_r, -ff_i, -fb_i], axis=1).reshape(G, 4 * P, C * L)
    a_l = jnp.concatenate([pr[0, ..., L], pr[1, ..., L], pim[0, ..., L], pim[1, ..., L]], axis=-1)
    a_l = jnp.broadcast_to(a_l[:, None, :], (G, SSM_NB_MAX, 4 * P))
    return _ssm_toeplitz(k_lag), w_s.astype(BF16), w_c.astype(BF16), a_l


def _ssm_toeplitz_kernel(k_ref, t_ref):
    L, C = SSM_L, SSM_CH

    def build(cp, carry):
        for c in range(C):
            lag = jnp.broadcast_to(k_ref[pl.ds(cp * C + c, 1), :], (L, 2 * L))
            tile = pltpu.roll(lag, L + 1, 1, stride=1, stride_axis=0)[:, 0:L]
            t_ref[pl.ds(pl.multiple_of(cp * L, L), L), c * L:(c + 1) * L] = tile.astype(BF16)
        return carry

    lax.fori_loop(0, C, build, 0)


def _ssm_toeplitz(k_lag):
    G, C, L = SSM_GROUPS, SSM_CH, SSM_L
    return pl.pallas_call(
        _ssm_toeplitz_kernel,
        grid=(G,),
        in_specs=[pl.BlockSpec((None, C * C, 2 * L), lambda g: (g, 0, 0))],
        out_specs=pl.BlockSpec((None, C * L, C * L), lambda g: (g, 0, 0)),
        out_shape=jax.ShapeDtypeStruct((G, C * L, C * L), BF16),
        compiler_params=_params(),
        name="ssm_toeplitz",
    )(k_lag)


def _ssm_kernel(u_ref, t_ref, ws_ref, wc_ref, al_ref, y_ref, x_scr, y_scr,
                s_re, s_im, ha_re, ha_im, hb_re, hb_im, *, nb):
    L, C, P = SSM_L, SSM_CH, SSM_STATE
    nc = SEQ // L
    m = nb * nc
    half = 2 * P

    for b in range(nb):
        ub = u_ref[b].astype(F32)
        for j in range(nc):
            r0 = (b * nc + j) * SSM_PITCH
            x_scr[r0:r0 + C, :] = ub[:, j * L:(j + 1) * L]
    lhs = jnp.concatenate([x_scr[pl.ds(c, m, stride=SSM_PITCH), :].astype(BF16) for c in range(C)],
                          axis=1)

    s = jnp.dot(lhs, ws_ref[...], preferred_element_type=F32)
    s_re[...] = s[:, 0:half]
    s_im[...] = s[:, half:2 * half]
    a_r = al_ref[0:nb, 0:half]
    a_i = al_ref[0:nb, half:2 * half]
    fwd = lax.broadcasted_iota(jnp.int32, (nb, half), 1) < P
    h_r = jnp.zeros((nb, half), F32)
    h_i = jnp.zeros((nb, half), F32)
    for k in range(nc):
        rf = pl.ds(k, nb, stride=nc)
        rb = pl.ds(nc - 1 - k, nb, stride=nc)
        ha_re[rf, :] = h_r
        ha_im[rf, :] = h_i
        hb_re[rb, :] = h_r
        hb_im[rb, :] = h_i
        s_r = jnp.where(fwd, s_re[rf, :], s_re[rb, :])
        s_i = jnp.where(fwd, s_im[rf, :], s_im[rb, :])
        h_r, h_i = a_r * h_r - a_i * h_i + s_r, a_r * h_i + a_i * h_r + s_i
    fwd_m = lax.broadcasted_iota(jnp.int32, (m, half), 1) < P
    h_all = jnp.concatenate([jnp.where(fwd_m, ha_re[...], hb_re[...]),
                             jnp.where(fwd_m, ha_im[...], hb_im[...])], axis=1).astype(BF16)
    y = jnp.dot(lhs, t_ref[...], preferred_element_type=F32)
    y = y + jnp.dot(h_all, wc_ref[...], preferred_element_type=F32)
    for c in range(C):
        y_scr[pl.ds(c, m, stride=SSM_PITCH), :] = y[:, c * L:(c + 1) * L]
    for b in range(nb):
        y_ref[b] = jnp.concatenate(
            [y_scr[(b * nc + j) * SSM_PITCH:(b * nc + j) * SSM_PITCH + C, :] for j in range(nc)], axis=1)


def _ssm_core(zt, w_t, w_s, w_c, a_l):
    b = zt.shape[0]
    L, G, C, P = SSM_L, SSM_GROUPS, SSM_CH, SSM_STATE
    nb = min(b, SSM_NB_MAX)
    m = nb * (SEQ // L)
    return pl.pallas_call(
        functools.partial(_ssm_kernel, nb=nb),
        grid=(G, b // nb),
        in_specs=[
            pl.BlockSpec((nb, C, SEQ), lambda g, i: (i, g, 0)),
            pl.BlockSpec((None, C * L, C * L), lambda g, i: (g, 0, 0)),
            pl.BlockSpec((None, C * L, 4 * P), lambda g, i: (g, 0, 0)),
            pl.BlockSpec((None, 4 * P, C * L), lambda g, i: (g, 0, 0)),
            pl.BlockSpec((None, SSM_NB_MAX, 4 * P), lambda g, i: (g, 0, 0)),
        ],
        out_specs=pl.BlockSpec((nb, C, SEQ), lambda g, i: (i, g, 0)),
        out_shape=jax.ShapeDtypeStruct((b, G * C, SEQ), F32),
        scratch_shapes=[pltpu.VMEM((m * SSM_PITCH, L), F32), pltpu.VMEM((m * SSM_PITCH, L), F32)]
        + [pltpu.VMEM((m, 2 * P), F32) for _ in range(6)],
        compiler_params=_params(2),
        name="ssm_core",
    )(zt, w_t, w_s, w_c, a_l)


def _out_proj_kernel(x_ref, ya_ref, yb_ref, yd_ref, yct_ref, zt_ref, d_ref, gwt_ref, gb_ref,
                     wabd_ref, wc_ref, fg_ref, o_ref, *, final):
    u = zt_ref[0:GROUP_W, :].astype(F32)
    y = u * d_ref[...] + yct_ref[...]
    g = _gelu_tanh(y)
    lin = jnp.dot(gwt_ref[...], g.astype(BF16), preferred_element_type=F32) + gb_ref[...]
    g = g * _sigmoid(lin)
    y_c = (g * _silu(zt_ref[GROUP_W:2 * GROUP_W, :].astype(F32))).astype(BF16)
    y_abd = jnp.concatenate([ya_ref[...], yb_ref[...], yd_ref[...]], axis=1)
    acc = jnp.dot(y_abd, wabd_ref[...], preferred_element_type=F32)
    acc = acc + lax.dot_general(y_c, wc_ref[...], (((0,), (0,)), ((), ())),
                                preferred_element_type=F32)
    x = x_ref[...] + acc
    if final:
        ms = jnp.mean(x * x, axis=-1, keepdims=True)
        x = x * lax.rsqrt(ms + EPS) * fg_ref[...]
    o_ref[...] = x


def _out_proj(x2, ya, yb, yd, yct, zt, d_skip, glu_wt, glu_b, w_abd, w_c, final_g, final):
    rows = x2.shape[0]
    per_seq = SEQ // ROW_TILE
    tile = lambda w: pl.BlockSpec((ROW_TILE, w), lambda i: (i, 0))
    chan = lambda c: pl.BlockSpec((None, c, ROW_TILE), lambda i: (i // per_seq, 0, i % per_seq))
    const = lambda r, c: pl.BlockSpec((r, c), lambda i: (0, 0))
    return pl.pallas_call(
        functools.partial(_out_proj_kernel, final=final),
        grid=(rows // ROW_TILE,),
        in_specs=[
            tile(D_MODEL), tile(GROUP_W), tile(GROUP_W), tile(GROUP_W),
            chan(GROUP_W), chan(2 * GROUP_W),
            const(GROUP_W, 1), const(GROUP_W, GROUP_W), const(GROUP_W, 1),
            const(3 * GROUP_W, D_MODEL), const(GROUP_W, D_MODEL), const(1, D_MODEL),
        ],
        out_specs=tile(D_MODEL),
        out_shape=jax.ShapeDtypeStruct((rows, D_MODEL), F32),
        compiler_params=_params(),
        name="out_proj",
    )(x2, ya, yb, yd, yct, zt, d_skip, glu_wt, glu_b, w_abd, w_c, final_g)


def _block_diag(w):
    n, k = w.shape[0], w.shape[1]
    out = jnp.zeros((n * k, n * k), w.dtype)
    for i in range(n):
        out = lax.dynamic_update_slice(out, w[i], (i * k, i * k))
    return out


def kernel(x_prompt, x_sample, norm_g, w_in, w_out, pool_w, pool_scale, ssm_a_re, ssm_a_im,
           ssm_log_dt, ssm_b_re, ssm_b_im, ssm_c_re, ssm_c_im, ssm_d, glu_w, glu_b, na_rpb, final_g):
    inv_cnt = _pool_inv_count()
    cos_t, sin_t = _rope_tables()
    mask_t = _dilated_mask_table()
    layers = []
    for l in range(DEPTH):
        w_l, wo_l = w_in[l], w_out[l]
        layers.append(dict(
            norm_g=norm_g[l].reshape(1, D_MODEL).astype(F32),
            w_tok=jnp.concatenate([w_l[:, :6 * GROUP_W], w_l[:, 8 * GROUP_W:]], axis=1).astype(BF16),
            w_ssm_t=w_l[:, 6 * GROUP_W:8 * GROUP_W].T.astype(BF16),
            w_abd=jnp.concatenate([wo_l[:2 * GROUP_W], wo_l[3 * GROUP_W:]], axis=0).astype(BF16),
            w_c=wo_l[2 * GROUP_W:3 * GROUP_W].astype(BF16),
            pool_w=_block_diag(pool_w[l]).astype(BF16),
            pool_scale=pool_scale[l].reshape(1, GROUP_W).astype(F32),
            ssm=_ssm_weights(ssm_a_re[l], ssm_a_im[l], ssm_log_dt[l], ssm_b_re[l], ssm_b_im[l],
                             ssm_c_re[l], ssm_c_im[l]),
            ssm_d=ssm_d[l].reshape(GROUP_W, 1).astype(F32),
            glu_wt=glu_w[l].T.astype(BF16),
            glu_b=glu_b[l].reshape(GROUP_W, 1).astype(F32),
            na_bias=_na_bias_table(na_rpb[l]),
        ))
    fg = final_g.reshape(1, D_MODEL).astype(F32)

    def trunk(x):
        b = x.shape[0]
        x2 = x.reshape(b * SEQ, D_MODEL)
        for l, p in enumerate(layers):
            z2, zt = _in_proj(x2, p['norm_g'], p['w_tok'], p['w_ssm_t'])
            z3 = z2.reshape(b, SEQ, TOK_W)
            y_a = _pool(z3, inv_cnt, p['pool_w'], p['pool_scale'])
            y_b = _dilated(z3, cos_t, sin_t, mask_t)
            y_ct = _ssm_core(zt, *p['ssm'])
            y_d = _neighbourhood(z3, p['na_bias'])
            flat = lambda a: a.reshape(b * SEQ, GROUP_W)
            x2 = _out_proj(x2, flat(y_a), flat(y_b), flat(y_d), y_ct, zt, p['ssm_d'], p['glu_wt'],
                           p['glu_b'], p['w_abd'], p['w_c'], fg, final=(l == DEPTH - 1))
        return x2.reshape(b, SEQ, D_MODEL)

    return (trunk(x_prompt), trunk(x_sample))
```

```python
import functools
import math

import numpy as np
import jax
import jax.numpy as jnp
from jax import lax
from jax.experimental import pallas as pl
from jax.experimental.pallas import tpu as pltpu

D_MODEL = 1024
SEQ = 2048
DEPTH = 4
GROUP_W = 256
HEAD_DIM = 64
POOL_WINDOWS = (2, 4, 8, 16)
POOL_GROUP = 64
DILATED_PATTERNS = ((128, 1), (512, 4), (2048, 16))
SSM_CH = 16
SSM_GROUPS = 16
SSM_STATE = 64
GRID_W = 64
NA_ROWS = 8
NA_COLS = 16
ROPE_THETA = 10000.0
EPS = 1e-6
PROJ_W = 12 * GROUP_W
NEG = -1e30

F32 = jnp.float32
BF16 = jnp.bfloat16
HI = lax.Precision.HIGHEST

ROW_TILE = 512
POOL_PAD = 16
DIL_Q = 128
DIL_R = 64
DIL_REGROUP = 4
STRIP = 32
ATT_UNROLL = 8
LOG2E = math.log2(math.e)
LN2 = math.log(2.0)
SSM_L = 128
SSM_NB_MAX = 16
SSM_PITCH = 24
TOK_W = 10 * GROUP_W
VMEM_LIMIT = 56 * 1024 * 1024


def _sigmoid(x):
    return 1.0 / (1.0 + jnp.exp(-x))


def _silu(x):
    return x * _sigmoid(x)


def _gelu_tanh(x):
    return 0.5 * x * (1.0 + jnp.tanh(math.sqrt(2.0 / math.pi) * (x + 0.044715 * (x * x * x))))


def _toeplitz(k, axis, a, b):
    k = jnp.moveaxis(k, axis, -1)
    w = a + b - 1
    lead = k.shape[:-1]
    kp = jnp.concatenate([k, jnp.zeros(lead + (1,), k.dtype)], axis=-1)
    t = jnp.broadcast_to(kp[..., None, :], lead + (a, w + 1)).reshape(lead + (a * (w + 1),))
    t = t[..., :a * w].reshape(lead + (a, w))[..., a - 1:a - 1 + b]
    return jnp.moveaxis(t, (-2, -1), (axis, axis + 1))


def _params(n_parallel=1, n_arbitrary=0):
    sem = ("parallel",) * n_parallel + ("arbitrary",) * n_arbitrary
    return pltpu.CompilerParams(dimension_semantics=sem, vmem_limit_bytes=VMEM_LIMIT)


def _proj_kernel(*refs, mix_out, proj_in, final):
    refs = list(refs)
    x_ref = refs.pop(0)
    if mix_out:
        (ya_ref, yb_ref, yd_ref, yct_ref, ztp_ref, d_ref, gwt_ref, gb_ref,
         wabd_ref, wc_ref, fg_ref) = refs[:11]
        del refs[:11]
    if proj_in:
        g_ref, w_ref, wt_ref = refs[:3]
        del refs[:3]
    if mix_out:
        xo_ref = refs.pop(0)
    if proj_in:
        z_ref, zt_ref = refs

    x = x_ref[...]
    if mix_out:
        u = ztp_ref[0:GROUP_W, :].astype(F32)
        y = u * d_ref[...] + yct_ref[...]
        g = _gelu_tanh(y)
        lin = jnp.dot(gwt_ref[...], g.astype(BF16), preferred_element_type=F32) + gb_ref[...]
        g = g * _sigmoid(lin)
        y_c = (g * _silu(ztp_ref[GROUP_W:2 * GROUP_W, :].astype(F32))).astype(BF16)
        y_abd = jnp.concatenate([ya_ref[...], yb_ref[...], yd_ref[...]], axis=1)
        acc = jnp.dot(y_abd, wabd_ref[...], preferred_element_type=F32)
        acc = acc + lax.dot_general(y_c, wc_ref[...], (((0,), (0,)), ((), ())),
                                    preferred_element_type=F32)
        x = x + acc
        if final:
            ms = jnp.mean(x * x, axis=-1, keepdims=True)
            x = x * lax.rsqrt(ms + EPS) * fg_ref[...]
        xo_ref[...] = x
    if proj_in:
        ms = jnp.mean(x * x, axis=-1, keepdims=True)
        h = (x * lax.rsqrt(ms + EPS) * g_ref[...]).astype(BF16)
        nch = 512
        for n in range(TOK_W // nch):
            z_ref[:, n * nch:(n + 1) * nch] = jnp.dot(
                h, w_ref[:, n * nch:(n + 1) * nch], preferred_element_type=F32).astype(BF16)
        zt_ref[...] = lax.dot_general(wt_ref[...], h, (((1,), (1,)), ((), ())),
                                      preferred_element_type=F32).astype(BF16)


def _proj(x2, mix=None, nxt=None, final=False):
    rows = x2.shape[0]
    per_seq = SEQ // ROW_TILE
    tile = lambda w: pl.BlockSpec((ROW_TILE, w), lambda i: (i, 0))
    chan = lambda c: pl.BlockSpec((None, c, ROW_TILE), lambda i: (i // per_seq, 0, i % per_seq))
    const = lambda r, c: pl.BlockSpec((r, c), lambda i: (0, 0))
    args, in_specs, out_specs, out_shape = [x2], [tile(D_MODEL)], [], []
    if mix is not None:
        args += list(mix)
        in_specs += [tile(GROUP_W), tile(GROUP_W), tile(GROUP_W), chan(GROUP_W), chan(2 * GROUP_W),
                     const(GROUP_W, 1), const(GROUP_W, GROUP_W), const(GROUP_W, 1),
                     const(3 * GROUP_W, D_MODEL), const(GROUP_W, D_MODEL), const(1, D_MODEL)]
        out_specs.append(tile(D_MODEL))
        out_shape.append(jax.ShapeDtypeStruct((rows, D_MODEL), F32))
    if nxt is not None:
        args += list(nxt)
        in_specs += [const(1, D_MODEL), const(D_MODEL, TOK_W), const(2 * GROUP_W, D_MODEL)]
        out_specs += [tile(TOK_W), chan(2 * GROUP_W)]
        out_shape += [jax.ShapeDtypeStruct((rows, TOK_W), BF16),
                      jax.ShapeDtypeStruct((rows // SEQ, 2 * GROUP_W, SEQ), BF16)]
    return pl.pallas_call(
        functools.partial(_proj_kernel, mix_out=mix is not None, proj_in=nxt is not None, final=final),
        grid=(rows // ROW_TILE,),
        in_specs=in_specs, out_specs=out_specs, out_shape=out_shape,
        compiler_params=_params(),
        name="proj",
    )(*args)


def _pool_kernel(z_ref, inv_ref, w_ref, scale_ref, o_ref, ub, s2b, s4b, s8b):
    s = SEQ
    p = POOL_PAD
    n = s + p
    u = z_ref[:, 0:GROUP_W].astype(F32)
    zero_p = jnp.zeros((p, GROUP_W), F32)
    zero_8 = jnp.zeros((8, GROUP_W), F32)
    ub[0:p, :] = zero_p
    ub[p:p + s, :] = u
    ub[p + s:p + s + p, :] = zero_p
    for buf in (s2b, s4b, s8b):
        buf[0:8, :] = zero_8
        buf[s + 24:s + 32, :] = zero_8
    s2b[8:8 + n, :] = ub[7:7 + n, :] + ub[8:8 + n, :]
    s4b[8:8 + n, :] = s2b[7:7 + n, :] + s2b[9:9 + n, :]
    s8b[8:8 + n, :] = s4b[6:6 + n, :] + s4b[10:10 + n, :]
    s16 = s8b[p - 4:p - 4 + s, :] + s8b[p + 4:p + 4 + s, :]
    lane = lax.broadcasted_iota(jnp.int32, (s, GROUP_W), 1)
    win = jnp.where(lane < 64, s2b[p:p + s, :],
                    jnp.where(lane < 128, s4b[p:p + s, :],
                              jnp.where(lane < 192, s8b[p:p + s, :], s16)))
    diff = (win * inv_ref[...] - u).astype(BF16)
    y = jnp.dot(diff, w_ref[...], preferred_element_type=F32) * scale_ref[...]
    gate = z_ref[:, GROUP_W:2 * GROUP_W].astype(F32)
    o_ref[...] = (y * _silu(gate)).astype(BF16)


def _pool_inv_count():
    t = np.arange(SEQ)
    cols = []
    for w in POOL_WINDOWS:
        lo = np.clip(t - w // 2, 0, SEQ)
        hi = np.clip(t + w // 2, 0, SEQ)
        cols.append(np.repeat((1.0 / (hi - lo))[:, None], POOL_GROUP, axis=1))
    return jnp.asarray(np.concatenate(cols, axis=1), F32)


def _pool(z3, inv_cnt, w_bd, scale):
    b = z3.shape[0]
    return pl.pallas_call(
        _pool_kernel,
        grid=(b,),
        in_specs=[
            pl.BlockSpec((None, SEQ, 2 * GROUP_W), lambda i: (i, 0, 0)),
            pl.BlockSpec((SEQ, GROUP_W), lambda i: (0, 0)),
            pl.BlockSpec((GROUP_W, GROUP_W), lambda i: (0, 0)),
            pl.BlockSpec((1, GROUP_W), lambda i: (0, 0)),
        ],
        out_specs=pl.BlockSpec((None, SEQ, GROUP_W), lambda i: (i, 0, 0)),
        out_shape=jax.ShapeDtypeStruct((b, SEQ, GROUP_W), BF16),
        scratch_shapes=[pltpu.VMEM((SEQ + 2 * POOL_PAD, GROUP_W), F32) for _ in range(4)],
        compiler_params=_params(),
        name="pool_mixer",
    )(z3, inv_cnt, w_bd, scale)


def _rope_tables():
    inv = ROPE_THETA ** (-np.arange(0, HEAD_DIM, 2, dtype=np.float64) / HEAD_DIM)
    ang = np.arange(SEQ, dtype=np.float64)[:, None] * inv[None, :]
    cos = np.concatenate([np.cos(ang), np.cos(ang)], axis=1)
    sin = np.concatenate([-np.sin(ang), np.sin(ang)], axis=1)
    return (jnp.asarray(np.tile(cos, (1, 2)), F32), jnp.asarray(np.tile(sin, (1, 2)), F32))


def _stack_heads(q):
    lane = lax.broadcasted_iota(jnp.int32, q.shape, 1)
    zero = jnp.zeros_like(q)
    return jnp.concatenate([jnp.where(lane < HEAD_DIM, q, zero),
                            jnp.where(lane < HEAD_DIM, zero, q)], axis=0)


def _unstack_heads(o, n):
    lane = lax.broadcasted_iota(jnp.int32, (n, 2 * HEAD_DIM), 1)
    return jnp.where(lane < HEAD_DIM, o[0:n], o[n:2 * n])


def _dilated_mask_table():
    q = np.arange(DIL_Q)[None, :, None]
    k = np.arange(2 * DIL_Q)[None, None, :]
    shift = (np.arange(3) * DIL_R)[:, None, None]
    return jnp.asarray(np.where(np.abs(k - q - shift) <= DIL_R, 0.0, NEG), F32)


def _dilated_kernel(qk_ref, vg_ref, cos_ref, sin_ref, mask_ref, o_ref, qf, kf, vf, q4, k4, v4,
                    acc_o, acc_l, s_scr, p_scr, m_scr):
    s = SEQ
    hd2 = 2 * HEAD_DIM
    lane2 = lax.broadcasted_iota(jnp.int32, (s, 2 * HEAD_DIM), 1)
    first_half = (lane2 % HEAD_DIM) < (HEAD_DIM // 2)
    cos = cos_ref[...]
    sin = sin_ref[...]
    for pair in range(2):
        lo = pair * 2 * HEAD_DIM
        for src, dst, scale in ((0, qf, HEAD_DIM ** -0.5 * LOG2E), (GROUP_W, kf, 1.0)):
            x = qk_ref[:, src + lo:src + lo + 2 * HEAD_DIM].astype(F32)
            swapped = jnp.where(first_half,
                                pltpu.roll(x, 2 * HEAD_DIM - HEAD_DIM // 2, 1),
                                pltpu.roll(x, HEAD_DIM // 2, 1))
            dst[pair] = (x * cos + swapped * sin) * scale
        vf[pair] = vg_ref[:, lo:lo + 2 * HEAD_DIM].astype(F32)

    quarter = s // DIL_REGROUP
    for pair in range(2):
        for nat, grp in ((qf, q4), (kf, k4), (vf, v4)):
            for rho in range(DIL_REGROUP):
                grp[pair, rho * quarter:(rho + 1) * quarter, :] = nat[
                    pair, pl.ds(rho, quarter, stride=DIL_REGROUP), :]

    for pi, (window, d) in enumerate(DILATED_PATTERNS):
        n_sub = s // d
        kw = min(2 * DIL_Q, n_sub)
        nb = n_sub // DIL_Q
        q_src, k_src, v_src = (qf, kf, vf) if d == 1 else (q4, k4, v4)

        def rows(rho, start, size, d=d):
            if d == 1:
                return pl.ds(pl.multiple_of(start, DIL_R), size)
            dd = d // DIL_REGROUP
            base = (rho % DIL_REGROUP) * quarter + rho // DIL_REGROUP + dd * start
            if dd == 1:
                return pl.ds(pl.multiple_of(base, DIL_R), size)
            return pl.ds(base, size, stride=dd)

        def block(trip, carry, d=d, n_sub=n_sub, kw=kw, nb=nb, pi=pi, rows=rows,
                  q_src=q_src, k_src=k_src, v_src=v_src):
            for sub in range(ATT_UNROLL):
                it = trip * ATT_UNROLL + sub
                rho = it // nb
                q0 = (it % nb) * DIL_Q
                ks = jnp.clip(q0 - DIL_R, 0, n_sub - kw)
                case = (q0 - ks) // DIL_R
                q_rows = rows(rho, q0, DIL_Q)
                k_rows = rows(rho, ks, kw)
                for pair in range(2):
                    slot = 2 * sub + pair
                    q = _stack_heads(q_src[pair, q_rows, :].astype(BF16))
                    k = k_src[pair, k_rows, :].astype(BF16)
                    v = _with_ones(v_src[pair, k_rows, :].astype(BF16))
                    s_scr[slot, :, 0:kw] = lax.dot_general(q, k, (((1,), (1,)), ((), ())),
                                                           preferred_element_type=F32)
                    _softmax_strips(
                        s_scr.at[slot], p_scr.at[slot],
                        lambda r0, r1, kw=kw, case=case: mask_ref[
                            case, r0 % DIL_Q:r0 % DIL_Q + (r1 - r0), 0:kw],
                        2 * DIL_Q, kw, m_scr.at[slot])
                    oa = jnp.dot(p_scr[slot, :, 0:kw], v, preferred_element_type=F32)
                    den = oa[:, hd2:2 * hd2]
                    o = oa[:, 0:hd2] / den
                    lse = (m_scr[slot] + jnp.log2(den)) * LN2
                    acc_o[pi, pair, q_rows, :] = _unstack_heads(o, DIL_Q)
                    acc_l[pi, pair, q_rows, :] = _unstack_heads(lse, DIL_Q)
            return carry

        lax.fori_loop(0, d * nb // ATT_UNROLL, block, 0)

    ch = 256
    for rho in range(DIL_REGROUP):
        for c in range(quarter // ch):
            g_rows = slice(rho * quarter + c * ch, rho * quarter + (c + 1) * ch)
            n_rows = pl.ds(rho + DIL_REGROUP * c * ch, ch, stride=DIL_REGROUP)
            for pair in range(2):
                l0, l1, l2 = acc_l[0, pair, n_rows, :], acc_l[1, pair, g_rows, :], acc_l[2, pair, g_rows, :]
                top = jnp.maximum(jnp.maximum(l0, l1), l2)
                w0, w1, w2 = jnp.exp(l0 - top), jnp.exp(l1 - top), jnp.exp(l2 - top)
                acc_o[0, pair, n_rows, :] = (
                    (w0 * acc_o[0, pair, n_rows, :] + w1 * acc_o[1, pair, g_rows, :]
                     + w2 * acc_o[2, pair, g_rows, :]) / (w0 + w1 + w2))
    for c in range(s // ch):
        r = slice(c * ch, (c + 1) * ch)
        for pair in range(2):
            lo = pair * 2 * HEAD_DIM
            gate = vg_ref[r, GROUP_W + lo:GROUP_W + lo + 2 * HEAD_DIM].astype(F32)
            o_ref[r, lo:lo + 2 * HEAD_DIM] = (acc_o[0, pair, r, :] * _silu(gate)).astype(BF16)


def _dilated(z3, cos_t, sin_t, mask_t):
    b = z3.shape[0]
    return pl.pallas_call(
        _dilated_kernel,
        grid=(b,),
        in_specs=[
            pl.BlockSpec((None, SEQ, 2 * GROUP_W), lambda i: (i, 0, 1)),
            pl.BlockSpec((None, SEQ, 2 * GROUP_W), lambda i: (i, 0, 2)),
            pl.BlockSpec((SEQ, 2 * HEAD_DIM), lambda i: (0, 0)),
            pl.BlockSpec((SEQ, 2 * HEAD_DIM), lambda i: (0, 0)),
            pl.BlockSpec(mask_t.shape, lambda i: (0, 0, 0)),
        ],
        out_specs=pl.BlockSpec((None, SEQ, GROUP_W), lambda i: (i, 0, 0)),
        out_shape=jax.ShapeDtypeStruct((b, SEQ, GROUP_W), BF16),
        scratch_shapes=[pltpu.VMEM((2, SEQ, 2 * HEAD_DIM), F32) for _ in range(6)]
        + [pltpu.VMEM((len(DILATED_PATTERNS), 2, SEQ, 2 * HEAD_DIM), F32) for _ in range(2)]
        + [pltpu.VMEM((2 * ATT_UNROLL, 2 * DIL_Q, 2 * DIL_Q), F32),
           pltpu.VMEM((2 * ATT_UNROLL, 2 * DIL_Q, 2 * DIL_Q), BF16),
           pltpu.VMEM((2 * ATT_UNROLL, 2 * DIL_Q, 2 * HEAD_DIM), F32)],
        compiler_params=_params(),
        name="dilated_attention",
    )(z3, z3, cos_t, sin_t, mask_t)


def _na_bias_table(rpb):
    c = np.arange(GRID_W)
    cs = np.clip(c - NA_COLS // 2, 0, GRID_W - NA_COLS)
    col_ok = (c[None, :] >= cs[:, None]) & (c[None, :] < cs[:, None] + NA_COLS)
    pad = GRID_W - NA_COLS
    tab = jnp.pad(rpb.astype(F32), ((0, 0), (0, 0), (pad, pad)))
    tab = _toeplitz(tab, 1, NA_ROWS, NA_ROWS)
    tab = _toeplitz(tab, 3, GRID_W, GRID_W)
    tab = jnp.where(col_ok[None, None, None, :, :], tab * LOG2E, NEG)
    h = rpb.shape[0]
    tab = tab.reshape(h // 2, 2, NA_ROWS, NA_ROWS, GRID_W, GRID_W)
    return tab.transpose(0, 2, 1, 4, 3, 5).reshape(h // 2, NA_ROWS, 2 * GRID_W, NA_ROWS * GRID_W)


def _softmax_strips(s_scr, p_scr, bias, n_rows, n_keys, m_scr=None):
    for r0 in range(0, n_rows, STRIP):
        t = s_scr[r0:r0 + STRIP, 0:n_keys] + bias(r0, r0 + STRIP)
        m = jnp.max(t, axis=1, keepdims=True)
        p_scr[r0:r0 + STRIP, 0:n_keys] = jnp.exp2(t - m).astype(BF16)
        if m_scr is not None:
            m_scr[r0:r0 + STRIP, :] = jnp.broadcast_to(m, (STRIP, 2 * HEAD_DIM))


def _with_ones(v):
    return jnp.concatenate([v, jnp.ones_like(v)], axis=1)


def _na_kernel(qk_ref, vg_ref, bias_ref, o_ref, s_scr, p_scr):
    n_rows = SEQ // GRID_W
    nk = NA_ROWS * GRID_W
    hd2 = 2 * HEAD_DIM

    def rows_trip(trip, carry):
        for sub in range(ATT_UNROLL):
            r = trip * ATT_UNROLL + sub
            rs = jnp.clip(r - NA_ROWS // 2, 0, n_rows - NA_ROWS)
            off = r - rs
            q_rows = pl.ds(pl.multiple_of(r * GRID_W, GRID_W), GRID_W)
            k_rows = pl.ds(pl.multiple_of(rs * GRID_W, GRID_W), nk)
            for pair in range(2):
                slot = 2 * sub + pair
                lo = pair * hd2
                q = qk_ref[q_rows, lo:lo + hd2].astype(F32) * (HEAD_DIM ** -0.5 * LOG2E)
                q = _stack_heads(q.astype(BF16))
                k = qk_ref[k_rows, GROUP_W + lo:GROUP_W + lo + hd2]
                v = _with_ones(vg_ref[k_rows, lo:lo + hd2])
                s_scr[slot] = lax.dot_general(q, k, (((1,), (1,)), ((), ())), preferred_element_type=F32)
                _softmax_strips(s_scr.at[slot], p_scr.at[slot],
                                lambda r0, r1, pair=pair, off=off: bias_ref[pair, off, r0:r1, :],
                                2 * GRID_W, nk)
                oa = jnp.dot(p_scr[slot], v, preferred_element_type=F32)
                o = oa[:, 0:hd2] / oa[:, hd2:2 * hd2]
                gate = vg_ref[q_rows, GROUP_W + lo:GROUP_W + lo + hd2].astype(F32)
                o_ref[q_rows, lo:lo + hd2] = (_unstack_heads(o, GRID_W) * _silu(gate)).astype(BF16)
        return carry

    lax.fori_loop(0, n_rows // ATT_UNROLL, rows_trip, 0)


def _neighbourhood(z3, bias_tab):
    b = z3.shape[0]
    return pl.pallas_call(
        _na_kernel,
        grid=(b,),
        in_specs=[
            pl.BlockSpec((None, SEQ, 2 * GROUP_W), lambda i: (i, 0, 3)),
            pl.BlockSpec((None, SEQ, 2 * GROUP_W), lambda i: (i, 0, 4)),
            pl.BlockSpec(bias_tab.shape, lambda i: (0, 0, 0, 0)),
        ],
        out_specs=pl.BlockSpec((None, SEQ, GROUP_W), lambda i: (i, 0, 0)),
        out_shape=jax.ShapeDtypeStruct((b, SEQ, GROUP_W), BF16),
        scratch_shapes=[pltpu.VMEM((2 * ATT_UNROLL, 2 * GRID_W, NA_ROWS * GRID_W), F32),
                        pltpu.VMEM((2 * ATT_UNROLL, 2 * GRID_W, NA_ROWS * GRID_W), BF16)],
        compiler_params=_params(),
        name="neighbourhood_attention",
    )(z3, z3, bias_tab)


def _ssm_weights(a_re, a_im, log_dt, b_re, b_im, c_re, c_im):
    L, G, P, C = SSM_L, SSM_GROUPS, SSM_STATE, SSM_CH
    a_re, a_im, log_dt = a_re.astype(F32), a_im.astype(F32), log_dt.astype(F32)
    dt = jnp.exp(log_dt)[..., None]
    ks = jnp.arange(L + 1, dtype=F32)
    mag = jnp.exp((a_re * dt)[..., None] * ks)
    ang = (a_im * dt)[..., None] * ks
    pr, pim = mag * jnp.cos(ang), mag * jnp.sin(ang)
    abr, abi = pr[..., 1], pim[..., 1]
    den = a_re * a_re + a_im * a_im
    gr = ((abr - 1.0) * a_re + abi * a_im) / den
    gi = (abi * a_re - (abr - 1.0) * a_im) / den
    br, bi = b_re.astype(F32), b_im.astype(F32)
    bbr = gr[..., None] * br - gi[..., None] * bi
    bbi = gr[..., None] * bi + gi[..., None] * br
    er = pr[..., None] * bbr[..., None, :] - pim[..., None] * bbi[..., None, :]
    ei = pr[..., None] * bbi[..., None, :] + pim[..., None] * bbr[..., None, :]
    cr, ci = c_re.astype(F32), c_im.astype(F32)
    kern = (jnp.einsum('dgcp,dgpkx->dgxck', cr, er, precision=HI)
            - jnp.einsum('dgcp,dgpkx->dgxck', ci, ei, precision=HI))
    kf, kb = kern[0], kern[1]
    k_lag = jnp.concatenate([kb[..., 1:L][..., ::-1], (kf[..., 0] + kb[..., 0])[..., None],
                             kf[..., 1:L], jnp.zeros(kf.shape[:-1] + (1,), F32)], axis=-1)
    k_lag = k_lag.reshape(G, C * C, 2 * L)

    def state_in(d, idx):
        a_r = pr[d][:, :, idx].transpose(0, 2, 1)[:, None]
        a_i = pim[d][:, :, idx].transpose(0, 2, 1)[:, None]
        b_r = bbr[d].transpose(0, 2, 1)[:, :, None]
        b_i = bbi[d].transpose(0, 2, 1)[:, :, None]
        return a_r * b_r - a_i * b_i, a_r * b_i + a_i * b_r

    sf_r, sf_i = state_in(0, slice(L - 1, None, -1))
    sb_r, sb_i = state_in(1, slice(0, L))
    w_s = jnp.concatenate([sf_r, sb_r, sf_i, sb_i], axis=-1).reshape(G, C * L, 4 * P)

    def state_out(d, idx):
        c_r = cr[d].transpose(0, 2, 1)[..., None]
        c_i = ci[d].transpose(0, 2, 1)[..., None]
        a_r = pr[d][:, :, idx][:, :, None, :]
        a_i = pim[d][:, :, idx][:, :, None, :]
        return c_r * a_r - c_i * a_i, c_r * a_i + c_i * a_r

    ff_r, ff_i = state_out(0, slice(1, L + 1))
    fb_r, fb_i = state_out(1, slice(L, 0, -1))
    w_c = jnp.concatenate([ff_r, fb_r, -ff_i, -fb_i], axis=1).reshape(G, 4 * P, C * L)
    a_l = jnp.concatenate([pr[0, ..., L], pr[1, ..., L], pim[0, ..., L], pim[1, ..., L]], axis=-1)
    a_l = jnp.broadcast_to(a_l[:, None, :], (G, SSM_NB_MAX, 4 * P))
    return _ssm_toeplitz(k_lag), w_s.astype(BF16), w_c.astype(BF16), a_l


def _ssm_toeplitz_kernel(k_ref, t_ref):
    L, C = SSM_L, SSM_CH

    def build(cp, carry):
        for c in range(C):
            lag = jnp.broadcast_to(k_ref[pl.ds(cp * C + c, 1), :], (L, 2 * L))
            tile = pltpu.roll(lag, L + 1, 1, stride=1, stride_axis=0)[:, 0:L]
            t_ref[pl.ds(pl.multiple_of(cp * L, L), L), c * L:(c + 1) * L] = tile.astype(BF16)
        return carry

    lax.fori_loop(0, C, build, 0)


def _ssm_toeplitz(k_lag):
    G, C, L = SSM_GROUPS, SSM_CH, SSM_L
    return pl.pallas_call(
        _ssm_toeplitz_kernel,
        grid=(G,),
        in_specs=[pl.BlockSpec((None, C * C, 2 * L), lambda g: (g, 0, 0))],
        out_specs=pl.BlockSpec((None, C * L, C * L), lambda g: (g, 0, 0)),
        out_shape=jax.ShapeDtypeStruct((G, C * L, C * L), BF16),
        compiler_params=_params(),
        name="ssm_toeplitz",
    )(k_lag)


def _ssm_kernel(u_ref, t_ref, ws_ref, wc_ref, al_ref, y_ref, x_scr, y_scr,
                s_re, s_im, ha_re, ha_im, hb_re, hb_im, *, nb):
    L, C, P = SSM_L, SSM_CH, SSM_STATE
    nc = SEQ // L
    m = nb * nc
    half = 2 * P

    for b in range(nb):
        ub = u_ref[b].astype(F32)
        for j in range(nc):
            r0 = (b * nc + j) * SSM_PITCH
            x_scr[r0:r0 + C, :] = ub[:, j * L:(j + 1) * L]
    lhs = jnp.concatenate([x_scr[pl.ds(c, m, stride=SSM_PITCH), :].astype(BF16) for c in range(C)],
                          axis=1)

    s = jnp.dot(lhs, ws_ref[...], preferred_element_type=F32)
    s_re[...] = s[:, 0:half]
    s_im[...] = s[:, half:2 * half]
    a_r = al_ref[0:nb, 0:half]
    a_i = al_ref[0:nb, half:2 * half]
    fwd = lax.broadcasted_iota(jnp.int32, (nb, half), 1) < P
    h_r = jnp.zeros((nb, half), F32)
    h_i = jnp.zeros((nb, half), F32)
    for k in range(nc):
        rf = pl.ds(k, nb, stride=nc)
        rb = pl.ds(nc - 1 - k, nb, stride=nc)
        ha_re[rf, :] = h_r
        ha_im[rf, :] = h_i
        hb_re[rb, :] = h_r
        hb_im[rb, :] = h_i
        s_r = jnp.where(fwd, s_re[rf, :], s_re[rb, :])
        s_i = jnp.where(fwd, s_im[rf, :], s_im[rb, :])
        h_r, h_i = a_r * h_r - a_i * h_i + s_r, a_r * h_i + a_i * h_r + s_i
    fwd_m = lax.broadcasted_iota(jnp.int32, (m, half), 1) < P
    h_all = jnp.concatenate([jnp.where(fwd_m, ha_re[...], hb_re[...]),
                             jnp.where(fwd_m, ha_im[...], hb_im[...])], axis=1).astype(BF16)
    y = jnp.dot(lhs, t_ref[...], preferred_element_type=F32)
    y = y + jnp.dot(h_all, wc_ref[...], preferred_element_type=F32)
    for c in range(C):
        y_scr[pl.ds(c, m, stride=SSM_PITCH), :] = y[:, c * L:(c + 1) * L]
    for b in range(nb):
        y_ref[b] = jnp.concatenate(
            [y_scr[(b * nc + j) * SSM_PITCH:(b * nc + j) * SSM_PITCH + C, :] for j in range(nc)], axis=1)


def _ssm_core(zt, w_t, w_s, w_c, a_l):
    b = zt.shape[0]
    L, G, C, P = SSM_L, SSM_GROUPS, SSM_CH, SSM_STATE
    nb = min(b, SSM_NB_MAX)
    m = nb * (SEQ // L)
    return pl.pallas_call(
        functools.partial(_ssm_kernel, nb=nb),
        grid=(G, b // nb),
        in_specs=[
            pl.BlockSpec((nb, C, SEQ), lambda g, i: (i, g, 0)),
            pl.BlockSpec((None, C * L, C * L), lambda g, i: (g, 0, 0)),
            pl.BlockSpec((None, C * L, 4 * P), lambda g, i: (g, 0, 0)),
            pl.BlockSpec((None, 4 * P, C * L), lambda g, i: (g, 0, 0)),
            pl.BlockSpec((None, SSM_NB_MAX, 4 * P), lambda g, i: (g, 0, 0)),
        ],
        out_specs=pl.BlockSpec((nb, C, SEQ), lambda g, i: (i, g, 0)),
        out_shape=jax.ShapeDtypeStruct((b, G * C, SEQ), F32),
        scratch_shapes=[pltpu.VMEM((m * SSM_PITCH, L), F32), pltpu.VMEM((m * SSM_PITCH, L), F32)]
        + [pltpu.VMEM((m, 2 * P), F32) for _ in range(6)],
        compiler_params=_params(2),
        name="ssm_core",
    )(zt, w_t, w_s, w_c, a_l)


def _block_diag(w):
    n, k = w.shape[0], w.shape[1]
    out = jnp.zeros((n * k, n * k), w.dtype)
    for i in range(n):
        out = lax.dynamic_update_slice(out, w[i], (i * k, i * k))
    return out


def kernel(x_prompt, x_sample, norm_g, w_in, w_out, pool_w, pool_scale, ssm_a_re, ssm_a_im,
           ssm_log_dt, ssm_b_re, ssm_b_im, ssm_c_re, ssm_c_im, ssm_d, glu_w, glu_b, na_rpb, final_g):
    inv_cnt = _pool_inv_count()
    cos_t, sin_t = _rope_tables()
    mask_t = _dilated_mask_table()
    layers = []
    for l in range(DEPTH):
        w_l, wo_l = w_in[l], w_out[l]
        layers.append(dict(
            norm_g=norm_g[l].reshape(1, D_MODEL).astype(F32),
            w_tok=jnp.concatenate([w_l[:, :6 * GROUP_W], w_l[:, 8 * GROUP_W:]], axis=1).astype(BF16),
            w_ssm_t=w_l[:, 6 * GROUP_W:8 * GROUP_W].T.astype(BF16),
            w_abd=jnp.concatenate([wo_l[:2 * GROUP_W], wo_l[3 * GROUP_W:]], axis=0).astype(BF16),
            w_c=wo_l[2 * GROUP_W:3 * GROUP_W].astype(BF16),
            pool_w=_block_diag(pool_w[l]).astype(BF16),
            pool_scale=pool_scale[l].reshape(1, GROUP_W).astype(F32),
            ssm=_ssm_weights(ssm_a_re[l], ssm_a_im[l], ssm_log_dt[l], ssm_b_re[l], ssm_b_im[l],
                             ssm_c_re[l], ssm_c_im[l]),
            ssm_d=ssm_d[l].reshape(GROUP_W, 1).astype(F32),
            glu_wt=glu_w[l].T.astype(BF16),
            glu_b=glu_b[l].reshape(GROUP_W, 1).astype(F32),
            na_bias=_na_bias_table(na_rpb[l]),
        ))
    fg = final_g.reshape(1, D_MODEL).astype(F32)

    def trunk(x):
        b = x.shape[0]
        x2 = x.reshape(b * SEQ, D_MODEL)
        nxt = lambda p: (p['norm_g'], p['w_tok'], p['w_ssm_t'])
        z2, zt = _proj(x2, nxt=nxt(layers[0]))
        for l, p in enumerate(layers):
            z3 = z2.reshape(b, SEQ, TOK_W)
            y_a = _pool(z3, inv_cnt, p['pool_w'], p['pool_scale'])
            y_b = _dilated(z3, cos_t, sin_t, mask_t)
            y_ct = _ssm_core(zt, *p['ssm'])
            y_d = _neighbourhood(z3, p['na_bias'])
            flat = lambda a: a.reshape(b * SEQ, GROUP_W)
            mix = (flat(y_a), flat(y_b), flat(y_d), y_ct, zt, p['ssm_d'], p['glu_wt'], p['glu_b'],
                   p['w_abd'], p['w_c'], fg)
            if l + 1 < DEPTH:
                x2, z2, zt = _proj(x2, mix=mix, nxt=nxt(layers[l + 1]))
            else:
                (x2,) = _proj(x2, mix=mix, final=True)
        return x2.reshape(b, SEQ, D_MODEL)

    return (trunk(x_prompt), trunk(x_sample))
```

```python
import functools
import math

import numpy as np
import jax
import jax.numpy as jnp
from jax import lax
from jax.experimental import pallas as pl
from jax.experimental.pallas import tpu as pltpu

D_MODEL = 1024
SEQ = 2048
DEPTH = 4
GROUP_W = 256
HEAD_DIM = 64
POOL_WINDOWS = (2, 4, 8, 16)
POOL_GROUP = 64
DILATED_PATTERNS = ((128, 1), (512, 4), (2048, 16))
SSM_CH = 16
SSM_GROUPS = 16
SSM_STATE = 64
GRID_W = 64
NA_ROWS = 8
NA_COLS = 16
ROPE_THETA = 10000.0
EPS = 1e-6
PROJ_W = 12 * GROUP_W
NEG = -1e30

F32 = jnp.float32
BF16 = jnp.bfloat16
HI = lax.Precision.HIGHEST

ROW_TILE = 512
POOL_PAD = 16
DIL_Q = 128
DIL_R = 64
DIL_REGROUP = 4
STRIP = 32
ATT_UNROLL = 8
LOG2E = math.log2(math.e)
LN2 = math.log(2.0)
SSM_L = 128
SSM_NB_MAX = 16
SSM_PITCH = 24
TOK_W = 10 * GROUP_W
VMEM_LIMIT = 56 * 1024 * 1024


def _sigmoid(x):
    return 1.0 / (1.0 + jnp.exp(-x))


def _silu(x):
    return x * _sigmoid(x)


def _gelu_tanh(x):
    return 0.5 * x * (1.0 + jnp.tanh(math.sqrt(2.0 / math.pi) * (x + 0.044715 * (x * x * x))))


def _toeplitz(k, axis, a, b):
    k = jnp.moveaxis(k, axis, -1)
    w = a + b - 1
    lead = k.shape[:-1]
    kp = jnp.concatenate([k, jnp.zeros(lead + (1,), k.dtype)], axis=-1)
    t = jnp.broadcast_to(kp[..., None, :], lead + (a, w + 1)).reshape(lead + (a * (w + 1),))
    t = t[..., :a * w].reshape(lead + (a, w))[..., a - 1:a - 1 + b]
    return jnp.moveaxis(t, (-2, -1), (axis, axis + 1))


def _params(n_parallel=1, n_arbitrary=0):
    sem = ("parallel",) * n_parallel + ("arbitrary",) * n_arbitrary
    return pltpu.CompilerParams(dimension_semantics=sem, vmem_limit_bytes=VMEM_LIMIT)


def _proj_kernel(*refs, mix_out, proj_in, final):
    refs = list(refs)
    x_ref = refs.pop(0)
    if mix_out:
        (ya_ref, yb_ref, yd_ref, yct_ref, ztp_ref, d_ref, gwt_ref, gb_ref,
         wabd_ref, wc_ref, fg_ref) = refs[:11]
        del refs[:11]
    if proj_in:
        g_ref, w_ref, wt_ref = refs[:3]
        del refs[:3]
    if mix_out:
        xo_ref = refs.pop(0)
    if proj_in:
        z_ref, zt_ref = refs

    x = x_ref[...]
    if mix_out:
        u = ztp_ref[0:GROUP_W, :].astype(F32)
        y = u * d_ref[...] + yct_ref[...]
        g = _gelu_tanh(y)
        lin = jnp.dot(gwt_ref[...], g.astype(BF16), preferred_element_type=F32) + gb_ref[...]
        g = g * _sigmoid(lin)
        y_c = (g * _silu(ztp_ref[GROUP_W:2 * GROUP_W, :].astype(F32))).astype(BF16)
        y_abd = jnp.concatenate([ya_ref[...], yb_ref[...], yd_ref[...]], axis=1)
        acc = jnp.dot(y_abd, wabd_ref[...], preferred_element_type=F32)
        acc = acc + lax.dot_general(y_c, wc_ref[...], (((0,), (0,)), ((), ())),
                                    preferred_element_type=F32)
        x = x + acc
        if final:
            ms = jnp.mean(x * x, axis=-1, keepdims=True)
            x = x * lax.rsqrt(ms + EPS) * fg_ref[...]
        xo_ref[...] = x
    if proj_in:
        ms = jnp.mean(x * x, axis=-1, keepdims=True)
        h = (x * lax.rsqrt(ms + EPS) * g_ref[...]).astype(BF16)
        nch = 512
        for n in range(TOK_W // nch):
            z_ref[:, n * nch:(n + 1) * nch] = jnp.dot(
                h, w_ref[:, n * nch:(n + 1) * nch], preferred_element_type=F32).astype(BF16)
        zt_ref[...] = lax.dot_general(wt_ref[...], h, (((1,), (1,)), ((), ())),
                                      preferred_element_type=F32).astype(BF16)


def _proj(x2, mix=None, mix_layer=None, nxt=None, nxt_layer=None, final=False):
    rows = x2.shape[0]
    per_seq = SEQ // ROW_TILE
    tile = lambda w: pl.BlockSpec((ROW_TILE, w), lambda i: (i, 0))
    chan = lambda c: pl.BlockSpec((None, c, ROW_TILE), lambda i: (i // per_seq, 0, i % per_seq))
    layer = lambda l, r, c: pl.BlockSpec((None, r, c), lambda i: (l, 0, 0))
    args, in_specs, out_specs, out_shape = [x2], [tile(D_MODEL)], [], []
    if mix is not None:
        args += list(mix)
        ml = mix_layer
        in_specs += [tile(GROUP_W), tile(GROUP_W), tile(GROUP_W), chan(GROUP_W), chan(2 * GROUP_W),
                     layer(ml, GROUP_W, 1), layer(ml, GROUP_W, GROUP_W), layer(ml, GROUP_W, 1),
                     layer(ml, 3 * GROUP_W, D_MODEL), layer(ml, GROUP_W, D_MODEL),
                     pl.BlockSpec((1, D_MODEL), lambda i: (0, 0))]
        out_specs.append(tile(D_MODEL))
        out_shape.append(jax.ShapeDtypeStruct((rows, D_MODEL), F32))
    if nxt is not None:
        args += list(nxt)
        nl = nxt_layer
        in_specs += [layer(nl, 1, D_MODEL), layer(nl, D_MODEL, TOK_W), layer(nl, 2 * GROUP_W, D_MODEL)]
        out_specs += [tile(TOK_W), chan(2 * GROUP_W)]
        out_shape += [jax.ShapeDtypeStruct((rows, TOK_W), BF16),
                      jax.ShapeDtypeStruct((rows // SEQ, 2 * GROUP_W, SEQ), BF16)]
    return pl.pallas_call(
        functools.partial(_proj_kernel, mix_out=mix is not None, proj_in=nxt is not None, final=final),
        grid=(rows // ROW_TILE,),
        in_specs=in_specs, out_specs=out_specs, out_shape=out_shape,
        compiler_params=_params(),
        name="proj",
    )(*args)


def _pool_kernel(z_ref, inv_ref, w_ref, scale_ref, o_ref, ub, s2b, s4b, s8b):
    s = SEQ
    p = POOL_PAD
    n = s + p
    u = z_ref[:, 0:GROUP_W].astype(F32)
    zero_p = jnp.zeros((p, GROUP_W), F32)
    zero_8 = jnp.zeros((8, GROUP_W), F32)
    ub[0:p, :] = zero_p
    ub[p:p + s, :] = u
    ub[p + s:p + s + p, :] = zero_p
    for buf in (s2b, s4b, s8b):
        buf[0:8, :] = zero_8
        buf[s + 24:s + 32, :] = zero_8
    s2b[8:8 + n, :] = ub[7:7 + n, :] + ub[8:8 + n, :]
    s4b[8:8 + n, :] = s2b[7:7 + n, :] + s2b[9:9 + n, :]
    s8b[8:8 + n, :] = s4b[6:6 + n, :] + s4b[10:10 + n, :]
    s16 = s8b[p - 4:p - 4 + s, :] + s8b[p + 4:p + 4 + s, :]
    lane = lax.broadcasted_iota(jnp.int32, (s, GROUP_W), 1)
    win = jnp.where(lane < 64, s2b[p:p + s, :],
                    jnp.where(lane < 128, s4b[p:p + s, :],
                              jnp.where(lane < 192, s8b[p:p + s, :], s16)))
    diff = (win * inv_ref[...] - u).astype(BF16)
    y = jnp.dot(diff, w_ref[...], preferred_element_type=F32) * scale_ref[...]
    gate = z_ref[:, GROUP_W:2 * GROUP_W].astype(F32)
    o_ref[...] = (y * _silu(gate)).astype(BF16)


def _pool_inv_count():
    t = np.arange(SEQ)
    cols = []
    for w in POOL_WINDOWS:
        lo = np.clip(t - w // 2, 0, SEQ)
        hi = np.clip(t + w // 2, 0, SEQ)
        cols.append(np.repeat((1.0 / (hi - lo))[:, None], POOL_GROUP, axis=1))
    return jnp.asarray(np.concatenate(cols, axis=1), F32)


def _pool(z3, inv_cnt, w_bd, scale, l):
    b = z3.shape[0]
    return pl.pallas_call(
        _pool_kernel,
        grid=(b,),
        in_specs=[
            pl.BlockSpec((None, SEQ, 2 * GROUP_W), lambda i: (i, 0, 0)),
            pl.BlockSpec((SEQ, GROUP_W), lambda i: (0, 0)),
            pl.BlockSpec((None, GROUP_W, GROUP_W), lambda i: (l, 0, 0)),
            pl.BlockSpec((None, 1, GROUP_W), lambda i: (l, 0, 0)),
        ],
        out_specs=pl.BlockSpec((None, SEQ, GROUP_W), lambda i: (i, 0, 0)),
        out_shape=jax.ShapeDtypeStruct((b, SEQ, GROUP_W), BF16),
        scratch_shapes=[pltpu.VMEM((SEQ + 2 * POOL_PAD, GROUP_W), F32) for _ in range(4)],
        compiler_params=_params(),
        name="pool_mixer",
    )(z3, inv_cnt, w_bd, scale)


def _rope_tables():
    inv = ROPE_THETA ** (-np.arange(0, HEAD_DIM, 2, dtype=np.float64) / HEAD_DIM)
    ang = np.arange(SEQ, dtype=np.float64)[:, None] * inv[None, :]
    cos = np.concatenate([np.cos(ang), np.cos(ang)], axis=1)
    sin = np.concatenate([-np.sin(ang), np.sin(ang)], axis=1)
    return (jnp.asarray(np.tile(cos, (1, 2)), F32), jnp.asarray(np.tile(sin, (1, 2)), F32))


def _stack_heads(q):
    lane = lax.broadcasted_iota(jnp.int32, q.shape, 1)
    zero = jnp.zeros_like(q)
    return jnp.concatenate([jnp.where(lane < HEAD_DIM, q, zero),
                            jnp.where(lane < HEAD_DIM, zero, q)], axis=0)


def _unstack_heads(o, n):
    lane = lax.broadcasted_iota(jnp.int32, (n, 2 * HEAD_DIM), 1)
    return jnp.where(lane < HEAD_DIM, o[0:n], o[n:2 * n])


def _dilated_mask_table():
    q = np.arange(DIL_Q)[None, :, None]
    k = np.arange(2 * DIL_Q)[None, None, :]
    shift = (np.arange(3) * DIL_R)[:, None, None]
    return jnp.asarray(np.where(np.abs(k - q - shift) <= DIL_R, 0.0, NEG), F32)


def _dilated_kernel(qk_ref, vg_ref, cos_ref, sin_ref, mask_ref, o_ref, qf, kf, vf, q4, k4, v4,
                    acc_o, acc_l, s_scr, p_scr, m_scr):
    s = SEQ
    hd2 = 2 * HEAD_DIM
    lane2 = lax.broadcasted_iota(jnp.int32, (s, 2 * HEAD_DIM), 1)
    first_half = (lane2 % HEAD_DIM) < (HEAD_DIM // 2)
    cos = cos_ref[...]
    sin = sin_ref[...]
    for pair in range(2):
        lo = pair * 2 * HEAD_DIM
        for src, dst, scale in ((0, qf, HEAD_DIM ** -0.5 * LOG2E), (GROUP_W, kf, 1.0)):
            x = qk_ref[:, src + lo:src + lo + 2 * HEAD_DIM].astype(F32)
            swapped = jnp.where(first_half,
                                pltpu.roll(x, 2 * HEAD_DIM - HEAD_DIM // 2, 1),
                                pltpu.roll(x, HEAD_DIM // 2, 1))
            dst[pair] = (x * cos + swapped * sin) * scale
        vf[pair] = vg_ref[:, lo:lo + 2 * HEAD_DIM].astype(F32)

    quarter = s // DIL_REGROUP
    for pair in range(2):
        for nat, grp in ((qf, q4), (kf, k4), (vf, v4)):
            for rho in range(DIL_REGROUP):
                grp[pair, rho * quarter:(rho + 1) * quarter, :] = nat[
                    pair, pl.ds(rho, quarter, stride=DIL_REGROUP), :]

    for pi, (window, d) in enumerate(DILATED_PATTERNS):
        n_sub = s // d
        kw = min(2 * DIL_Q, n_sub)
        nb = n_sub // DIL_Q
        q_src, k_src, v_src = (qf, kf, vf) if d == 1 else (q4, k4, v4)

        def rows(rho, start, size, d=d):
            if d == 1:
                return pl.ds(pl.multiple_of(start, DIL_R), size)
            dd = d // DIL_REGROUP
            base = (rho % DIL_REGROUP) * quarter + rho // DIL_REGROUP + dd * start
            if dd == 1:
                return pl.ds(pl.multiple_of(base, DIL_R), size)
            return pl.ds(base, size, stride=dd)

        def block(trip, carry, d=d, n_sub=n_sub, kw=kw, nb=nb, pi=pi, rows=rows,
                  q_src=q_src, k_src=k_src, v_src=v_src):
            for sub in range(ATT_UNROLL):
                it = trip * ATT_UNROLL + sub
                rho = it // nb
                q0 = (it % nb) * DIL_Q
                ks = jnp.clip(q0 - DIL_R, 0, n_sub - kw)
                case = (q0 - ks) // DIL_R
                q_rows = rows(rho, q0, DIL_Q)
                k_rows = rows(rho, ks, kw)
                for pair in range(2):
                    slot = 2 * sub + pair
                    q = _stack_heads(q_src[pair, q_rows, :].astype(BF16))
                    k = k_src[pair, k_rows, :].astype(BF16)
                    v = _with_ones(v_src[pair, k_rows, :].astype(BF16))
                    s_scr[slot, :, 0:kw] = lax.dot_general(q, k, (((1,), (1,)), ((), ())),
                                                           preferred_element_type=F32)
                    _softmax_strips(
                        s_scr.at[slot], p_scr.at[slot],
                        lambda r0, r1, kw=kw, case=case: mask_ref[
                            case, r0 % DIL_Q:r0 % DIL_Q + (r1 - r0), 0:kw],
                        2 * DIL_Q, kw, m_scr.at[slot])
                    oa = jnp.dot(p_scr[slot, :, 0:kw], v, preferred_element_type=F32)
                    den = oa[:, hd2:2 * hd2]
                    o = oa[:, 0:hd2] / den
                    lse = (m_scr[slot] + jnp.log2(den)) * LN2
                    acc_o[pi, pair, q_rows, :] = _unstack_heads(o, DIL_Q)
                    acc_l[pi, pair, q_rows, :] = _unstack_heads(lse, DIL_Q)
            return carry

        lax.fori_loop(0, d * nb // ATT_UNROLL, block, 0)

    ch = 256
    for rho in range(DIL_REGROUP):
        for c in range(quarter // ch):
            g_rows = slice(rho * quarter + c * ch, rho * quarter + (c + 1) * ch)
            n_rows = pl.ds(rho + DIL_REGROUP * c * ch, ch, stride=DIL_REGROUP)
            for pair in range(2):
                l0, l1, l2 = acc_l[0, pair, n_rows, :], acc_l[1, pair, g_rows, :], acc_l[2, pair, g_rows, :]
                top = jnp.maximum(jnp.maximum(l0, l1), l2)
                w0, w1, w2 = jnp.exp(l0 - top), jnp.exp(l1 - top), jnp.exp(l2 - top)
                acc_o[0, pair, n_rows, :] = (
                    (w0 * acc_o[0, pair, n_rows, :] + w1 * acc_o[1, pair, g_rows, :]
                     + w2 * acc_o[2, pair, g_rows, :]) / (w0 + w1 + w2))
    for c in range(s // ch):
        r = slice(c * ch, (c + 1) * ch)
        for pair in range(2):
            lo = pair * 2 * HEAD_DIM
            gate = vg_ref[r, GROUP_W + lo:GROUP_W + lo + 2 * HEAD_DIM].astype(F32)
            o_ref[r, lo:lo + 2 * HEAD_DIM] = (acc_o[0, pair, r, :] * _silu(gate)).astype(BF16)


def _dilated(z3, cos_t, sin_t, mask_t):
    b = z3.shape[0]
    return pl.pallas_call(
        _dilated_kernel,
        grid=(b,),
        in_specs=[
            pl.BlockSpec((None, SEQ, 2 * GROUP_W), lambda i: (i, 0, 1)),
            pl.BlockSpec((None, SEQ, 2 * GROUP_W), lambda i: (i, 0, 2)),
            pl.BlockSpec((SEQ, 2 * HEAD_DIM), lambda i: (0, 0)),
            pl.BlockSpec((SEQ, 2 * HEAD_DIM), lambda i: (0, 0)),
            pl.BlockSpec(mask_t.shape, lambda i: (0, 0, 0)),
        ],
        out_specs=pl.BlockSpec((None, SEQ, GROUP_W), lambda i: (i, 0, 0)),
        out_shape=jax.ShapeDtypeStruct((b, SEQ, GROUP_W), BF16),
        scratch_shapes=[pltpu.VMEM((2, SEQ, 2 * HEAD_DIM), F32) for _ in range(6)]
        + [pltpu.VMEM((len(DILATED_PATTERNS), 2, SEQ, 2 * HEAD_DIM), F32) for _ in range(2)]
        + [pltpu.VMEM((2 * ATT_UNROLL, 2 * DIL_Q, 2 * DIL_Q), F32),
           pltpu.VMEM((2 * ATT_UNROLL, 2 * DIL_Q, 2 * DIL_Q), BF16),
           pltpu.VMEM((2 * ATT_UNROLL, 2 * DIL_Q, 2 * HEAD_DIM), F32)],
        compiler_params=_params(),
        name="dilated_attention",
    )(z3, z3, cos_t, sin_t, mask_t)


def _na_bias_table(rpb):
    c = np.arange(GRID_W)
    cs = np.clip(c - NA_COLS // 2, 0, GRID_W - NA_COLS)
    col_ok = (c[None, :] >= cs[:, None]) & (c[None, :] < cs[:, None] + NA_COLS)
    pad = GRID_W - NA_COLS
    tab = jnp.pad(rpb.astype(F32), ((0, 0), (0, 0), (pad, pad)))
    tab = _toeplitz(tab, 1, NA_ROWS, NA_ROWS)
    tab = _toeplitz(tab, 3, GRID_W, GRID_W)
    tab = jnp.where(col_ok[None, None, None, :, :], tab * LOG2E, NEG)
    h = rpb.shape[0]
    tab = tab.reshape(h // 2, 2, NA_ROWS, NA_ROWS, GRID_W, GRID_W)
    return tab.transpose(0, 2, 1, 4, 3, 5).reshape(h // 2, NA_ROWS, 2 * GRID_W, NA_ROWS * GRID_W)


def _softmax_strips(s_scr, p_scr, bias, n_rows, n_keys, m_scr=None):
    for r0 in range(0, n_rows, STRIP):
        t = s_scr[r0:r0 + STRIP, 0:n_keys] + bias(r0, r0 + STRIP)
        m = jnp.max(t, axis=1, keepdims=True)
        p_scr[r0:r0 + STRIP, 0:n_keys] = jnp.exp2(t - m).astype(BF16)
        if m_scr is not None:
            m_scr[r0:r0 + STRIP, :] = jnp.broadcast_to(m, (STRIP, 2 * HEAD_DIM))


def _with_ones(v):
    return jnp.concatenate([v, jnp.ones_like(v)], axis=1)


def _na_kernel(qk_ref, vg_ref, bias_ref, o_ref, s_scr, p_scr):
    n_rows = SEQ // GRID_W
    nk = NA_ROWS * GRID_W
    hd2 = 2 * HEAD_DIM

    def rows_trip(trip, carry):
        for sub in range(ATT_UNROLL):
            r = trip * ATT_UNROLL + sub
            rs = jnp.clip(r - NA_ROWS // 2, 0, n_rows - NA_ROWS)
            off = r - rs
            q_rows = pl.ds(pl.multiple_of(r * GRID_W, GRID_W), GRID_W)
            k_rows = pl.ds(pl.multiple_of(rs * GRID_W, GRID_W), nk)
            for pair in range(2):
                slot = 2 * sub + pair
                lo = pair * hd2
                q = qk_ref[q_rows, lo:lo + hd2].astype(F32) * (HEAD_DIM ** -0.5 * LOG2E)
                q = _stack_heads(q.astype(BF16))
                k = qk_ref[k_rows, GROUP_W + lo:GROUP_W + lo + hd2]
                v = _with_ones(vg_ref[k_rows, lo:lo + hd2])
                s_scr[slot] = lax.dot_general(q, k, (((1,), (1,)), ((), ())), preferred_element_type=F32)
                _softmax_strips(s_scr.at[slot], p_scr.at[slot],
                                lambda r0, r1, pair=pair, off=off: bias_ref[pair, off, r0:r1, :],
                                2 * GRID_W, nk)
                oa = jnp.dot(p_scr[slot], v, preferred_element_type=F32)
                o = oa[:, 0:hd2] / oa[:, hd2:2 * hd2]
                gate = vg_ref[q_rows, GROUP_W + lo:GROUP_W + lo + hd2].astype(F32)
                o_ref[q_rows, lo:lo + hd2] = (_unstack_heads(o, GRID_W) * _silu(gate)).astype(BF16)
        return carry

    lax.fori_loop(0, n_rows // ATT_UNROLL, rows_trip, 0)


def _neighbourhood(z3, bias_tab, l):
    b = z3.shape[0]
    return pl.pallas_call(
        _na_kernel,
        grid=(b,),
        in_specs=[
            pl.BlockSpec((None, SEQ, 2 * GROUP_W), lambda i: (i, 0, 3)),
            pl.BlockSpec((None, SEQ, 2 * GROUP_W), lambda i: (i, 0, 4)),
            pl.BlockSpec((None,) + bias_tab.shape[1:], lambda i: (l, 0, 0, 0, 0)),
        ],
        out_specs=pl.BlockSpec((None, SEQ, GROUP_W), lambda i: (i, 0, 0)),
        out_shape=jax.ShapeDtypeStruct((b, SEQ, GROUP_W), BF16),
        scratch_shapes=[pltpu.VMEM((2 * ATT_UNROLL, 2 * GRID_W, NA_ROWS * GRID_W), F32),
                        pltpu.VMEM((2 * ATT_UNROLL, 2 * GRID_W, NA_ROWS * GRID_W), BF16)],
        compiler_params=_params(),
        name="neighbourhood_attention",
    )(z3, z3, bias_tab)


def _ssm_weights(a_re, a_im, log_dt, b_re, b_im, c_re, c_im):
    L, G, P, C = SSM_L, SSM_GROUPS, SSM_STATE, SSM_CH
    a_re, a_im, log_dt = a_re.astype(F32), a_im.astype(F32), log_dt.astype(F32)
    dt = jnp.exp(log_dt)[..., None]
    ks = jnp.arange(L + 1, dtype=F32)
    mag = jnp.exp((a_re * dt)[..., None] * ks)
    ang = (a_im * dt)[..., None] * ks
    pr, pim = mag * jnp.cos(ang), mag * jnp.sin(ang)
    abr, abi = pr[..., 1], pim[..., 1]
    den = a_re * a_re + a_im * a_im
    gr = ((abr - 1.0) * a_re + abi * a_im) / den
    gi = (abi * a_re - (abr - 1.0) * a_im) / den
    br, bi = b_re.astype(F32), b_im.astype(F32)
    bbr = gr[..., None] * br - gi[..., None] * bi
    bbi = gr[..., None] * bi + gi[..., None] * br
    er = pr[..., None] * bbr[..., None, :] - pim[..., None] * bbi[..., None, :]
    ei = pr[..., None] * bbi[..., None, :] + pim[..., None] * bbr[..., None, :]
    cr, ci = c_re.astype(F32), c_im.astype(F32)
    kern = (jnp.einsum('dgcp,dgpkx->dgxck', cr, er, precision=HI)
            - jnp.einsum('dgcp,dgpkx->dgxck', ci, ei, precision=HI))
    kf, kb = kern[0], kern[1]
    k_lag = jnp.concatenate([kb[..., 1:L][..., ::-1], (kf[..., 0] + kb[..., 0])[..., None],
                             kf[..., 1:L], jnp.zeros(kf.shape[:-1] + (1,), F32)], axis=-1)
    k_lag = k_lag.reshape(G, C * C, 2 * L)

    def state_in(d, idx):
        a_r = pr[d][:, :, idx].transpose(0, 2, 1)[:, None]
        a_i = pim[d][:, :, idx].transpose(0, 2, 1)[:, None]
        b_r = bbr[d].transpose(0, 2, 1)[:, :, None]
        b_i = bbi[d].transpose(0, 2, 1)[:, :, None]
        return a_r * b_r - a_i * b_i, a_r * b_i + a_i * b_r

    sf_r, sf_i = state_in(0, slice(L - 1, None, -1))
    sb_r, sb_i = state_in(1, slice(0, L))
    w_s = jnp.concatenate([sf_r, sb_r, sf_i, sb_i], axis=-1).reshape(G, C * L, 4 * P)

    def state_out(d, idx):
        c_r = cr[d].transpose(0, 2, 1)[..., None]
        c_i = ci[d].transpose(0, 2, 1)[..., None]
        a_r = pr[d][:, :, idx][:, :, None, :]
        a_i = pim[d][:, :, idx][:, :, None, :]
        return c_r * a_r - c_i * a_i, c_r * a_i + c_i * a_r

    ff_r, ff_i = state_out(0, slice(1, L + 1))
    fb_r, fb_i = state_out(1, slice(L, 0, -1))
    w_c = jnp.concatenate([ff_r, fb_r, -ff_i, -fb_i], axis=1).reshape(G, 4 * P, C * L)
    a_l = jnp.concatenate([pr[0, ..., L], pr[1, ..., L], pim[0, ..., L], pim[1, ..., L]], axis=-1)
    a_l = jnp.broadcast_to(a_l[:, None, :], (G, SSM_NB_MAX, 4 * P))
    return k_lag, w_s.astype(BF16), w_c.astype(BF16), a_l


def _ssm_toeplitz_kernel(k_ref, t_ref):
    L, C = SSM_L, SSM_CH

    def build(cp, carry):
        for c in range(C):
            lag = jnp.broadcast_to(k_ref[pl.ds(cp * C + c, 1), :], (L, 2 * L))
            tile = pltpu.roll(lag, L + 1, 1, stride=1, stride_axis=0)[:, 0:L]
            t_ref[pl.ds(pl.multiple_of(cp * L, L), L), c * L:(c + 1) * L] = tile.astype(BF16)
        return carry

    lax.fori_loop(0, C, build, 0)


def _ssm_toeplitz(k_lag):
    n, C, L = k_lag.shape[0], SSM_CH, SSM_L
    return pl.pallas_call(
        _ssm_toeplitz_kernel,
        grid=(n,),
        in_specs=[pl.BlockSpec((None, C * C, 2 * L), lambda g: (g, 0, 0))],
        out_specs=pl.BlockSpec((None, C * L, C * L), lambda g: (g, 0, 0)),
        out_shape=jax.ShapeDtypeStruct((n, C * L, C * L), BF16),
        compiler_params=_params(),
        name="ssm_toeplitz",
    )(k_lag)


def _ssm_kernel(u_ref, t_ref, ws_ref, wc_ref, al_ref, y_ref, x_scr, y_scr,
                s_re, s_im, ha_re, ha_im, hb_re, hb_im, *, nb):
    L, C, P = SSM_L, SSM_CH, SSM_STATE
    nc = SEQ // L
    m = nb * nc
    half = 2 * P

    for b in range(nb):
        ub = u_ref[b].astype(F32)
        for j in range(nc):
            r0 = (b * nc + j) * SSM_PITCH
            x_scr[r0:r0 + C, :] = ub[:, j * L:(j + 1) * L]
    lhs = jnp.concatenate([x_scr[pl.ds(c, m, stride=SSM_PITCH), :].astype(BF16) for c in range(C)],
                          axis=1)

    s = jnp.dot(lhs, ws_ref[...], preferred_element_type=F32)
    s_re[...] = s[:, 0:half]
    s_im[...] = s[:, half:2 * half]
    a_r = al_ref[0:nb, 0:half]
    a_i = al_ref[0:nb, half:2 * half]
    fwd = lax.broadcasted_iota(jnp.int32, (nb, half), 1) < P
    h_r = jnp.zeros((nb, half), F32)
    h_i = jnp.zeros((nb, half), F32)
    for k in range(nc):
        rf = pl.ds(k, nb, stride=nc)
        rb = pl.ds(nc - 1 - k, nb, stride=nc)
        ha_re[rf, :] = h_r
        ha_im[rf, :] = h_i
        hb_re[rb, :] = h_r
        hb_im[rb, :] = h_i
        s_r = jnp.where(fwd, s_re[rf, :], s_re[rb, :])
        s_i = jnp.where(fwd, s_im[rf, :], s_im[rb, :])
        h_r, h_i = a_r * h_r - a_i * h_i + s_r, a_r * h_i + a_i * h_r + s_i
    fwd_m = lax.broadcasted_iota(jnp.int32, (m, half), 1) < P
    h_all = jnp.concatenate([jnp.where(fwd_m, ha_re[...], hb_re[...]),
                             jnp.where(fwd_m, ha_im[...], hb_im[...])], axis=1).astype(BF16)
    y = jnp.dot(lhs, t_ref[...], preferred_element_type=F32)
    y = y + jnp.dot(h_all, wc_ref[...], preferred_element_type=F32)
    for c in range(C):
        y_scr[pl.ds(c, m, stride=SSM_PITCH), :] = y[:, c * L:(c + 1) * L]
    for b in range(nb):
        y_ref[b] = jnp.concatenate(
            [y_scr[(b * nc + j) * SSM_PITCH:(b * nc + j) * SSM_PITCH + C, :] for j in range(nc)], axis=1)


def _ssm_core(zt, w_t, w_s, w_c, a_l, l):
    b = zt.shape[0]
    L, G, C, P = SSM_L, SSM_GROUPS, SSM_CH, SSM_STATE
    nb = min(b, SSM_NB_MAX)
    m = nb * (SEQ // L)
    return pl.pallas_call(
        functools.partial(_ssm_kernel, nb=nb),
        grid=(G, b // nb),
        in_specs=[
            pl.BlockSpec((nb, C, SEQ), lambda g, i: (i, g, 0)),
            pl.BlockSpec((None, C * L, C * L), lambda g, i: (l * G + g, 0, 0)),
            pl.BlockSpec((None, C * L, 4 * P), lambda g, i: (l * G + g, 0, 0)),
            pl.BlockSpec((None, 4 * P, C * L), lambda g, i: (l * G + g, 0, 0)),
            pl.BlockSpec((None, SSM_NB_MAX, 4 * P), lambda g, i: (l * G + g, 0, 0)),
        ],
        out_specs=pl.BlockSpec((nb, C, SEQ), lambda g, i: (i, g, 0)),
        out_shape=jax.ShapeDtypeStruct((b, G * C, SEQ), F32),
        scratch_shapes=[pltpu.VMEM((m * SSM_PITCH, L), F32), pltpu.VMEM((m * SSM_PITCH, L), F32)]
        + [pltpu.VMEM((m, 2 * P), F32) for _ in range(6)],
        compiler_params=_params(2),
        name="ssm_core",
    )(zt, w_t, w_s, w_c, a_l)


def _block_diag(w):
    n, k = w.shape[0], w.shape[1]
    out = jnp.zeros((n * k, n * k), w.dtype)
    for i in range(n):
        out = lax.dynamic_update_slice(out, w[i], (i * k, i * k))
    return out


def kernel(x_prompt, x_sample, norm_g, w_in, w_out, pool_w, pool_scale, ssm_a_re, ssm_a_im,
           ssm_log_dt, ssm_b_re, ssm_b_im, ssm_c_re, ssm_c_im, ssm_d, glu_w, glu_b, na_rpb, final_g):
    inv_cnt = _pool_inv_count()
    cos_t, sin_t = _rope_tables()
    mask_t = _dilated_mask_table()
    gw = GROUP_W
    norm_all = norm_g.reshape(DEPTH, 1, D_MODEL).astype(F32)
    w_tok = jnp.concatenate([w_in[:, :, :6 * gw], w_in[:, :, 8 * gw:]], axis=2).astype(BF16)
    w_ssm_t = jnp.swapaxes(w_in[:, :, 6 * gw:8 * gw], 1, 2).astype(BF16)
    w_abd = jnp.concatenate([w_out[:, :2 * gw], w_out[:, 3 * gw:]], axis=1).astype(BF16)
    w_c = w_out[:, 2 * gw:3 * gw].astype(BF16)
    pool_bd = jax.vmap(_block_diag)(pool_w).astype(BF16)
    pool_sc = pool_scale.reshape(DEPTH, 1, gw).astype(F32)
    k_lag, ssm_ws, ssm_wc, ssm_al = jax.vmap(_ssm_weights)(
        ssm_a_re, ssm_a_im, ssm_log_dt, ssm_b_re, ssm_b_im, ssm_c_re, ssm_c_im)
    merge = lambda a: a.reshape((DEPTH * SSM_GROUPS,) + a.shape[2:])
    ssm = (_ssm_toeplitz(merge(k_lag)), merge(ssm_ws), merge(ssm_wc), merge(ssm_al))
    d_all = ssm_d.reshape(DEPTH, gw, 1).astype(F32)
    glu_wt = jnp.swapaxes(glu_w, 1, 2).astype(BF16)
    glu_b_all = glu_b.reshape(DEPTH, gw, 1).astype(F32)
    na_bias = jax.vmap(_na_bias_table)(na_rpb)
    fg = final_g.reshape(1, D_MODEL).astype(F32)
    nxt = (norm_all, w_tok, w_ssm_t)

    def trunk(x):
        b = x.shape[0]
        x2 = x.reshape(b * SEQ, D_MODEL)
        z2, zt = _proj(x2, nxt=nxt, nxt_layer=0)
        for l in range(DEPTH):
            z3 = z2.reshape(b, SEQ, TOK_W)
            y_a = _pool(z3, inv_cnt, pool_bd, pool_sc, l)
            y_b = _dilated(z3, cos_t, sin_t, mask_t)
            y_ct = _ssm_core(zt, *ssm, l)
            y_d = _neighbourhood(z3, na_bias, l)
            flat = lambda a: a.reshape(b * SEQ, gw)
            mix = (flat(y_a), flat(y_b), flat(y_d), y_ct, zt, d_all, glu_wt, glu_b_all, w_abd, w_c, fg)
            if l + 1 < DEPTH:
                x2, z2, zt = _proj(x2, mix=mix, mix_layer=l, nxt=nxt, nxt_layer=l + 1)
            else:
                (x2,) = _proj(x2, mix=mix, mix_layer=l, final=True)
        return x2.reshape(b, SEQ, D_MODEL)

    return (trunk(x_prompt), trunk(x_sample))
```

```python
import functools
import math

import numpy as np
import jax
import jax.numpy as jnp
from jax import lax
from jax.experimental import pallas as pl
from jax.experimental.pallas import tpu as pltpu

D_MODEL = 1024
SEQ = 2048
DEPTH = 4
GROUP_W = 256
HEAD_DIM = 64
POOL_WINDOWS = (2, 4, 8, 16)
POOL_GROUP = 64
DILATED_PATTERNS = ((128, 1), (512, 4), (2048, 16))
SSM_CH = 16
SSM_GROUPS = 16
SSM_STATE = 64
GRID_W = 64
NA_ROWS = 8
NA_COLS = 16
ROPE_THETA = 10000.0
EPS = 1e-6
PROJ_W = 12 * GROUP_W
NEG = -1e30

F32 = jnp.float32
BF16 = jnp.bfloat16
HI = lax.Precision.HIGHEST

ROW_TILE = 512
POOL_PAD = 16
DIL_Q = 128
DIL_R = 64
DIL_REGROUP = 4
STRIP = 32
ATT_UNROLL = 8
LOG2E = math.log2(math.e)
LN2 = math.log(2.0)
SSM_L = 128
SSM_NB_MAX = 16
SSM_PITCH = 24
TOK_W = 10 * GROUP_W
VMEM_LIMIT = 56 * 1024 * 1024


def _sigmoid(x):
    return 1.0 / (1.0 + jnp.exp(-x))


def _silu(x):
    return x * _sigmoid(x)


def _gelu_tanh(x):
    return 0.5 * x * (1.0 + jnp.tanh(math.sqrt(2.0 / math.pi) * (x + 0.044715 * (x * x * x))))


def _toeplitz(k, axis, a, b):
    k = jnp.moveaxis(k, axis, -1)
    w = a + b - 1
    lead = k.shape[:-1]
    kp = jnp.concatenate([k, jnp.zeros(lead + (1,), k.dtype)], axis=-1)
    t = jnp.broadcast_to(kp[..., None, :], lead + (a, w + 1)).reshape(lead + (a * (w + 1),))
    t = t[..., :a * w].reshape(lead + (a, w))[..., a - 1:a - 1 + b]
    return jnp.moveaxis(t, (-2, -1), (axis, axis + 1))


def _params(n_parallel=1, n_arbitrary=0):
    sem = ("parallel",) * n_parallel + ("arbitrary",) * n_arbitrary
    return pltpu.CompilerParams(dimension_semantics=sem, vmem_limit_bytes=VMEM_LIMIT)


def _proj_kernel(*refs, mix_out, proj_in, final):
    refs = list(refs)
    x_ref = refs.pop(0)
    if mix_out:
        (ya_ref, yb_ref, yd_ref, yct_ref, ztp_ref, d_ref, gwt_ref, gb_ref,
         wabd_ref, wc_ref, fg_ref) = refs[:11]
        del refs[:11]
    if proj_in:
        g_ref, w_ref = refs[:2]
        del refs[:2]
    if mix_out:
        xo_ref = refs.pop(0)
    if proj_in:
        z_ref, zt_ref = refs

    x = x_ref[...]
    if mix_out:
        u = ztp_ref[0:GROUP_W, :].astype(F32)
        y = u * d_ref[...] + yct_ref[...]
        g = _gelu_tanh(y)
        lin = jnp.dot(gwt_ref[...], g.astype(BF16), preferred_element_type=F32) + gb_ref[...]
        g = g * _sigmoid(lin)
        y_c = (g * _silu(ztp_ref[GROUP_W:2 * GROUP_W, :].astype(F32))).astype(BF16)
        y_abd = jnp.concatenate([ya_ref[...], yb_ref[...], yd_ref[...]], axis=1)
        acc = jnp.dot(y_abd, wabd_ref[...], preferred_element_type=F32)
        acc = acc + lax.dot_general(y_c, wc_ref[...], (((0,), (0,)), ((), ())),
                                    preferred_element_type=F32)
        x = x + acc
        if final:
            ms = jnp.mean(x * x, axis=-1, keepdims=True)
            x = x * lax.rsqrt(ms + EPS) * fg_ref[...]
        xo_ref[...] = x
    if proj_in:
        ms = jnp.mean(x * x, axis=-1, keepdims=True)
        h = (x * lax.rsqrt(ms + EPS) * g_ref[...]).astype(BF16)
        nch = 2 * GROUP_W
        for n, src in enumerate((0, 1, 2, 4, 5)):
            z_ref[:, n * nch:(n + 1) * nch] = jnp.dot(
                h, w_ref[:, src * nch:(src + 1) * nch], preferred_element_type=F32).astype(BF16)
        zt_ref[...] = lax.dot_general(w_ref[:, 3 * nch:4 * nch], h, (((0,), (1,)), ((), ())),
                                      preferred_element_type=F32).astype(BF16)


def _proj(x2, mix=None, mix_layer=None, nxt=None, nxt_layer=None, final=False):
    rows = x2.shape[0]
    per_seq = SEQ // ROW_TILE
    tile = lambda w: pl.BlockSpec((ROW_TILE, w), lambda i: (i, 0))
    chan = lambda c: pl.BlockSpec((None, c, ROW_TILE), lambda i: (i // per_seq, 0, i % per_seq))
    layer = lambda l, r, c: pl.BlockSpec((None, r, c), lambda i: (l, 0, 0))
    args, in_specs, out_specs, out_shape = [x2], [tile(D_MODEL)], [], []
    if mix is not None:
        args += list(mix)
        ml = mix_layer
        in_specs += [tile(GROUP_W), tile(GROUP_W), tile(GROUP_W), chan(GROUP_W), chan(2 * GROUP_W),
                     layer(ml, GROUP_W, 1), layer(ml, GROUP_W, GROUP_W), layer(ml, GROUP_W, 1),
                     layer(ml, 3 * GROUP_W, D_MODEL), layer(ml, GROUP_W, D_MODEL),
                     pl.BlockSpec((1, D_MODEL), lambda i: (0, 0))]
        out_specs.append(tile(D_MODEL))
        out_shape.append(jax.ShapeDtypeStruct((rows, D_MODEL), F32))
    if nxt is not None:
        args += list(nxt)
        nl = nxt_layer
        in_specs += [layer(nl, 1, D_MODEL), layer(nl, D_MODEL, PROJ_W)]
        out_specs += [tile(TOK_W), chan(2 * GROUP_W)]
        out_shape += [jax.ShapeDtypeStruct((rows, TOK_W), BF16),
                      jax.ShapeDtypeStruct((rows // SEQ, 2 * GROUP_W, SEQ), BF16)]
    return pl.pallas_call(
        functools.partial(_proj_kernel, mix_out=mix is not None, proj_in=nxt is not None, final=final),
        grid=(rows // ROW_TILE,),
        in_specs=in_specs, out_specs=out_specs, out_shape=out_shape,
        compiler_params=_params(),
        name="proj",
    )(*args)


def _pool_kernel(z_ref, inv_ref, w_ref, scale_ref, o_ref, ub, s2b, s4b, s8b):
    s = SEQ
    p = POOL_PAD
    n = s + p
    u = z_ref[:, 0:GROUP_W].astype(F32)
    zero_p = jnp.zeros((p, GROUP_W), F32)
    zero_8 = jnp.zeros((8, GROUP_W), F32)
    ub[0:p, :] = zero_p
    ub[p:p + s, :] = u
    ub[p + s:p + s + p, :] = zero_p
    for buf in (s2b, s4b, s8b):
        buf[0:8, :] = zero_8
        buf[s + 24:s + 32, :] = zero_8
    s2b[8:8 + n, :] = ub[7:7 + n, :] + ub[8:8 + n, :]
    s4b[8:8 + n, :] = s2b[7:7 + n, :] + s2b[9:9 + n, :]
    s8b[8:8 + n, :] = s4b[6:6 + n, :] + s4b[10:10 + n, :]
    s16 = s8b[p - 4:p - 4 + s, :] + s8b[p + 4:p + 4 + s, :]
    lane = lax.broadcasted_iota(jnp.int32, (s, GROUP_W), 1)
    win = jnp.where(lane < 64, s2b[p:p + s, :],
                    jnp.where(lane < 128, s4b[p:p + s, :],
                              jnp.where(lane < 192, s8b[p:p + s, :], s16)))
    diff = (win * inv_ref[...] - u).astype(BF16)
    y = jnp.dot(diff, w_ref[...], preferred_element_type=F32) * scale_ref[...]
    gate = z_ref[:, GROUP_W:2 * GROUP_W].astype(F32)
    o_ref[...] = (y * _silu(gate)).astype(BF16)


def _pool_inv_count():
    t = np.arange(SEQ)
    cols = []
    for w in POOL_WINDOWS:
        lo = np.clip(t - w // 2, 0, SEQ)
        hi = np.clip(t + w // 2, 0, SEQ)
        cols.append(np.repeat((1.0 / (hi - lo))[:, None], POOL_GROUP, axis=1))
    return jnp.asarray(np.concatenate(cols, axis=1), F32)


def _pool(z3, inv_cnt, w_bd, scale, l):
    b = z3.shape[0]
    return pl.pallas_call(
        _pool_kernel,
        grid=(b,),
        in_specs=[
            pl.BlockSpec((None, SEQ, 2 * GROUP_W), lambda i: (i, 0, 0)),
            pl.BlockSpec((SEQ, GROUP_W), lambda i: (0, 0)),
            pl.BlockSpec((None, GROUP_W, GROUP_W), lambda i: (l, 0, 0)),
            pl.BlockSpec((None, 1, GROUP_W), lambda i: (l, 0, 0)),
        ],
        out_specs=pl.BlockSpec((None, SEQ, GROUP_W), lambda i: (i, 0, 0)),
        out_shape=jax.ShapeDtypeStruct((b, SEQ, GROUP_W), BF16),
        scratch_shapes=[pltpu.VMEM((SEQ + 2 * POOL_PAD, GROUP_W), F32) for _ in range(4)],
        compiler_params=_params(),
        name="pool_mixer",
    )(z3, inv_cnt, w_bd, scale)


def _rope_tables():
    inv = ROPE_THETA ** (-np.arange(0, HEAD_DIM, 2, dtype=np.float64) / HEAD_DIM)
    ang = np.arange(SEQ, dtype=np.float64)[:, None] * inv[None, :]
    cos = np.concatenate([np.cos(ang), np.cos(ang)], axis=1)
    sin = np.concatenate([-np.sin(ang), np.sin(ang)], axis=1)
    return (jnp.asarray(np.tile(cos, (1, 2)), F32), jnp.asarray(np.tile(sin, (1, 2)), F32))


def _stack_heads(q):
    lane = lax.broadcasted_iota(jnp.int32, q.shape, 1)
    zero = jnp.zeros_like(q)
    return jnp.concatenate([jnp.where(lane < HEAD_DIM, q, zero),
                            jnp.where(lane < HEAD_DIM, zero, q)], axis=0)


def _unstack_heads(o, n):
    lane = lax.broadcasted_iota(jnp.int32, (n, 2 * HEAD_DIM), 1)
    return jnp.where(lane < HEAD_DIM, o[0:n], o[n:2 * n])


def _dilated_mask_table():
    q = np.arange(DIL_Q)[None, :, None]
    k = np.arange(2 * DIL_Q)[None, None, :]
    shift = (np.arange(3) * DIL_R)[:, None, None]
    return jnp.asarray(np.where(np.abs(k - q - shift) <= DIL_R, 0.0, NEG), F32)


def _dilated_kernel(qk_ref, vg_ref, cos_ref, sin_ref, mask_ref, o_ref, qf, kf, vf, q4, k4, v4,
                    acc_o, acc_l, s_scr, p_scr, m_scr):
    s = SEQ
    hd2 = 2 * HEAD_DIM
    lane2 = lax.broadcasted_iota(jnp.int32, (s, 2 * HEAD_DIM), 1)
    first_half = (lane2 % HEAD_DIM) < (HEAD_DIM // 2)
    cos = cos_ref[...]
    sin = sin_ref[...]
    for pair in range(2):
        lo = pair * 2 * HEAD_DIM
        for src, dst, scale in ((0, qf, HEAD_DIM ** -0.5 * LOG2E), (GROUP_W, kf, 1.0)):
            x = qk_ref[:, src + lo:src + lo + 2 * HEAD_DIM].astype(F32)
            swapped = jnp.where(first_half,
                                pltpu.roll(x, 2 * HEAD_DIM - HEAD_DIM // 2, 1),
                                pltpu.roll(x, HEAD_DIM // 2, 1))
            dst[pair] = (x * cos + swapped * sin) * scale
        vf[pair] = vg_ref[:, lo:lo + 2 * HEAD_DIM].astype(F32)

    quarter = s // DIL_REGROUP
    for pair in range(2):
        for nat, grp in ((qf, q4), (kf, k4), (vf, v4)):
            for rho in range(DIL_REGROUP):
                grp[pair, rho * quarter:(rho + 1) * quarter, :] = nat[
                    pair, pl.ds(rho, quarter, stride=DIL_REGROUP), :]

    for pi, (window, d) in enumerate(DILATED_PATTERNS):
        n_sub = s // d
        kw = min(2 * DIL_Q, n_sub)
        nb = n_sub // DIL_Q
        q_src, k_src, v_src = (qf, kf, vf) if d == 1 else (q4, k4, v4)

        def rows(rho, start, size, d=d):
            if d == 1:
                return pl.ds(pl.multiple_of(start, DIL_R), size)
            dd = d // DIL_REGROUP
            base = (rho % DIL_REGROUP) * quarter + rho // DIL_REGROUP + dd * start
            if dd == 1:
                return pl.ds(pl.multiple_of(base, DIL_R), size)
            return pl.ds(base, size, stride=dd)

        def block(trip, carry, d=d, n_sub=n_sub, kw=kw, nb=nb, pi=pi, rows=rows,
                  q_src=q_src, k_src=k_src, v_src=v_src):
            for sub in range(ATT_UNROLL):
                it = trip * ATT_UNROLL + sub
                rho = it // nb
                q0 = (it % nb) * DIL_Q
                ks = jnp.clip(q0 - DIL_R, 0, n_sub - kw)
                case = (q0 - ks) // DIL_R
                q_rows = rows(rho, q0, DIL_Q)
                k_rows = rows(rho, ks, kw)
                for pair in range(2):
                    slot = 2 * sub + pair
                    q = _stack_heads(q_src[pair, q_rows, :].astype(BF16))
                    k = k_src[pair, k_rows, :].astype(BF16)
                    v = _with_ones(v_src[pair, k_rows, :].astype(BF16))
                    s_scr[slot, :, 0:kw] = lax.dot_general(q, k, (((1,), (1,)), ((), ())),
                                                           preferred_element_type=F32)
                    _softmax_strips(
                        s_scr.at[slot], p_scr.at[slot],
                        lambda r0, r1, kw=kw, case=case: mask_ref[
                            case, r0 % DIL_Q:r0 % DIL_Q + (r1 - r0), 0:kw],
                        2 * DIL_Q, kw, m_scr.at[slot])
                    oa = jnp.dot(p_scr[slot, :, 0:kw], v, preferred_element_type=F32)
                    den = oa[:, hd2:2 * hd2]
                    o = oa[:, 0:hd2] / den
                    lse = (m_scr[slot] + jnp.log2(den)) * LN2
                    acc_o[pi, pair, q_rows, :] = _unstack_heads(o, DIL_Q)
                    acc_l[pi, pair, q_rows, :] = _unstack_heads(lse, DIL_Q)
            return carry

        lax.fori_loop(0, d * nb // ATT_UNROLL, block, 0)

    ch = 256
    for rho in range(DIL_REGROUP):
        for c in range(quarter // ch):
            g_rows = slice(rho * quarter + c * ch, rho * quarter + (c + 1) * ch)
            n_rows = pl.ds(rho + DIL_REGROUP * c * ch, ch, stride=DIL_REGROUP)
            for pair in range(2):
                l0, l1, l2 = acc_l[0, pair, n_rows, :], acc_l[1, pair, g_rows, :], acc_l[2, pair, g_rows, :]
                top = jnp.maximum(jnp.maximum(l0, l1), l2)
                w0, w1, w2 = jnp.exp(l0 - top), jnp.exp(l1 - top), jnp.exp(l2 - top)
                acc_o[0, pair, n_rows, :] = (
                    (w0 * acc_o[0, pair, n_rows, :] + w1 * acc_o[1, pair, g_rows, :]
                     + w2 * acc_o[2, pair, g_rows, :]) / (w0 + w1 + w2))
    for c in range(s // ch):
        r = slice(c * ch, (c + 1) * ch)
        for pair in range(2):
            lo = pair * 2 * HEAD_DIM
            gate = vg_ref[r, GROUP_W + lo:GROUP_W + lo + 2 * HEAD_DIM].astype(F32)
            o_ref[r, lo:lo + 2 * HEAD_DIM] = (acc_o[0, pair, r, :] * _silu(gate)).astype(BF16)


def _dilated(z3, cos_t, sin_t, mask_t):
    b = z3.shape[0]
    return pl.pallas_call(
        _dilated_kernel,
        grid=(b,),
        in_specs=[
            pl.BlockSpec((None, SEQ, 2 * GROUP_W), lambda i: (i, 0, 1)),
            pl.BlockSpec((None, SEQ, 2 * GROUP_W), lambda i: (i, 0, 2)),
            pl.BlockSpec((SEQ, 2 * HEAD_DIM), lambda i: (0, 0)),
            pl.BlockSpec((SEQ, 2 * HEAD_DIM), lambda i: (0, 0)),
            pl.BlockSpec(mask_t.shape, lambda i: (0, 0, 0)),
        ],
        out_specs=pl.BlockSpec((None, SEQ, GROUP_W), lambda i: (i, 0, 0)),
        out_shape=jax.ShapeDtypeStruct((b, SEQ, GROUP_W), BF16),
        scratch_shapes=[pltpu.VMEM((2, SEQ, 2 * HEAD_DIM), F32) for _ in range(6)]
        + [pltpu.VMEM((len(DILATED_PATTERNS), 2, SEQ, 2 * HEAD_DIM), F32) for _ in range(2)]
        + [pltpu.VMEM((2 * ATT_UNROLL, 2 * DIL_Q, 2 * DIL_Q), F32),
           pltpu.VMEM((2 * ATT_UNROLL, 2 * DIL_Q, 2 * DIL_Q), BF16),
           pltpu.VMEM((2 * ATT_UNROLL, 2 * DIL_Q, 2 * HEAD_DIM), F32)],
        compiler_params=_params(),
        name="dilated_attention",
    )(z3, z3, cos_t, sin_t, mask_t)


def _na_bias_table(rpb):
    c = np.arange(GRID_W)
    cs = np.clip(c - NA_COLS // 2, 0, GRID_W - NA_COLS)
    col_ok = (c[None, :] >= cs[:, None]) & (c[None, :] < cs[:, None] + NA_COLS)
    pad = GRID_W - NA_COLS
    tab = jnp.pad(rpb.astype(F32), ((0, 0), (0, 0), (pad, pad)))
    tab = _toeplitz(tab, 2, GRID_W, GRID_W)
    tab = jnp.where(col_ok[None, None, :, :], tab * LOG2E, NEG)
    h, nd = rpb.shape[0], 2 * NA_ROWS - 1
    tab = tab.reshape(h // 2, 2, nd, GRID_W, GRID_W).transpose(0, 2, 1, 3, 4)
    tab = tab.reshape(h // 2, nd, 2 * GRID_W, GRID_W)
    return jnp.concatenate([tab[:, :-1], tab[:, 1:]], axis=-1)


def _softmax_strips(s_scr, p_scr, bias, n_rows, n_keys, m_scr=None):
    for r0 in range(0, n_rows, STRIP):
        t = s_scr[r0:r0 + STRIP, 0:n_keys] + bias(r0, r0 + STRIP)
        m = jnp.max(t, axis=1, keepdims=True)
        p_scr[r0:r0 + STRIP, 0:n_keys] = jnp.exp2(t - m).astype(BF16)
        if m_scr is not None:
            m_scr[r0:r0 + STRIP, :] = jnp.broadcast_to(m, (STRIP, 2 * HEAD_DIM))


def _with_ones(v):
    return jnp.concatenate([v, jnp.ones_like(v)], axis=1)


def _na_kernel(qk_ref, vg_ref, bias_ref, o_ref, s_scr, p_scr, bias_scr):
    n_rows = SEQ // GRID_W
    nk = NA_ROWS * GRID_W
    hd2 = 2 * HEAD_DIM

    @pl.when(pl.program_id(0) == 0)
    def _():
        for pair in range(2):
            for off in range(NA_ROWS):
                for j in range(NA_ROWS // 2):
                    bias_scr[pair, off, :, 2 * j * GRID_W:(2 * j + 2) * GRID_W] = bias_ref[
                        pair, 2 * j - off + NA_ROWS - 1]

    def rows_trip(trip, carry):
        for sub in range(ATT_UNROLL):
            r = trip * ATT_UNROLL + sub
            rs = jnp.clip(r - NA_ROWS // 2, 0, n_rows - NA_ROWS)
            off = r - rs
            q_rows = pl.ds(pl.multiple_of(r * GRID_W, GRID_W), GRID_W)
            k_rows = pl.ds(pl.multiple_of(rs * GRID_W, GRID_W), nk)
            for pair in range(2):
                slot = 2 * sub + pair
                lo = pair * hd2
                q = qk_ref[q_rows, lo:lo + hd2].astype(F32) * (HEAD_DIM ** -0.5 * LOG2E)
                q = _stack_heads(q.astype(BF16))
                k = qk_ref[k_rows, GROUP_W + lo:GROUP_W + lo + hd2]
                v = _with_ones(vg_ref[k_rows, lo:lo + hd2])
                s_scr[slot] = lax.dot_general(q, k, (((1,), (1,)), ((), ())), preferred_element_type=F32)
                _softmax_strips(s_scr.at[slot], p_scr.at[slot],
                                lambda r0, r1, pair=pair, off=off: bias_scr[pair, off, r0:r1, :],
                                2 * GRID_W, nk)
                oa = jnp.dot(p_scr[slot], v, preferred_element_type=F32)
                o = oa[:, 0:hd2] / oa[:, hd2:2 * hd2]
                gate = vg_ref[q_rows, GROUP_W + lo:GROUP_W + lo + hd2].astype(F32)
                o_ref[q_rows, lo:lo + hd2] = (_unstack_heads(o, GRID_W) * _silu(gate)).astype(BF16)
        return carry

    lax.fori_loop(0, n_rows // ATT_UNROLL, rows_trip, 0)


def _neighbourhood(z3, bias_tab, l):
    b = z3.shape[0]
    return pl.pallas_call(
        _na_kernel,
        grid=(b,),
        in_specs=[
            pl.BlockSpec((None, SEQ, 2 * GROUP_W), lambda i: (i, 0, 3)),
            pl.BlockSpec((None, SEQ, 2 * GROUP_W), lambda i: (i, 0, 4)),
            pl.BlockSpec((None,) + bias_tab.shape[1:], lambda i: (l, 0, 0, 0, 0)),
        ],
        out_specs=pl.BlockSpec((None, SEQ, GROUP_W), lambda i: (i, 0, 0)),
        out_shape=jax.ShapeDtypeStruct((b, SEQ, GROUP_W), BF16),
        scratch_shapes=[pltpu.VMEM((2 * ATT_UNROLL, 2 * GRID_W, NA_ROWS * GRID_W), F32),
                        pltpu.VMEM((2 * ATT_UNROLL, 2 * GRID_W, NA_ROWS * GRID_W), BF16),
                        pltpu.VMEM((2, NA_ROWS, 2 * GRID_W, NA_ROWS * GRID_W), F32)],
        compiler_params=_params(0, 1),
        name="neighbourhood_attention",
    )(z3, z3, bias_tab)


def _ssm_weights(a_re, a_im, log_dt, b_re, b_im, c_re, c_im):
    L, G, P, C = SSM_L, SSM_GROUPS, SSM_STATE, SSM_CH
    a_re, a_im, log_dt = a_re.astype(F32), a_im.astype(F32), log_dt.astype(F32)
    dt = jnp.exp(log_dt)[..., None]
    ks = jnp.arange(L + 1, dtype=F32)
    mag = jnp.exp((a_re * dt)[..., None] * ks)
    ang = (a_im * dt)[..., None] * ks
    pr, pim = mag * jnp.cos(ang), mag * jnp.sin(ang)
    abr, abi = pr[..., 1], pim[..., 1]
    den = a_re * a_re + a_im * a_im
    gr = ((abr - 1.0) * a_re + abi * a_im) / den
    gi = (abi * a_re - (abr - 1.0) * a_im) / den
    br, bi = b_re.astype(F32), b_im.astype(F32)
    bbr = gr[..., None] * br - gi[..., None] * bi
    bbi = gr[..., None] * bi + gi[..., None] * br
    cr, ci = c_re.astype(F32), c_im.astype(F32)
    cb_r = cr[:, :, None] * bbr.transpose(0, 1, 3, 2)[:, :, :, None] \
        - ci[:, :, None] * bbi.transpose(0, 1, 3, 2)[:, :, :, None]
    cb_i = cr[:, :, None] * bbi.transpose(0, 1, 3, 2)[:, :, :, None] \
        + ci[:, :, None] * bbr.transpose(0, 1, 3, 2)[:, :, :, None]
    kern = (jnp.einsum('dgxcp,dgpk->dgxck', cb_r, pr, precision=HI)
            - jnp.einsum('dgxcp,dgpk->dgxck', cb_i, pim, precision=HI))
    kf, kb = kern[0], kern[1]
    k_lag = jnp.concatenate([kb[..., 1:L][..., ::-1], (kf[..., 0] + kb[..., 0])[..., None],
                             kf[..., 1:L], jnp.zeros(kf.shape[:-1] + (1,), F32)], axis=-1)
    k_lag = k_lag.reshape(G, C * C, 2 * L)

    def state_in(d, idx):
        a_r = pr[d][:, :, idx].transpose(0, 2, 1)[:, None]
        a_i = pim[d][:, :, idx].transpose(0, 2, 1)[:, None]
        b_r = bbr[d].transpose(0, 2, 1)[:, :, None]
        b_i = bbi[d].transpose(0, 2, 1)[:, :, None]
        return a_r * b_r - a_i * b_i, a_r * b_i + a_i * b_r

    sf_r, sf_i = state_in(0, slice(L - 1, None, -1))
    sb_r, sb_i = state_in(1, slice(0, L))
    w_s = jnp.concatenate([sf_r, sb_r, sf_i, sb_i], axis=-1).reshape(G, C * L, 4 * P)

    def state_out(d, idx):
        c_r = cr[d].transpose(0, 2, 1)[..., None]
        c_i = ci[d].transpose(0, 2, 1)[..., None]
        a_r = pr[d][:, :, idx][:, :, None, :]
        a_i = pim[d][:, :, idx][:, :, None, :]
        return c_r * a_r - c_i * a_i, c_r * a_i + c_i * a_r

    ff_r, ff_i = state_out(0, slice(1, L + 1))
    fb_r, fb_i = state_out(1, slice(L, 0, -1))
    w_c = jnp.concatenate([ff_r, fb_r, -ff_i, -fb_i], axis=1).reshape(G, 4 * P, C * L)
    a_l = jnp.concatenate([pr[0, ..., L], pr[1, ..., L], pim[0, ..., L], pim[1, ..., L]], axis=-1)
    a_l = jnp.broadcast_to(a_l[:, None, :], (G, SSM_NB_MAX, 4 * P))
    return k_lag, w_s.astype(BF16), w_c.astype(BF16), a_l


def _ssm_toeplitz_kernel(k_ref, t_ref):
    L, C = SSM_L, SSM_CH

    def build(cp, carry):
        for c in range(C):
            lag = jnp.broadcast_to(k_ref[pl.ds(cp * C + c, 1), :], (L, 2 * L))
            tile = pltpu.roll(lag, L + 1, 1, stride=1, stride_axis=0)[:, 0:L]
            t_ref[pl.ds(pl.multiple_of(cp * L, L), L), c * L:(c + 1) * L] = tile.astype(BF16)
        return carry

    lax.fori_loop(0, C, build, 0)


def _ssm_toeplitz(k_lag):
    n, C, L = k_lag.shape[0], SSM_CH, SSM_L
    return pl.pallas_call(
        _ssm_toeplitz_kernel,
        grid=(n,),
        in_specs=[pl.BlockSpec((None, C * C, 2 * L), lambda g: (g, 0, 0))],
        out_specs=pl.BlockSpec((None, C * L, C * L), lambda g: (g, 0, 0)),
        out_shape=jax.ShapeDtypeStruct((n, C * L, C * L), BF16),
        compiler_params=_params(),
        name="ssm_toeplitz",
    )(k_lag)


def _ssm_kernel(u_ref, t_ref, ws_ref, wc_ref, al_ref, y_ref, x_scr, y_scr,
                s_re, s_im, ha_re, ha_im, hb_re, hb_im, *, nb):
    L, C, P = SSM_L, SSM_CH, SSM_STATE
    nc = SEQ // L
    m = nb * nc
    half = 2 * P

    for b in range(nb):
        ub = u_ref[b].astype(F32)
        for j in range(nc):
            r0 = (b * nc + j) * SSM_PITCH
            x_scr[r0:r0 + C, :] = ub[:, j * L:(j + 1) * L]
    lhs = jnp.concatenate([x_scr[pl.ds(c, m, stride=SSM_PITCH), :].astype(BF16) for c in range(C)],
                          axis=1)

    s = jnp.dot(lhs, ws_ref[...], preferred_element_type=F32)
    s_re[...] = s[:, 0:half]
    s_im[...] = s[:, half:2 * half]
    a_r = al_ref[0:nb, 0:half]
    a_i = al_ref[0:nb, half:2 * half]
    fwd = lax.broadcasted_iota(jnp.int32, (nb, half), 1) < P
    h_r = jnp.zeros((nb, half), F32)
    h_i = jnp.zeros((nb, half), F32)
    for k in range(nc):
        rf = pl.ds(k, nb, stride=nc)
        rb = pl.ds(nc - 1 - k, nb, stride=nc)
        ha_re[rf, :] = h_r
        ha_im[rf, :] = h_i
        hb_re[rb, :] = h_r
        hb_im[rb, :] = h_i
        s_r = jnp.where(fwd, s_re[rf, :], s_re[rb, :])
        s_i = jnp.where(fwd, s_im[rf, :], s_im[rb, :])
        h_r, h_i = a_r * h_r - a_i * h_i + s_r, a_r * h_i + a_i * h_r + s_i
    fwd_m = lax.broadcasted_iota(jnp.int32, (m, half), 1) < P
    h_all = jnp.concatenate([jnp.where(fwd_m, ha_re[...], hb_re[...]),
                             jnp.where(fwd_m, ha_im[...], hb_im[...])], axis=1).astype(BF16)
    y = jnp.dot(lhs, t_ref[...], preferred_element_type=F32)
    y = y + jnp.dot(h_all, wc_ref[...], preferred_element_type=F32)
    for c in range(C):
        y_scr[pl.ds(c, m, stride=SSM_PITCH), :] = y[:, c * L:(c + 1) * L]
    for b in range(nb):
        y_ref[b] = jnp.concatenate(
            [y_scr[(b * nc + j) * SSM_PITCH:(b * nc + j) * SSM_PITCH + C, :] for j in range(nc)], axis=1)


def _ssm_core(zt, w_t, w_s, w_c, a_l, l):
    b = zt.shape[0]
    L, G, C, P = SSM_L, SSM_GROUPS, SSM_CH, SSM_STATE
    nb = min(b, SSM_NB_MAX)
    m = nb * (SEQ // L)
    return pl.pallas_call(
        functools.partial(_ssm_kernel, nb=nb),
        grid=(G, b // nb),
        in_specs=[
            pl.BlockSpec((nb, C, SEQ), lambda g, i: (i, g, 0)),
            pl.BlockSpec((None, C * L, C * L), lambda g, i: (l * G + g, 0, 0)),
            pl.BlockSpec((None, C * L, 4 * P), lambda g, i: (l * G + g, 0, 0)),
            pl.BlockSpec((None, 4 * P, C * L), lambda g, i: (l * G + g, 0, 0)),
            pl.BlockSpec((None, SSM_NB_MAX, 4 * P), lambda g, i: (l * G + g, 0, 0)),
        ],
        out_specs=pl.BlockSpec((nb, C, SEQ), lambda g, i: (i, g, 0)),
        out_shape=jax.ShapeDtypeStruct((b, G * C, SEQ), F32),
        scratch_shapes=[pltpu.VMEM((m * SSM_PITCH, L), F32), pltpu.VMEM((m * SSM_PITCH, L), F32)]
        + [pltpu.VMEM((m, 2 * P), F32) for _ in range(6)],
        compiler_params=_params(2),
        name="ssm_core",
    )(zt, w_t, w_s, w_c, a_l)


def _block_diag(w):
    n, k = w.shape[0], w.shape[1]
    out = jnp.zeros((n * k, n * k), w.dtype)
    for i in range(n):
        out = lax.dynamic_update_slice(out, w[i], (i * k, i * k))
    return out


def kernel(x_prompt, x_sample, norm_g, w_in, w_out, pool_w, pool_scale, ssm_a_re, ssm_a_im,
           ssm_log_dt, ssm_b_re, ssm_b_im, ssm_c_re, ssm_c_im, ssm_d, glu_w, glu_b, na_rpb, final_g):
    inv_cnt = _pool_inv_count()
    cos_t, sin_t = _rope_tables()
    mask_t = _dilated_mask_table()
    gw = GROUP_W
    norm_all = norm_g.reshape(DEPTH, 1, D_MODEL).astype(F32)
    w_abd = jnp.concatenate([w_out[:, :2 * gw], w_out[:, 3 * gw:]], axis=1).astype(BF16)
    w_c = w_out[:, 2 * gw:3 * gw].astype(BF16)
    pool_bd = jax.vmap(_block_diag)(pool_w).astype(BF16)
    pool_sc = pool_scale.reshape(DEPTH, 1, gw).astype(F32)
    k_lag, ssm_ws, ssm_wc, ssm_al = jax.vmap(_ssm_weights)(
        ssm_a_re, ssm_a_im, ssm_log_dt, ssm_b_re, ssm_b_im, ssm_c_re, ssm_c_im)
    merge = lambda a: a.reshape((DEPTH * SSM_GROUPS,) + a.shape[2:])
    ssm = (_ssm_toeplitz(merge(k_lag)), merge(ssm_ws), merge(ssm_wc), merge(ssm_al))
    d_all = ssm_d.reshape(DEPTH, gw, 1).astype(F32)
    glu_wt = jnp.swapaxes(glu_w, 1, 2).astype(BF16)
    glu_b_all = glu_b.reshape(DEPTH, gw, 1).astype(F32)
    na_bias = jax.vmap(_na_bias_table)(na_rpb)
    fg = final_g.reshape(1, D_MODEL).astype(F32)
    nxt = (norm_all, w_in.astype(BF16))

    def trunk(x):
        b = x.shape[0]
        x2 = x.reshape(b * SEQ, D_MODEL)
        z2, zt = _proj(x2, nxt=nxt, nxt_layer=0)
        for l in range(DEPTH):
            z3 = z2.reshape(b, SEQ, TOK_W)
            y_a = _pool(z3, inv_cnt, pool_bd, pool_sc, l)
            y_b = _dilated(z3, cos_t, sin_t, mask_t)
            y_ct = _ssm_core(zt, *ssm, l)
            y_d = _neighbourhood(z3, na_bias, l)
            flat = lambda a: a.reshape(b * SEQ, gw)
            mix = (flat(y_a), flat(y_b), flat(y_d), y_ct, zt, d_all, glu_wt, glu_b_all, w_abd, w_c, fg)
            if l + 1 < DEPTH:
                x2, z2, zt = _proj(x2, mix=mix, mix_layer=l, nxt=nxt, nxt_layer=l + 1)
            else:
                (x2,) = _proj(x2, mix=mix, mix_layer=l, final=True)
        return x2.reshape(b, SEQ, D_MODEL)

    return (trunk(x_prompt), trunk(x_sample))
```

```python
import functools
import math

import numpy as np
import jax
import jax.numpy as jnp
from jax import lax
from jax.experimental import pallas as pl
from jax.experimental.pallas import tpu as pltpu

D_MODEL = 1024
SEQ = 2048
DEPTH = 4
GROUP_W = 256
HEAD_DIM = 64
POOL_WINDOWS = (2, 4, 8, 16)
POOL_GROUP = 64
DILATED_PATTERNS = ((128, 1), (512, 4), (2048, 16))
SSM_CH = 16
SSM_GROUPS = 16
SSM_STATE = 64
GRID_W = 64
NA_ROWS = 8
NA_COLS = 16
ROPE_THETA = 10000.0
EPS = 1e-6
PROJ_W = 12 * GROUP_W
NEG = -1e30

F32 = jnp.float32
BF16 = jnp.bfloat16
HI = lax.Precision.HIGHEST

ROW_TILE = 512
POOL_PAD = 16
DIL_Q = 128
DIL_R = 64
DIL_REGROUP = 4
STRIP = 32
ATT_UNROLL = 8
LOG2E = math.log2(math.e)
SSM_L = 128
SSM_NB = 8
SSM_PITCH = 24
TOK_W = 10 * GROUP_W
VMEM_LIMIT = 56 * 1024 * 1024


def _sigmoid(x):
    return 1.0 / (1.0 + jnp.exp(-x))


def _silu(x):
    return x * _sigmoid(x)


def _gelu_tanh(x):
    return 0.5 * x * (1.0 + jnp.tanh(math.sqrt(2.0 / math.pi) * (x + 0.044715 * (x * x * x))))


def _toeplitz(k, axis, a, b):
    k = jnp.moveaxis(k, axis, -1)
    w = a + b - 1
    lead = k.shape[:-1]
    kp = jnp.concatenate([k, jnp.zeros(lead + (1,), k.dtype)], axis=-1)
    t = jnp.broadcast_to(kp[..., None, :], lead + (a, w + 1)).reshape(lead + (a * (w + 1),))
    t = t[..., :a * w].reshape(lead + (a, w))[..., a - 1:a - 1 + b]
    return jnp.moveaxis(t, (-2, -1), (axis, axis + 1))


def _params(n_parallel=1, n_arbitrary=0):
    sem = ("parallel",) * n_parallel + ("arbitrary",) * n_arbitrary
    return pltpu.CompilerParams(dimension_semantics=sem, vmem_limit_bytes=VMEM_LIMIT)


def _proj_kernel(*refs, mix_out, proj_in, final):
    refs = list(refs)
    x_ref = refs.pop(0)
    if mix_out:
        (ya_ref, yb_ref, yd_ref, yct_ref, ztp_ref, d_ref, gwt_ref, gb_ref,
         wabd_ref, wc_ref, fg_ref) = refs[:11]
        del refs[:11]
    if proj_in:
        g_ref, w_ref = refs[:2]
        del refs[:2]
    if mix_out:
        xo_ref = refs.pop(0)
    if proj_in:
        z_ref, zt_ref = refs

    x = x_ref[...]
    if mix_out:
        u = ztp_ref[0:GROUP_W, :].astype(F32)
        y = u * d_ref[...] + yct_ref[...]
        g = _gelu_tanh(y)
        lin = jnp.dot(gwt_ref[...], g.astype(BF16), preferred_element_type=F32) + gb_ref[...]
        g = g * _sigmoid(lin)
        y_c = (g * _silu(ztp_ref[GROUP_W:2 * GROUP_W, :].astype(F32))).astype(BF16)
        y_abd = jnp.concatenate([ya_ref[...], yb_ref[...], yd_ref[...]], axis=1)
        acc = jnp.dot(y_abd, wabd_ref[...], preferred_element_type=F32)
        acc = acc + lax.dot_general(y_c, wc_ref[...], (((0,), (0,)), ((), ())),
                                    preferred_element_type=F32)
        x = x + acc
        if final:
            ms = jnp.mean(x * x, axis=-1, keepdims=True)
            x = x * lax.rsqrt(ms + EPS) * fg_ref[...]
        xo_ref[...] = x
    if proj_in:
        ms = jnp.mean(x * x, axis=-1, keepdims=True)
        h = (x * lax.rsqrt(ms + EPS) * g_ref[...]).astype(BF16)
        nch = 2 * GROUP_W
        for n, src in enumerate((0, 1, 2, 4, 5)):
            z_ref[:, n * nch:(n + 1) * nch] = jnp.dot(
                h, w_ref[:, src * nch:(src + 1) * nch], preferred_element_type=F32).astype(BF16)
        zt_ref[...] = lax.dot_general(w_ref[:, 3 * nch:4 * nch], h, (((0,), (1,)), ((), ())),
                                      preferred_element_type=F32).astype(BF16)


def _proj(x2, mix=None, mix_layer=None, nxt=None, nxt_layer=None, final=False):
    rows = x2.shape[0]
    per_seq = SEQ // ROW_TILE
    tile = lambda w: pl.BlockSpec((ROW_TILE, w), lambda i: (i, 0))
    chan = lambda c: pl.BlockSpec((None, c, ROW_TILE), lambda i: (i // per_seq, 0, i % per_seq))
    layer = lambda l, r, c: pl.BlockSpec((None, r, c), lambda i: (l, 0, 0))
    args, in_specs, out_specs, out_shape = [x2], [tile(D_MODEL)], [], []
    if mix is not None:
        args += list(mix)
        ml = mix_layer
        in_specs += [tile(GROUP_W), tile(GROUP_W), tile(GROUP_W), chan(GROUP_W), chan(2 * GROUP_W),
                     layer(ml, GROUP_W, 1), layer(ml, GROUP_W, GROUP_W), layer(ml, GROUP_W, 1),
                     layer(ml, 3 * GROUP_W, D_MODEL), layer(ml, GROUP_W, D_MODEL),
                     pl.BlockSpec((1, D_MODEL), lambda i: (0, 0))]
        out_specs.append(tile(D_MODEL))
        out_shape.append(jax.ShapeDtypeStruct((rows, D_MODEL), F32))
    if nxt is not None:
        args += list(nxt)
        nl = nxt_layer
        in_specs += [layer(nl, 1, D_MODEL), layer(nl, D_MODEL, PROJ_W)]
        out_specs += [tile(TOK_W), chan(2 * GROUP_W)]
        out_shape += [jax.ShapeDtypeStruct((rows, TOK_W), BF16),
                      jax.ShapeDtypeStruct((rows // SEQ, 2 * GROUP_W, SEQ), BF16)]
    return pl.pallas_call(
        functools.partial(_proj_kernel, mix_out=mix is not None, proj_in=nxt is not None, final=final),
        grid=(rows // ROW_TILE,),
        in_specs=in_specs, out_specs=out_specs, out_shape=out_shape,
        compiler_params=_params(),
        name="proj",
    )(*args)


def _pool_kernel(z_ref, inv_ref, w_ref, scale_ref, o_ref, ub, s2b, s4b, s8b):
    s = SEQ
    p = POOL_PAD
    n = s + p
    u = z_ref[:, 0:GROUP_W].astype(F32)
    zero_p = jnp.zeros((p, GROUP_W), F32)
    zero_8 = jnp.zeros((8, GROUP_W), F32)
    ub[0:p, :] = zero_p
    ub[p:p + s, :] = u
    ub[p + s:p + s + p, :] = zero_p
    half = GROUP_W // 2
    for buf in (s2b, s4b, s8b):
        buf[0:8, :] = zero_8[:, 0:buf.shape[1]]
        buf[s + 24:s + 32, :] = zero_8[:, 0:buf.shape[1]]
    s2b[8:8 + n, :] = ub[7:7 + n, :] + ub[8:8 + n, :]
    s4b[8:8 + n, :] = s2b[7:7 + n, :] + s2b[9:9 + n, :]
    s8b[8:8 + n, :] = s4b[6:6 + n, half:] + s4b[10:10 + n, half:]
    s16 = s8b[p - 4:p - 4 + s, :] + s8b[p + 4:p + 4 + s, :]
    lane = lax.broadcasted_iota(jnp.int32, (s, half), 1)
    win = jnp.concatenate([
        jnp.where(lane < POOL_GROUP, s2b[p:p + s, 0:half], s4b[p:p + s, 0:half]),
        jnp.where(lane < POOL_GROUP, s8b[p:p + s, :], s16)], axis=1)
    diff = (win * inv_ref[...] - u).astype(BF16)
    y = jnp.dot(diff, w_ref[...], preferred_element_type=F32) * scale_ref[...]
    gate = z_ref[:, GROUP_W:2 * GROUP_W].astype(F32)
    o_ref[...] = (y * _silu(gate)).astype(BF16)


def _pool_inv_count():
    t = np.arange(SEQ)
    cols = []
    for w in POOL_WINDOWS:
        lo = np.clip(t - w // 2, 0, SEQ)
        hi = np.clip(t + w // 2, 0, SEQ)
        cols.append(np.repeat((1.0 / (hi - lo))[:, None], POOL_GROUP, axis=1))
    return jnp.asarray(np.concatenate(cols, axis=1), F32)


def _pool(z3, inv_cnt, w_bd, scale, l):
    b = z3.shape[0]
    return pl.pallas_call(
        _pool_kernel,
        grid=(b,),
        in_specs=[
            pl.BlockSpec((None, SEQ, 2 * GROUP_W), lambda i: (i, 0, 0)),
            pl.BlockSpec((SEQ, GROUP_W), lambda i: (0, 0)),
            pl.BlockSpec((None, GROUP_W, GROUP_W), lambda i: (l, 0, 0)),
            pl.BlockSpec((None, 1, GROUP_W), lambda i: (l, 0, 0)),
        ],
        out_specs=pl.BlockSpec((None, SEQ, GROUP_W), lambda i: (i, 0, 0)),
        out_shape=jax.ShapeDtypeStruct((b, SEQ, GROUP_W), BF16),
        scratch_shapes=[pltpu.VMEM((SEQ + 2 * POOL_PAD, GROUP_W), F32) for _ in range(3)]
        + [pltpu.VMEM((SEQ + 2 * POOL_PAD, GROUP_W // 2), F32)],
        compiler_params=_params(),
        name="pool_mixer",
    )(z3, inv_cnt, w_bd, scale)


def _rope_tables():
    inv = ROPE_THETA ** (-np.arange(0, HEAD_DIM, 2, dtype=np.float64) / HEAD_DIM)
    ang = np.arange(SEQ, dtype=np.float64)[:, None] * inv[None, :]
    cos = np.concatenate([np.cos(ang), np.cos(ang)], axis=1)
    sin = np.concatenate([-np.sin(ang), np.sin(ang)], axis=1)
    return (jnp.asarray(np.tile(cos, (1, 2)), F32), jnp.asarray(np.tile(sin, (1, 2)), F32))


def _stack_heads(q):
    lane = lax.broadcasted_iota(jnp.int32, q.shape, 1)
    zero = jnp.zeros_like(q)
    return jnp.concatenate([jnp.where(lane < HEAD_DIM, q, zero),
                            jnp.where(lane < HEAD_DIM, zero, q)], axis=0)


def _unstack_heads(o, n):
    lane = lax.broadcasted_iota(jnp.int32, (n, 2 * HEAD_DIM), 1)
    return jnp.where(lane < HEAD_DIM, o[0:n], o[n:2 * n])


def _dilated_mask_table():
    q = np.arange(DIL_Q)[None, :, None]
    k = np.arange(2 * DIL_Q)[None, None, :]
    shift = (np.arange(3) * DIL_R)[:, None, None]
    return jnp.asarray(np.where(np.abs(k - q - shift) <= DIL_R, 0.0, NEG), F32)


def _dilated_kernel(qk_ref, vg_ref, cos_ref, sin_ref, mask_ref, o_ref, qf, kf, vf, q4, k4, v4,
                    acc_o, acc_l, s_scr, p_scr, m_scr):
    s = SEQ
    hd2 = 2 * HEAD_DIM
    lane2 = lax.broadcasted_iota(jnp.int32, (s, 2 * HEAD_DIM), 1)
    first_half = (lane2 % HEAD_DIM) < (HEAD_DIM // 2)
    cos = cos_ref[...]
    sin = sin_ref[...]
    for pair in range(2):
        lo = pair * 2 * HEAD_DIM
        for src, dst, scale in ((0, qf, HEAD_DIM ** -0.5 * LOG2E), (GROUP_W, kf, 1.0)):
            x = qk_ref[:, src + lo:src + lo + 2 * HEAD_DIM].astype(F32)
            swapped = jnp.where(first_half,
                                pltpu.roll(x, 2 * HEAD_DIM - HEAD_DIM // 2, 1),
                                pltpu.roll(x, HEAD_DIM // 2, 1))
            roped = x * cos + swapped * sin
            dst[pair] = roped if scale == 1.0 else roped * scale
        vf[pair] = vg_ref[:, lo:lo + 2 * HEAD_DIM].astype(F32)

    quarter = s // DIL_REGROUP
    for pair in range(2):
        for nat, grp in ((qf, q4), (kf, k4), (vf, v4)):
            for rho in range(DIL_REGROUP):
                grp[pair, rho * quarter:(rho + 1) * quarter, :] = nat[
                    pair, pl.ds(rho, quarter, stride=DIL_REGROUP), :]

    for pi, (window, d) in enumerate(DILATED_PATTERNS):
        n_sub = s // d
        kw = min(2 * DIL_Q, n_sub)
        nb = n_sub // DIL_Q
        q_src, k_src, v_src = (qf, kf, vf) if d == 1 else (q4, k4, v4)

        def rows(rho, start, size, d=d):
            if d == 1:
                return pl.ds(pl.multiple_of(start, DIL_R), size)
            dd = d // DIL_REGROUP
            base = (rho % DIL_REGROUP) * quarter + rho // DIL_REGROUP + dd * start
            if dd == 1:
                return pl.ds(pl.multiple_of(base, DIL_R), size)
            return pl.ds(base, size, stride=dd)

        def block(trip, carry, d=d, n_sub=n_sub, kw=kw, nb=nb, pi=pi, rows=rows,
                  q_src=q_src, k_src=k_src, v_src=v_src):
            for sub in range(ATT_UNROLL):
                it = trip * ATT_UNROLL + sub
                rho = it // nb
                q0 = (it % nb) * DIL_Q
                ks = jnp.clip(q0 - DIL_R, 0, n_sub - kw)
                case = (q0 - ks) // DIL_R
                q_rows = rows(rho, q0, DIL_Q)
                k_rows = rows(rho, ks, kw)
                for pair in range(2):
                    slot = 2 * sub + pair
                    q = _stack_heads(q_src[pair, q_rows, :].astype(BF16))
                    k = k_src[pair, k_rows, :].astype(BF16)
                    v = _with_ones(v_src[pair, k_rows, :].astype(BF16))
                    s_scr[slot, :, 0:kw] = lax.dot_general(q, k, (((1,), (1,)), ((), ())),
                                                           preferred_element_type=F32)
                    _softmax_strips(
                        s_scr.at[slot], p_scr.at[slot],
                        lambda r0, r1, kw=kw, case=case: mask_ref[
                            case, r0 % DIL_Q:r0 % DIL_Q + (r1 - r0), 0:kw],
                        2 * DIL_Q, kw, m_scr.at[slot])
                    oa = jnp.dot(p_scr[slot, :, 0:kw], v, preferred_element_type=F32)
                    den = oa[:, hd2:2 * hd2]
                    o = oa[:, 0:hd2] / den
                    lse = m_scr[slot] + jnp.log2(den)
                    acc_o[pi, pair, q_rows, :] = _unstack_heads(o, DIL_Q)
                    acc_l[pi, pair, q_rows, :] = _unstack_heads(lse, DIL_Q)
            return carry

        lax.fori_loop(0, d * nb // ATT_UNROLL, block, 0)

    ch = 256
    for rho in range(DIL_REGROUP):
        for c in range(quarter // ch):
            g_rows = slice(rho * quarter + c * ch, rho * quarter + (c + 1) * ch)
            n_rows = pl.ds(rho + DIL_REGROUP * c * ch, ch, stride=DIL_REGROUP)
            for pair in range(2):
                l0, l1, l2 = acc_l[0, pair, n_rows, :], acc_l[1, pair, g_rows, :], acc_l[2, pair, g_rows, :]
                top = jnp.maximum(jnp.maximum(l0, l1), l2)
                w0, w1, w2 = jnp.exp2(l0 - top), jnp.exp2(l1 - top), jnp.exp2(l2 - top)
                acc_o[0, pair, n_rows, :] = (
                    (w0 * acc_o[0, pair, n_rows, :] + w1 * acc_o[1, pair, g_rows, :]
                     + w2 * acc_o[2, pair, g_rows, :]) / (w0 + w1 + w2))
    for c in range(s // ch):
        r = slice(c * ch, (c + 1) * ch)
        for pair in range(2):
            lo = pair * 2 * HEAD_DIM
            gate = vg_ref[r, GROUP_W + lo:GROUP_W + lo + 2 * HEAD_DIM].astype(F32)
            o_ref[r, lo:lo + 2 * HEAD_DIM] = (acc_o[0, pair, r, :] * _silu(gate)).astype(BF16)


def _dilated(z3, cos_t, sin_t, mask_t):
    b = z3.shape[0]
    return pl.pallas_call(
        _dilated_kernel,
        grid=(b,),
        in_specs=[
            pl.BlockSpec((None, SEQ, 2 * GROUP_W), lambda i: (i, 0, 1)),
            pl.BlockSpec((None, SEQ, 2 * GROUP_W), lambda i: (i, 0, 2)),
            pl.BlockSpec((SEQ, 2 * HEAD_DIM), lambda i: (0, 0)),
            pl.BlockSpec((SEQ, 2 * HEAD_DIM), lambda i: (0, 0)),
            pl.BlockSpec(mask_t.shape, lambda i: (0, 0, 0)),
        ],
        out_specs=pl.BlockSpec((None, SEQ, GROUP_W), lambda i: (i, 0, 0)),
        out_shape=jax.ShapeDtypeStruct((b, SEQ, GROUP_W), BF16),
        scratch_shapes=[pltpu.VMEM((2, SEQ, 2 * HEAD_DIM), F32) for _ in range(6)]
        + [pltpu.VMEM((len(DILATED_PATTERNS), 2, SEQ, 2 * HEAD_DIM), F32) for _ in range(2)]
        + [pltpu.VMEM((2 * ATT_UNROLL, 2 * DIL_Q, 2 * DIL_Q), F32),
           pltpu.VMEM((2 * ATT_UNROLL, 2 * DIL_Q, 2 * DIL_Q), BF16),
           pltpu.VMEM((2 * ATT_UNROLL, 2 * DIL_Q, 2 * HEAD_DIM), F32)],
        compiler_params=_params(),
        name="dilated_attention",
    )(z3, z3, cos_t, sin_t, mask_t)


def _na_bias_table(rpb):
    c = np.arange(GRID_W)
    cs = np.clip(c - NA_COLS // 2, 0, GRID_W - NA_COLS)
    col_ok = (c[None, :] >= cs[:, None]) & (c[None, :] < cs[:, None] + NA_COLS)
    pad = GRID_W - NA_COLS
    tab = jnp.pad(rpb.astype(F32), ((0, 0), (0, 0), (pad, pad)))
    tab = _toeplitz(tab, 2, GRID_W, GRID_W)
    tab = jnp.where(col_ok[None, None, :, :], tab * LOG2E, NEG)
    h, nd = rpb.shape[0], 2 * NA_ROWS - 1
    tab = tab.reshape(h // 2, 2, nd, GRID_W, GRID_W).transpose(0, 2, 1, 3, 4)
    tab = tab.reshape(h // 2, nd, 2 * GRID_W, GRID_W)
    return jnp.concatenate([tab[:, :-1], tab[:, 1:]], axis=-1)


def _softmax_strips(s_scr, p_scr, bias, n_rows, n_keys, m_scr=None):
    for r0 in range(0, n_rows, STRIP):
        t = s_scr[r0:r0 + STRIP, 0:n_keys] + bias(r0, r0 + STRIP)
        m = jnp.max(t, axis=1, keepdims=True)
        p_scr[r0:r0 + STRIP, 0:n_keys] = jnp.exp2(t - m).astype(BF16)
        if m_scr is not None:
            m_scr[r0:r0 + STRIP, :] = jnp.broadcast_to(m, (STRIP, 2 * HEAD_DIM))


def _with_ones(v):
    return jnp.concatenate([v, jnp.ones_like(v)], axis=1)


def _na_kernel(qk_ref, vg_ref, bias_ref, o_ref, s_scr, p_scr, bias_scr):
    n_rows = SEQ // GRID_W
    nk = NA_ROWS * GRID_W
    hd2 = 2 * HEAD_DIM

    @pl.when(pl.program_id(0) == 0)
    def _():
        for pair in range(2):
            for off in range(NA_ROWS):
                for j in range(NA_ROWS // 2):
                    bias_scr[pair, off, :, 2 * j * GRID_W:(2 * j + 2) * GRID_W] = bias_ref[
                        pair, 2 * j - off + NA_ROWS - 1]

    def rows_trip(trip, carry):
        for sub in range(ATT_UNROLL):
            r = trip * ATT_UNROLL + sub
            rs = jnp.clip(r - NA_ROWS // 2, 0, n_rows - NA_ROWS)
            off = r - rs
            q_rows = pl.ds(pl.multiple_of(r * GRID_W, GRID_W), GRID_W)
            k_rows = pl.ds(pl.multiple_of(rs * GRID_W, GRID_W), nk)
            for pair in range(2):
                slot = 2 * sub + pair
                lo = pair * hd2
                q = qk_ref[q_rows, lo:lo + hd2].astype(F32) * (HEAD_DIM ** -0.5 * LOG2E)
                q = _stack_heads(q.astype(BF16))
                k = qk_ref[k_rows, GROUP_W + lo:GROUP_W + lo + hd2]
                v = _with_ones(vg_ref[k_rows, lo:lo + hd2])
                s_scr[slot] = lax.dot_general(q, k, (((1,), (1,)), ((), ())), preferred_element_type=F32)
                _softmax_strips(s_scr.at[slot], p_scr.at[slot],
                                lambda r0, r1, pair=pair, off=off: bias_scr[pair, off, r0:r1, :],
                                2 * GRID_W, nk)
                oa = jnp.dot(p_scr[slot], v, preferred_element_type=F32)
                o = oa[:, 0:hd2] / oa[:, hd2:2 * hd2]
                gate = vg_ref[q_rows, GROUP_W + lo:GROUP_W + lo + hd2].astype(F32)
                o_ref[q_rows, lo:lo + hd2] = (_unstack_heads(o, GRID_W) * _silu(gate)).astype(BF16)
        return carry

    lax.fori_loop(0, n_rows // ATT_UNROLL, rows_trip, 0)


def _neighbourhood(z3, bias_tab, l):
    b = z3.shape[0]
    return pl.pallas_call(
        _na_kernel,
        grid=(b,),
        in_specs=[
            pl.BlockSpec((None, SEQ, 2 * GROUP_W), lambda i: (i, 0, 3)),
            pl.BlockSpec((None, SEQ, 2 * GROUP_W), lambda i: (i, 0, 4)),
            pl.BlockSpec((None,) + bias_tab.shape[1:], lambda i: (l, 0, 0, 0, 0)),
        ],
        out_specs=pl.BlockSpec((None, SEQ, GROUP_W), lambda i: (i, 0, 0)),
        out_shape=jax.ShapeDtypeStruct((b, SEQ, GROUP_W), BF16),
        scratch_shapes=[pltpu.VMEM((2 * ATT_UNROLL, 2 * GRID_W, NA_ROWS * GRID_W), F32),
                        pltpu.VMEM((2 * ATT_UNROLL, 2 * GRID_W, NA_ROWS * GRID_W), BF16),
                        pltpu.VMEM((2, NA_ROWS, 2 * GRID_W, NA_ROWS * GRID_W), F32)],
        compiler_params=_params(0, 1),
        name="neighbourhood_attention",
    )(z3, z3, bias_tab)


def _ssm_weights(a_re, a_im, log_dt, b_re, b_im, c_re, c_im):
    L, G, P, C = SSM_L, SSM_GROUPS, SSM_STATE, SSM_CH
    a_re, a_im, log_dt = a_re.astype(F32), a_im.astype(F32), log_dt.astype(F32)
    dt = jnp.exp(log_dt)[..., None]
    ks = jnp.arange(L + 1, dtype=F32)
    mag = jnp.exp((a_re * dt)[..., None] * ks)
    ang = (a_im * dt)[..., None] * ks
    pr, pim = mag * jnp.cos(ang), mag * jnp.sin(ang)
    abr, abi = pr[..., 1], pim[..., 1]
    den = a_re * a_re + a_im * a_im
    gr = ((abr - 1.0) * a_re + abi * a_im) / den
    gi = (abi * a_re - (abr - 1.0) * a_im) / den
    br, bi = b_re.astype(F32), b_im.astype(F32)
    bbr = gr[..., None] * br - gi[..., None] * bi
    bbi = gr[..., None] * bi + gi[..., None] * br
    cr, ci = c_re.astype(F32), c_im.astype(F32)
    cb_r = cr[:, :, None] * bbr.transpose(0, 1, 3, 2)[:, :, :, None] \
        - ci[:, :, None] * bbi.transpose(0, 1, 3, 2)[:, :, :, None]
    cb_i = cr[:, :, None] * bbi.transpose(0, 1, 3, 2)[:, :, :, None] \
        + ci[:, :, None] * bbr.transpose(0, 1, 3, 2)[:, :, :, None]
    kern = (jnp.einsum('dgxcp,dgpk->dgxck', cb_r, pr, precision=HI)
            - jnp.einsum('dgxcp,dgpk->dgxck', cb_i, pim, precision=HI))
    kf, kb = kern[0], kern[1]
    k_lag = jnp.concatenate([kb[..., 1:L][..., ::-1], (kf[..., 0] + kb[..., 0])[..., None],
                             kf[..., 1:L], jnp.zeros(kf.shape[:-1] + (1,), F32)], axis=-1)
    k_lag = k_lag.reshape(G, C * C, 2 * L)

    def state_in(d, idx):
        a_r = pr[d][:, :, idx].transpose(0, 2, 1)[:, None]
        a_i = pim[d][:, :, idx].transpose(0, 2, 1)[:, None]
        b_r = bbr[d].transpose(0, 2, 1)[:, :, None]
        b_i = bbi[d].transpose(0, 2, 1)[:, :, None]
        return a_r * b_r - a_i * b_i, a_r * b_i + a_i * b_r

    sf_r, sf_i = state_in(0, slice(L - 1, None, -1))
    sb_r, sb_i = state_in(1, slice(0, L))
    w_s = jnp.concatenate([sf_r, sb_r, sf_i, sb_i], axis=-1).reshape(G, C * L, 4 * P)

    def state_out(d, idx):
        c_r = cr[d].transpose(0, 2, 1)[..., None]
        c_i = ci[d].transpose(0, 2, 1)[..., None]
        a_r = pr[d][:, :, idx][:, :, None, :]
        a_i = pim[d][:, :, idx][:, :, None, :]
        return c_r * a_r - c_i * a_i, c_r * a_i + c_i * a_r

    ff_r, ff_i = state_out(0, slice(1, L + 1))
    fb_r, fb_i = state_out(1, slice(L, 0, -1))
    w_c = jnp.concatenate([ff_r, fb_r, -ff_i, -fb_i], axis=1).reshape(G, 4 * P, C * L)
    a_l = jnp.concatenate([pr[0, ..., L], pr[1, ..., L], pim[0, ..., L], pim[1, ..., L]], axis=-1)
    a_l = jnp.broadcast_to(a_l[:, None, :], (G, SSM_NB, 4 * P))
    return k_lag, w_s.astype(BF16), w_c.astype(BF16), a_l


def _ssm_toeplitz_kernel(k_ref, t_ref):
    L, C = SSM_L, SSM_CH

    def build(cp, carry):
        for c in range(C):
            lag = jnp.broadcast_to(k_ref[pl.ds(cp * C + c, 1), :], (L, 2 * L))
            tile = pltpu.roll(lag, L + 1, 1, stride=1, stride_axis=0)[:, 0:L]
            t_ref[pl.ds(pl.multiple_of(cp * L, L), L), c * L:(c + 1) * L] = tile.astype(BF16)
        return carry

    lax.fori_loop(0, C, build, 0)


def _ssm_toeplitz(k_lag):
    n, C, L = k_lag.shape[0], SSM_CH, SSM_L
    return pl.pallas_call(
        _ssm_toeplitz_kernel,
        grid=(n,),
        in_specs=[pl.BlockSpec((None, C * C, 2 * L), lambda g: (g, 0, 0))],
        out_specs=pl.BlockSpec((None, C * L, C * L), lambda g: (g, 0, 0)),
        out_shape=jax.ShapeDtypeStruct((n, C * L, C * L), BF16),
        compiler_params=_params(),
        name="ssm_toeplitz",
    )(k_lag)


def _ssm_kernel(*refs, nb, first_step):
    n = len(first_step) - 1
    u_refs, (t_ref, ws_ref, wc_ref, al_ref), y_refs = refs[:n], refs[n:n + 4], refs[n + 4:2 * n + 4]
    x_scr, y_scr, s_re, s_im, ha_re, ha_im, hb_re, hb_im = refs[2 * n + 4:]
    L, C, P = SSM_L, SSM_CH, SSM_STATE
    nc = SEQ // L
    m = nb * nc
    half = 2 * P
    step = pl.program_id(1)
    u_all = u_refs[0][...]
    for k in range(1, n):
        u_all = jnp.where(step >= first_step[k], u_refs[k][...], u_all)

    for b in range(nb):
        ub = u_all[b].astype(F32)
        for j in range(nc):
            r0 = (b * nc + j) * SSM_PITCH
            x_scr[r0:r0 + C, :] = ub[:, j * L:(j + 1) * L]
    lhs = jnp.concatenate([x_scr[pl.ds(c, m, stride=SSM_PITCH), :].astype(BF16) for c in range(C)],
                          axis=1)

    s = jnp.dot(lhs, ws_ref[...], preferred_element_type=F32)
    s_re[...] = s[:, 0:half]
    s_im[...] = s[:, half:2 * half]
    a_r = al_ref[0:nb, 0:half]
    a_i = al_ref[0:nb, half:2 * half]
    fwd = lax.broadcasted_iota(jnp.int32, (nb, half), 1) < P
    h_r = jnp.zeros((nb, half), F32)
    h_i = jnp.zeros((nb, half), F32)
    for k in range(nc):
        rf = pl.ds(k, nb, stride=nc)
        rb = pl.ds(nc - 1 - k, nb, stride=nc)
        ha_re[rf, :] = h_r
        ha_im[rf, :] = h_i
        hb_re[rb, :] = h_r
        hb_im[rb, :] = h_i
        s_r = jnp.where(fwd, s_re[rf, :], s_re[rb, :])
        s_i = jnp.where(fwd, s_im[rf, :], s_im[rb, :])
        h_r, h_i = a_r * h_r - a_i * h_i + s_r, a_r * h_i + a_i * h_r + s_i
    fwd_m = lax.broadcasted_iota(jnp.int32, (m, half), 1) < P
    h_all = jnp.concatenate([jnp.where(fwd_m, ha_re[...], hb_re[...]),
                             jnp.where(fwd_m, ha_im[...], hb_im[...])], axis=1).astype(BF16)
    y = jnp.dot(lhs, t_ref[...], preferred_element_type=F32)
    y = y + jnp.dot(h_all, wc_ref[...], preferred_element_type=F32)
    for c in range(C):
        y_scr[pl.ds(c, m, stride=SSM_PITCH), :] = y[:, c * L:(c + 1) * L]
    for k in range(n):
        @pl.when((step >= first_step[k]) & (step < first_step[k + 1]))
        def _(k=k):
            for b in range(nb):
                y_refs[k][b] = jnp.concatenate(
                    [y_scr[(b * nc + j) * SSM_PITCH:(b * nc + j) * SSM_PITCH + C, :] for j in range(nc)],
                    axis=1)


def _ssm_core(zts, w_t, w_s, w_c, a_l, l):
    L, G, C, P = SSM_L, SSM_GROUPS, SSM_CH, SSM_STATE
    nb = SSM_NB
    m = nb * (SEQ // L)
    steps = [zt.shape[0] // nb for zt in zts]
    first = [sum(steps[:k]) for k in range(len(zts) + 1)]

    def batch_spec(k):
        return pl.BlockSpec((nb, C, SEQ),
                            lambda g, i: (jnp.clip(i - first[k], 0, steps[k] - 1), g, 0))

    weight = lambda r, c: pl.BlockSpec((None, r, c), lambda g, i: (l * G + g, 0, 0))
    return pl.pallas_call(
        functools.partial(_ssm_kernel, nb=nb, first_step=tuple(first)),
        grid=(G, first[-1]),
        in_specs=[batch_spec(k) for k in range(len(zts))]
        + [weight(C * L, C * L), weight(C * L, 4 * P), weight(4 * P, C * L), weight(nb, 4 * P)],
        out_specs=[batch_spec(k) for k in range(len(zts))],
        out_shape=[jax.ShapeDtypeStruct((zt.shape[0], G * C, SEQ), F32) for zt in zts],
        scratch_shapes=[pltpu.VMEM((m * SSM_PITCH, L), F32), pltpu.VMEM((m * SSM_PITCH, L), F32)]
        + [pltpu.VMEM((m, 2 * P), F32) for _ in range(6)],
        compiler_params=_params(0, 2),
        name="ssm_core",
    )(*zts, w_t, w_s, w_c, a_l)


def _block_diag(w):
    n, k = w.shape[0], w.shape[1]
    out = jnp.zeros((n * k, n * k), w.dtype)
    for i in range(n):
        out = lax.dynamic_update_slice(out, w[i], (i * k, i * k))
    return out


def kernel(x_prompt, x_sample, norm_g, w_in, w_out, pool_w, pool_scale, ssm_a_re, ssm_a_im,
           ssm_log_dt, ssm_b_re, ssm_b_im, ssm_c_re, ssm_c_im, ssm_d, glu_w, glu_b, na_rpb, final_g):
    inv_cnt = _pool_inv_count()
    cos_t, sin_t = _rope_tables()
    mask_t = _dilated_mask_table()
    gw = GROUP_W
    norm_all = norm_g.reshape(DEPTH, 1, D_MODEL).astype(F32)
    w_abd = jnp.concatenate([w_out[:, :2 * gw], w_out[:, 3 * gw:]], axis=1).astype(BF16)
    w_c = w_out[:, 2 * gw:3 * gw].astype(BF16)
    pool_bd = jax.vmap(_block_diag)(pool_w).astype(BF16)
    pool_sc = pool_scale.reshape(DEPTH, 1, gw).astype(F32)
    k_lag, ssm_ws, ssm_wc, ssm_al = jax.vmap(_ssm_weights)(
        ssm_a_re, ssm_a_im, ssm_log_dt, ssm_b_re, ssm_b_im, ssm_c_re, ssm_c_im)
    merge = lambda a: a.reshape((DEPTH * SSM_GROUPS,) + a.shape[2:])
    ssm = (_ssm_toeplitz(merge(k_lag)), merge(ssm_ws), merge(ssm_wc), merge(ssm_al))
    d_all = ssm_d.reshape(DEPTH, gw, 1).astype(F32)
    glu_wt = jnp.swapaxes(glu_w, 1, 2).astype(BF16)
    glu_b_all = glu_b.reshape(DEPTH, gw, 1).astype(F32)
    na_bias = jax.vmap(_na_bias_table)(na_rpb)
    fg = final_g.reshape(1, D_MODEL).astype(F32)
    nxt = (norm_all, w_in.astype(BF16))

    xs = [x_prompt, x_sample]
    bs = [x.shape[0] for x in xs]
    x2s = [x.reshape(b * SEQ, D_MODEL) for x, b in zip(xs, bs)]
    opened = [_proj(x2, nxt=nxt, nxt_layer=0) for x2 in x2s]
    for l in range(DEPTH):
        y_cts = _ssm_core([zt for _, zt in opened], *ssm, l)
        for t, b in enumerate(bs):
            z2, zt = opened[t]
            z3 = z2.reshape(b, SEQ, TOK_W)
            y_a = _pool(z3, inv_cnt, pool_bd, pool_sc, l)
            y_b = _dilated(z3, cos_t, sin_t, mask_t)
            y_d = _neighbourhood(z3, na_bias, l)
            flat = lambda a: a.reshape(b * SEQ, gw)
            mix = (flat(y_a), flat(y_b), flat(y_d), y_cts[t], zt, d_all, glu_wt, glu_b_all, w_abd, w_c, fg)
            if l + 1 < DEPTH:
                x2s[t], z2n, ztn = _proj(x2s[t], mix=mix, mix_layer=l, nxt=nxt, nxt_layer=l + 1)
                opened[t] = (z2n, ztn)
            else:
                (x2s[t],) = _proj(x2s[t], mix=mix, mix_layer=l, final=True)
    return tuple(x2.reshape(b, SEQ, D_MODEL) for x2, b in zip(x2s, bs))
```

```python
import functools
import math

import numpy as np
import jax
import jax.numpy as jnp
from jax import lax
from jax.experimental import pallas as pl
from jax.experimental.pallas import tpu as pltpu

D_MODEL = 1024
SEQ = 2048
DEPTH = 4
GROUP_W = 256
HEAD_DIM = 64
POOL_WINDOWS = (2, 4, 8, 16)
POOL_GROUP = 64
DILATED_PATTERNS = ((128, 1), (512, 4), (2048, 16))
SSM_CH = 16
SSM_GROUPS = 16
SSM_STATE = 64
GRID_W = 64
NA_ROWS = 8
NA_COLS = 16
ROPE_THETA = 10000.0
EPS = 1e-6
PROJ_W = 12 * GROUP_W
NEG = -1e30

F32 = jnp.float32
BF16 = jnp.bfloat16
HI = lax.Precision.HIGHEST

ROW_TILE = 512
POOL_PAD = 16
DIL_Q = 128
DIL_R = 64
DIL_REGROUP = 4
STRIP = 32
ATT_UNROLL = 8
NA_UNROLL = 16
LOG2E = math.log2(math.e)
SSM_L = 128
SSM_NB_MAX = 16
SSM_PITCH = 24
TOK_W = 10 * GROUP_W
VMEM_LIMIT = 56 * 1024 * 1024


def _sigmoid(x):
    return 1.0 / (1.0 + jnp.exp(-x))


def _silu(x):
    return x * _sigmoid(x)


def _gelu_tanh(x):
    return 0.5 * x * (1.0 + jnp.tanh(math.sqrt(2.0 / math.pi) * (x + 0.044715 * (x * x * x))))


def _toeplitz(k, axis, a, b):
    k = jnp.moveaxis(k, axis, -1)
    w = a + b - 1
    lead = k.shape[:-1]
    kp = jnp.concatenate([k, jnp.zeros(lead + (1,), k.dtype)], axis=-1)
    t = jnp.broadcast_to(kp[..., None, :], lead + (a, w + 1)).reshape(lead + (a * (w + 1),))
    t = t[..., :a * w].reshape(lead + (a, w))[..., a - 1:a - 1 + b]
    return jnp.moveaxis(t, (-2, -1), (axis, axis + 1))


def _params(n_parallel=1, n_arbitrary=0):
    sem = ("parallel",) * n_parallel + ("arbitrary",) * n_arbitrary
    return pltpu.CompilerParams(dimension_semantics=sem, vmem_limit_bytes=VMEM_LIMIT)


def _proj_kernel(*refs, mix_out, proj_in, final):
    refs = list(refs)
    x_ref = refs.pop(0)
    if mix_out:
        (ya_ref, yb_ref, yd_ref, yct_ref, ztp_ref, d_ref, gwt_ref, gb_ref,
         wabd_ref, wc_ref, fg_ref) = refs[:11]
        del refs[:11]
    if proj_in:
        g_ref, w_ref = refs[:2]
        del refs[:2]
    if mix_out:
        xo_ref = refs.pop(0)
    if proj_in:
        z_ref, zt_ref = refs

    x = x_ref[...]
    if mix_out:
        u = ztp_ref[0:GROUP_W, :].astype(F32)
        y = u * d_ref[...] + yct_ref[...]
        g = _gelu_tanh(y)
        lin = jnp.dot(gwt_ref[...], g.astype(BF16), preferred_element_type=F32) + gb_ref[...]
        g = g * _sigmoid(lin)
        y_c = (g * _silu(ztp_ref[GROUP_W:2 * GROUP_W, :].astype(F32))).astype(BF16)
        y_abd = jnp.concatenate([ya_ref[...], yb_ref[...], yd_ref[...]], axis=1)
        acc = jnp.dot(y_abd, wabd_ref[...], preferred_element_type=F32)
        acc = acc + lax.dot_general(y_c, wc_ref[...], (((0,), (0,)), ((), ())),
                                    preferred_element_type=F32)
        x = x + acc
        if final:
            ms = jnp.mean(x * x, axis=-1, keepdims=True)
            x = x * lax.rsqrt(ms + EPS) * fg_ref[...]
        xo_ref[...] = x
    if proj_in:
        ms = jnp.mean(x * x, axis=-1, keepdims=True)
        h = (x * lax.rsqrt(ms + EPS) * g_ref[...]).astype(BF16)
        nch = 2 * GROUP_W
        for n, src in enumerate((0, 1, 2, 4, 5)):
            z_ref[:, n * nch:(n + 1) * nch] = jnp.dot(
                h, w_ref[:, src * nch:(src + 1) * nch], preferred_element_type=F32).astype(BF16)
        zt_ref[...] = lax.dot_general(w_ref[:, 3 * nch:4 * nch], h, (((0,), (1,)), ((), ())),
                                      preferred_element_type=F32).astype(BF16)


def _proj(x2, mix=None, mix_layer=None, nxt=None, nxt_layer=None, final=False):
    rows = x2.shape[0]
    per_seq = SEQ // ROW_TILE
    tile = lambda w: pl.BlockSpec((ROW_TILE, w), lambda i: (i, 0))
    chan = lambda c: pl.BlockSpec((None, c, ROW_TILE), lambda i: (i // per_seq, 0, i % per_seq))
    layer = lambda l, r, c: pl.BlockSpec((None, r, c), lambda i: (l, 0, 0))
    args, in_specs, out_specs, out_shape = [x2], [tile(D_MODEL)], [], []
    if mix is not None:
        args += list(mix)
        ml = mix_layer
        in_specs += [tile(GROUP_W), tile(GROUP_W), tile(GROUP_W), chan(GROUP_W), chan(2 * GROUP_W),
                     layer(ml, GROUP_W, 1), layer(ml, GROUP_W, GROUP_W), layer(ml, GROUP_W, 1),
                     layer(ml, 3 * GROUP_W, D_MODEL), layer(ml, GROUP_W, D_MODEL),
                     pl.BlockSpec((1, D_MODEL), lambda i: (0, 0))]
        out_specs.append(tile(D_MODEL))
        out_shape.append(jax.ShapeDtypeStruct((rows, D_MODEL), F32))
    if nxt is not None:
        args += list(nxt)
        nl = nxt_layer
        in_specs += [layer(nl, 1, D_MODEL), layer(nl, D_MODEL, PROJ_W)]
        out_specs += [tile(TOK_W), chan(2 * GROUP_W)]
        out_shape += [jax.ShapeDtypeStruct((rows, TOK_W), BF16),
                      jax.ShapeDtypeStruct((rows // SEQ, 2 * GROUP_W, SEQ), BF16)]
    return pl.pallas_call(
        functools.partial(_proj_kernel, mix_out=mix is not None, proj_in=nxt is not None, final=final),
        grid=(rows // ROW_TILE,),
        in_specs=in_specs, out_specs=out_specs, out_shape=out_shape,
        compiler_params=_params(),
        name="proj",
    )(*args)


def _pool_kernel(z_ref, inv_ref, w_ref, scale_ref, o_ref, ub, s2b, s4b, s8b):
    s = SEQ
    p = POOL_PAD
    n = s + p
    u = z_ref[:, 0:GROUP_W].astype(F32)
    zero_p = jnp.zeros((p, GROUP_W), F32)
    zero_8 = jnp.zeros((8, GROUP_W), F32)
    ub[0:p, :] = zero_p
    ub[p:p + s, :] = u
    ub[p + s:p + s + p, :] = zero_p
    half = GROUP_W // 2
    for buf in (s2b, s4b, s8b):
        buf[0:8, :] = zero_8[:, 0:buf.shape[1]]
        buf[s + 24:s + 32, :] = zero_8[:, 0:buf.shape[1]]
    s2b[8:8 + n, :] = ub[7:7 + n, :] + ub[8:8 + n, :]
    s4b[8:8 + n, :] = s2b[7:7 + n, :] + s2b[9:9 + n, :]
    s8b[8:8 + n, :] = s4b[6:6 + n, half:] + s4b[10:10 + n, half:]
    s16 = s8b[p - 4:p - 4 + s, :] + s8b[p + 4:p + 4 + s, :]
    lane = lax.broadcasted_iota(jnp.int32, (s, half), 1)
    win = jnp.concatenate([
        jnp.where(lane < POOL_GROUP, s2b[p:p + s, 0:half], s4b[p:p + s, 0:half]),
        jnp.where(lane < POOL_GROUP, s8b[p:p + s, :], s16)], axis=1)
    diff = (win * inv_ref[...] - u).astype(BF16)
    y = jnp.dot(diff, w_ref[...], preferred_element_type=F32) * scale_ref[...]
    gate = z_ref[:, GROUP_W:2 * GROUP_W].astype(F32)
    o_ref[...] = (y * _silu(gate)).astype(BF16)


def _pool_inv_count():
    t = np.arange(SEQ)
    cols = []
    for w in POOL_WINDOWS:
        lo = np.clip(t - w // 2, 0, SEQ)
        hi = np.clip(t + w // 2, 0, SEQ)
        cols.append(np.repeat((1.0 / (hi - lo))[:, None], POOL_GROUP, axis=1))
    return jnp.asarray(np.concatenate(cols, axis=1), F32)


def _pool(z3, inv_cnt, w_bd, scale, l):
    b = z3.shape[0]
    return pl.pallas_call(
        _pool_kernel,
        grid=(b,),
        in_specs=[
            pl.BlockSpec((None, SEQ, 2 * GROUP_W), lambda i: (i, 0, 0)),
            pl.BlockSpec((SEQ, GROUP_W), lambda i: (0, 0)),
            pl.BlockSpec((None, GROUP_W, GROUP_W), lambda i: (l, 0, 0)),
            pl.BlockSpec((None, 1, GROUP_W), lambda i: (l, 0, 0)),
        ],
        out_specs=pl.BlockSpec((None, SEQ, GROUP_W), lambda i: (i, 0, 0)),
        out_shape=jax.ShapeDtypeStruct((b, SEQ, GROUP_W), BF16),
        scratch_shapes=[pltpu.VMEM((SEQ + 2 * POOL_PAD, GROUP_W), F32) for _ in range(3)]
        + [pltpu.VMEM((SEQ + 2 * POOL_PAD, GROUP_W // 2), F32)],
        compiler_params=_params(),
        name="pool_mixer",
    )(z3, inv_cnt, w_bd, scale)


def _rope_tables():
    inv = ROPE_THETA ** (-np.arange(0, HEAD_DIM, 2, dtype=np.float64) / HEAD_DIM)
    ang = np.arange(SEQ, dtype=np.float64)[:, None] * inv[None, :]
    cos = np.concatenate([np.cos(ang), np.cos(ang)], axis=1)
    sin = np.concatenate([-np.sin(ang), np.sin(ang)], axis=1)
    return (jnp.asarray(np.tile(cos, (1, 2)), F32), jnp.asarray(np.tile(sin, (1, 2)), F32))


def _stack_heads(q):
    lane = lax.broadcasted_iota(jnp.int32, q.shape, 1)
    zero = jnp.zeros_like(q)
    return jnp.concatenate([jnp.where(lane < HEAD_DIM, q, zero),
                            jnp.where(lane < HEAD_DIM, zero, q)], axis=0)


def _unstack_heads(o, n):
    lane = lax.broadcasted_iota(jnp.int32, (n, 2 * HEAD_DIM), 1)
    return jnp.where(lane < HEAD_DIM, o[0:n], o[n:2 * n])


def _dilated_mask_table():
    q = np.arange(DIL_Q)[None, :, None]
    k = np.arange(2 * DIL_Q)[None, None, :]
    shift = (np.arange(3) * DIL_R)[:, None, None]
    return jnp.asarray(np.where(np.abs(k - q - shift) <= DIL_R, 0.0, NEG), F32)


def _dilated_kernel(qk_ref, vg_ref, cos_ref, sin_ref, mask_ref, o_ref, qf, kf, vf, q4, k4, v4,
                    acc_o, acc_l, s_scr, p_scr, m_scr):
    s = SEQ
    hd2 = 2 * HEAD_DIM
    lane2 = lax.broadcasted_iota(jnp.int32, (s, 2 * HEAD_DIM), 1)
    first_half = (lane2 % HEAD_DIM) < (HEAD_DIM // 2)
    cos = cos_ref[...]
    sin = sin_ref[...]
    for pair in range(2):
        lo = pair * 2 * HEAD_DIM
        for src, dst, scale in ((0, qf, HEAD_DIM ** -0.5 * LOG2E), (GROUP_W, kf, 1.0)):
            x = qk_ref[:, src + lo:src + lo + 2 * HEAD_DIM].astype(F32)
            swapped = jnp.where(first_half,
                                pltpu.roll(x, 2 * HEAD_DIM - HEAD_DIM // 2, 1),
                                pltpu.roll(x, HEAD_DIM // 2, 1))
            roped = x * cos + swapped * sin
            dst[pair] = roped if scale == 1.0 else roped * scale
        vf[pair] = vg_ref[:, lo:lo + 2 * HEAD_DIM].astype(F32)

    quarter = s // DIL_REGROUP
    for pair in range(2):
        for nat, grp in ((qf, q4), (kf, k4), (vf, v4)):
            for rho in range(DIL_REGROUP):
                grp[pair, rho * quarter:(rho + 1) * quarter, :] = nat[
                    pair, pl.ds(rho, quarter, stride=DIL_REGROUP), :]

    for pi, (window, d) in enumerate(DILATED_PATTERNS):
        n_sub = s // d
        kw = min(2 * DIL_Q, n_sub)
        nb = n_sub // DIL_Q
        q_src, k_src, v_src = (qf, kf, vf) if d == 1 else (q4, k4, v4)

        def rows(rho, start, size, d=d):
            if d == 1:
                return pl.ds(pl.multiple_of(start, DIL_R), size)
            dd = d // DIL_REGROUP
            base = (rho % DIL_REGROUP) * quarter + rho // DIL_REGROUP + dd * start
            if dd == 1:
                return pl.ds(pl.multiple_of(base, DIL_R), size)
            return pl.ds(base, size, stride=dd)

        def block(trip, carry, d=d, n_sub=n_sub, kw=kw, nb=nb, pi=pi, rows=rows,
                  q_src=q_src, k_src=k_src, v_src=v_src):
            for sub in range(ATT_UNROLL):
                it = trip * ATT_UNROLL + sub
                rho = it // nb
                q0 = (it % nb) * DIL_Q
                ks = jnp.clip(q0 - DIL_R, 0, n_sub - kw)
                case = (q0 - ks) // DIL_R
                q_rows = rows(rho, q0, DIL_Q)
                k_rows = rows(rho, ks, kw)
                for pair in range(2):
                    slot = 2 * sub + pair
                    q = _stack_heads(q_src[pair, q_rows, :].astype(BF16))
                    k = k_src[pair, k_rows, :].astype(BF16)
                    v = _with_ones(v_src[pair, k_rows, :].astype(BF16))
                    s_scr[slot, :, 0:kw] = lax.dot_general(q, k, (((1,), (1,)), ((), ())),
                                                           preferred_element_type=F32)
                    _softmax_strips(
                        s_scr.at[slot], p_scr.at[slot],
                        lambda r0, r1, kw=kw, case=case: mask_ref[
                            case, r0 % DIL_Q:r0 % DIL_Q + (r1 - r0), 0:kw],
                        2 * DIL_Q, kw, m_scr.at[slot])
                    oa = jnp.dot(p_scr[slot, :, 0:kw], v, preferred_element_type=F32)
                    den = oa[:, hd2:2 * hd2]
                    o = oa[:, 0:hd2] / den
                    lse = m_scr[slot] + jnp.log2(den)
                    acc_o[pi, pair, q_rows, :] = _unstack_heads(o, DIL_Q)
                    acc_l[pi, pair, q_rows, :] = _unstack_heads(lse, DIL_Q)
            return carry

        lax.fori_loop(0, d * nb // ATT_UNROLL, block, 0)

    ch = 256
    for rho in range(DIL_REGROUP):
        for c in range(quarter // ch):
            g_rows = slice(rho * quarter + c * ch, rho * quarter + (c + 1) * ch)
            n_rows = pl.ds(rho + DIL_REGROUP * c * ch, ch, stride=DIL_REGROUP)
            for pair in range(2):
                l0, l1, l2 = acc_l[0, pair, n_rows, :], acc_l[1, pair, g_rows, :], acc_l[2, pair, g_rows, :]
                top = jnp.maximum(jnp.maximum(l0, l1), l2)
                w0, w1, w2 = jnp.exp2(l0 - top), jnp.exp2(l1 - top), jnp.exp2(l2 - top)
                acc_o[0, pair, n_rows, :] = (
                    (w0 * acc_o[0, pair, n_rows, :] + w1 * acc_o[1, pair, g_rows, :]
                     + w2 * acc_o[2, pair, g_rows, :]) / (w0 + w1 + w2))
    for c in range(s // ch):
        r = slice(c * ch, (c + 1) * ch)
        for pair in range(2):
            lo = pair * 2 * HEAD_DIM
            gate = vg_ref[r, GROUP_W + lo:GROUP_W + lo + 2 * HEAD_DIM].astype(F32)
            o_ref[r, lo:lo + 2 * HEAD_DIM] = (acc_o[0, pair, r, :] * _silu(gate)).astype(BF16)


def _dilated(z3, cos_t, sin_t, mask_t):
    b = z3.shape[0]
    return pl.pallas_call(
        _dilated_kernel,
        grid=(b,),
        in_specs=[
            pl.BlockSpec((None, SEQ, 2 * GROUP_W), lambda i: (i, 0, 1)),
            pl.BlockSpec((None, SEQ, 2 * GROUP_W), lambda i: (i, 0, 2)),
            pl.BlockSpec((SEQ, 2 * HEAD_DIM), lambda i: (0, 0)),
            pl.BlockSpec((SEQ, 2 * HEAD_DIM), lambda i: (0, 0)),
            pl.BlockSpec(mask_t.shape, lambda i: (0, 0, 0)),
        ],
        out_specs=pl.BlockSpec((None, SEQ, GROUP_W), lambda i: (i, 0, 0)),
        out_shape=jax.ShapeDtypeStruct((b, SEQ, GROUP_W), BF16),
        scratch_shapes=[pltpu.VMEM((2, SEQ, 2 * HEAD_DIM), F32) for _ in range(6)]
        + [pltpu.VMEM((len(DILATED_PATTERNS), 2, SEQ, 2 * HEAD_DIM), F32) for _ in range(2)]
        + [pltpu.VMEM((2 * ATT_UNROLL, 2 * DIL_Q, 2 * DIL_Q), F32),
           pltpu.VMEM((2 * ATT_UNROLL, 2 * DIL_Q, 2 * DIL_Q), BF16),
           pltpu.VMEM((2 * ATT_UNROLL, 2 * DIL_Q, 2 * HEAD_DIM), F32)],
        compiler_params=_params(),
        name="dilated_attention",
    )(z3, z3, cos_t, sin_t, mask_t)


def _na_bias_table(rpb):
    c = np.arange(GRID_W)
    cs = np.clip(c - NA_COLS // 2, 0, GRID_W - NA_COLS)
    col_ok = (c[None, :] >= cs[:, None]) & (c[None, :] < cs[:, None] + NA_COLS)
    pad = GRID_W - NA_COLS
    tab = jnp.pad(rpb.astype(F32), ((0, 0), (0, 0), (pad, pad)))
    tab = _toeplitz(tab, 2, GRID_W, GRID_W)
    tab = jnp.where(col_ok[None, None, :, :], tab * LOG2E, NEG)
    h, nd = rpb.shape[0], 2 * NA_ROWS - 1
    tab = tab.reshape(h // 2, 2, nd, GRID_W, GRID_W).transpose(0, 2, 1, 3, 4)
    tab = tab.reshape(h // 2, nd, 2 * GRID_W, GRID_W)
    return jnp.concatenate([tab[:, :-1], tab[:, 1:]], axis=-1)


def _softmax_strips(s_scr, p_scr, bias, n_rows, n_keys, m_scr=None):
    for r0 in range(0, n_rows, STRIP):
        t = s_scr[r0:r0 + STRIP, 0:n_keys] + bias(r0, r0 + STRIP)
        m = jnp.max(t, axis=1, keepdims=True)
        p_scr[r0:r0 + STRIP, 0:n_keys] = jnp.exp2(t - m).astype(BF16)
        if m_scr is not None:
            m_scr[r0:r0 + STRIP, :] = jnp.broadcast_to(m, (STRIP, 2 * HEAD_DIM))


def _with_ones(v):
    return jnp.concatenate([v, jnp.ones_like(v)], axis=1)


def _na_kernel(qk_ref, vg_ref, bias_ref, o_ref, s_scr, p_scr, bias_scr):
    n_rows = SEQ // GRID_W
    nk = NA_ROWS * GRID_W
    hd2 = 2 * HEAD_DIM

    @pl.when(pl.program_id(0) == 0)
    def _():
        for pair in range(2):
            for off in range(NA_ROWS):
                for j in range(NA_ROWS // 2):
                    bias_scr[pair, off, :, 2 * j * GRID_W:(2 * j + 2) * GRID_W] = bias_ref[
                        pair, 2 * j - off + NA_ROWS - 1]

    def rows_trip(trip, carry):
        for sub in range(NA_UNROLL):
            r = trip * NA_UNROLL + sub
            rs = jnp.clip(r - NA_ROWS // 2, 0, n_rows - NA_ROWS)
            off = r - rs
            q_rows = pl.ds(pl.multiple_of(r * GRID_W, GRID_W), GRID_W)
            k_rows = pl.ds(pl.multiple_of(rs * GRID_W, GRID_W), nk)
            for pair in range(2):
                slot = 2 * sub + pair
                lo = pair * hd2
                q = qk_ref[q_rows, lo:lo + hd2].astype(F32) * (HEAD_DIM ** -0.5 * LOG2E)
                q = _stack_heads(q.astype(BF16))
                k = qk_ref[k_rows, GROUP_W + lo:GROUP_W + lo + hd2]
                v = _with_ones(vg_ref[k_rows, lo:lo + hd2])
                s_scr[slot] = lax.dot_general(q, k, (((1,), (1,)), ((), ())), preferred_element_type=F32)
                _softmax_strips(s_scr.at[slot], p_scr.at[slot],
                                lambda r0, r1, pair=pair, off=off: bias_scr[pair, off, r0:r1, :],
                                2 * GRID_W, nk)
                oa = jnp.dot(p_scr[slot], v, preferred_element_type=F32)
                o = oa[:, 0:hd2] / oa[:, hd2:2 * hd2]
                gate = vg_ref[q_rows, GROUP_W + lo:GROUP_W + lo + hd2].astype(F32)
                o_ref[q_rows, lo:lo + hd2] = (_unstack_heads(o, GRID_W) * _silu(gate)).astype(BF16)
        return carry

    lax.fori_loop(0, n_rows // NA_UNROLL, rows_trip, 0)


def _neighbourhood(z3, bias_tab, l):
    b = z3.shape[0]
    return pl.pallas_call(
        _na_kernel,
        grid=(b,),
        in_specs=[
            pl.BlockSpec((None, SEQ, 2 * GROUP_W), lambda i: (i, 0, 3)),
            pl.BlockSpec((None, SEQ, 2 * GROUP_W), lambda i: (i, 0, 4)),
            pl.BlockSpec((None,) + bias_tab.shape[1:], lambda i: (l, 0, 0, 0, 0)),
        ],
        out_specs=pl.BlockSpec((None, SEQ, GROUP_W), lambda i: (i, 0, 0)),
        out_shape=jax.ShapeDtypeStruct((b, SEQ, GROUP_W), BF16),
        scratch_shapes=[pltpu.VMEM((2 * NA_UNROLL, 2 * GRID_W, NA_ROWS * GRID_W), F32),
                        pltpu.VMEM((2 * NA_UNROLL, 2 * GRID_W, NA_ROWS * GRID_W), BF16),
                        pltpu.VMEM((2, NA_ROWS, 2 * GRID_W, NA_ROWS * GRID_W), F32)],
        compiler_params=_params(0, 1),
        name="neighbourhood_attention",
    )(z3, z3, bias_tab)


def _ssm_weights(a_re, a_im, log_dt, b_re, b_im, c_re, c_im):
    L, G, P, C = SSM_L, SSM_GROUPS, SSM_STATE, SSM_CH
    a_re, a_im, log_dt = a_re.astype(F32), a_im.astype(F32), log_dt.astype(F32)
    dt = jnp.exp(log_dt)[..., None]
    ks = jnp.arange(L + 1, dtype=F32)
    mag = jnp.exp((a_re * dt)[..., None] * ks)
    ang = (a_im * dt)[..., None] * ks
    pr, pim = mag * jnp.cos(ang), mag * jnp.sin(ang)
    abr, abi = pr[..., 1], pim[..., 1]
    den = a_re * a_re + a_im * a_im
    gr = ((abr - 1.0) * a_re + abi * a_im) / den
    gi = (abi * a_re - (abr - 1.0) * a_im) / den
    br, bi = b_re.astype(F32), b_im.astype(F32)
    bbr = gr[..., None] * br - gi[..., None] * bi
    bbi = gr[..., None] * bi + gi[..., None] * br
    cr, ci = c_re.astype(F32), c_im.astype(F32)
    cb_r = cr[:, :, None] * bbr.transpose(0, 1, 3, 2)[:, :, :, None] \
        - ci[:, :, None] * bbi.transpose(0, 1, 3, 2)[:, :, :, None]
    cb_i = cr[:, :, None] * bbi.transpose(0, 1, 3, 2)[:, :, :, None] \
        + ci[:, :, None] * bbr.transpose(0, 1, 3, 2)[:, :, :, None]
    kern = (jnp.einsum('dgxcp,dgpk->dgxck', cb_r, pr, precision=HI)
            - jnp.einsum('dgxcp,dgpk->dgxck', cb_i, pim, precision=HI))
    kf, kb = kern[0], kern[1]
    k_lag = jnp.concatenate([kb[..., 1:L][..., ::-1], (kf[..., 0] + kb[..., 0])[..., None],
                             kf[..., 1:L], jnp.zeros(kf.shape[:-1] + (1,), F32)], axis=-1)
    k_lag = k_lag.reshape(G, C * C, 2 * L)

    def state_in(d, idx):
        a_r = pr[d][:, :, idx].transpose(0, 2, 1)[:, None]
        a_i = pim[d][:, :, idx].transpose(0, 2, 1)[:, None]
        b_r = bbr[d].transpose(0, 2, 1)[:, :, None]
        b_i = bbi[d].transpose(0, 2, 1)[:, :, None]
        return a_r * b_r - a_i * b_i, a_r * b_i + a_i * b_r

    sf_r, sf_i = state_in(0, slice(L - 1, None, -1))
    sb_r, sb_i = state_in(1, slice(0, L))
    w_s = jnp.concatenate([sf_r, sb_r, sf_i, sb_i], axis=-1).reshape(G, C * L, 4 * P)

    def state_out(d, idx):
        c_r = cr[d].transpose(0, 2, 1)[..., None]
        c_i = ci[d].transpose(0, 2, 1)[..., None]
        a_r = pr[d][:, :, idx][:, :, None, :]
        a_i = pim[d][:, :, idx][:, :, None, :]
        return c_r * a_r - c_i * a_i, c_r * a_i + c_i * a_r

    ff_r, ff_i = state_out(0, slice(1, L + 1))
    fb_r, fb_i = state_out(1, slice(L, 0, -1))
    w_c = jnp.concatenate([ff_r, fb_r, -ff_i, -fb_i], axis=1).reshape(G, 4 * P, C * L)
    a_l = jnp.concatenate([pr[0, ..., L], pr[1, ..., L], pim[0, ..., L], pim[1, ..., L]], axis=-1)
    a_l = jnp.broadcast_to(a_l[:, None, :], (G, SSM_NB_MAX, 4 * P))
    return k_lag, w_s.astype(BF16), w_c.astype(BF16), a_l


def _ssm_toeplitz_kernel(k_ref, t_ref):
    L, C = SSM_L, SSM_CH

    def build(cp, carry):
        for c in range(C):
            lag = jnp.broadcast_to(k_ref[pl.ds(cp * C + c, 1), :], (L, 2 * L))
            tile = pltpu.roll(lag, L + 1, 1, stride=1, stride_axis=0)[:, 0:L]
            t_ref[pl.ds(pl.multiple_of(cp * L, L), L), c * L:(c + 1) * L] = tile.astype(BF16)
        return carry

    lax.fori_loop(0, C, build, 0)


def _ssm_toeplitz(k_lag):
    n, C, L = k_lag.shape[0], SSM_CH, SSM_L
    return pl.pallas_call(
        _ssm_toeplitz_kernel,
        grid=(n,),
        in_specs=[pl.BlockSpec((None, C * C, 2 * L), lambda g: (g, 0, 0))],
        out_specs=pl.BlockSpec((None, C * L, C * L), lambda g: (g, 0, 0)),
        out_shape=jax.ShapeDtypeStruct((n, C * L, C * L), BF16),
        compiler_params=_params(),
        name="ssm_toeplitz",
    )(k_lag)


def _ssm_kernel(u_ref, t_ref, ws_ref, wc_ref, al_ref, y_ref, x_scr, y_scr,
                s_re, s_im, ha_re, ha_im, hb_re, hb_im, *, nb):
    L, C, P = SSM_L, SSM_CH, SSM_STATE
    nc = SEQ // L
    m = nb * nc
    half = 2 * P

    for b in range(nb):
        ub = u_ref[b].astype(F32)
        for j in range(nc):
            r0 = (b * nc + j) * SSM_PITCH
            x_scr[r0:r0 + C, :] = ub[:, j * L:(j + 1) * L]
    lhs = jnp.concatenate([x_scr[pl.ds(c, m, stride=SSM_PITCH), :].astype(BF16) for c in range(C)],
                          axis=1)

    s = jnp.dot(lhs, ws_ref[...], preferred_element_type=F32)
    s_re[...] = s[:, 0:half]
    s_im[...] = s[:, half:2 * half]
    a_r = al_ref[0:nb, 0:half]
    a_i = al_ref[0:nb, half:2 * half]
    fwd = lax.broadcasted_iota(jnp.int32, (nb, half), 1) < P
    h_r = jnp.zeros((nb, half), F32)
    h_i = jnp.zeros((nb, half), F32)
    for k in range(nc):
        rf = pl.ds(k, nb, stride=nc)
        rb = pl.ds(nc - 1 - k, nb, stride=nc)
        ha_re[rf, :] = h_r
        ha_im[rf, :] = h_i
        hb_re[rb, :] = h_r
        hb_im[rb, :] = h_i
        s_r = jnp.where(fwd, s_re[rf, :], s_re[rb, :])
        s_i = jnp.where(fwd, s_im[rf, :], s_im[rb, :])
        h_r, h_i = a_r * h_r - a_i * h_i + s_r, a_r * h_i + a_i * h_r + s_i
    fwd_m = lax.broadcasted_iota(jnp.int32, (m, half), 1) < P
    h_all = jnp.concatenate([jnp.where(fwd_m, ha_re[...], hb_re[...]),
                             jnp.where(fwd_m, ha_im[...], hb_im[...])], axis=1).astype(BF16)
    y = jnp.dot(lhs, t_ref[...], preferred_element_type=F32)
    y = y + jnp.dot(h_all, wc_ref[...], preferred_element_type=F32)
    for c in range(C):
        y_scr[pl.ds(c, m, stride=SSM_PITCH), :] = y[:, c * L:(c + 1) * L]
    for b in range(nb):
        y_ref[b] = jnp.concatenate(
            [y_scr[(b * nc + j) * SSM_PITCH:(b * nc + j) * SSM_PITCH + C, :] for j in range(nc)], axis=1)


def _ssm_core(zt, w_t, w_s, w_c, a_l, l):
    b = zt.shape[0]
    L, G, C, P = SSM_L, SSM_GROUPS, SSM_CH, SSM_STATE
    nb = min(b, SSM_NB_MAX)
    m = nb * (SEQ // L)
    weight = lambda r, c: pl.BlockSpec((None, r, c), lambda g, i: (l * G + g, 0, 0))
    return pl.pallas_call(
        functools.partial(_ssm_kernel, nb=nb),
        grid=(G, b // nb),
        in_specs=[pl.BlockSpec((nb, C, SEQ), lambda g, i: (i, g, 0)),
                  weight(C * L, C * L), weight(C * L, 4 * P), weight(4 * P, C * L),
                  weight(SSM_NB_MAX, 4 * P)],
        out_specs=pl.BlockSpec((nb, C, SEQ), lambda g, i: (i, g, 0)),
        out_shape=jax.ShapeDtypeStruct((b, G * C, SEQ), F32),
        scratch_shapes=[pltpu.VMEM((m * SSM_PITCH, L), F32), pltpu.VMEM((m * SSM_PITCH, L), F32)]
        + [pltpu.VMEM((m, 2 * P), F32) for _ in range(6)],
        compiler_params=_params(2),
        name="ssm_core",
    )(zt, w_t, w_s, w_c, a_l)


def _block_diag(w):
    n, k = w.shape[0], w.shape[1]
    out = jnp.zeros((n * k, n * k), w.dtype)
    for i in range(n):
        out = lax.dynamic_update_slice(out, w[i], (i * k, i * k))
    return out


def kernel(x_prompt, x_sample, norm_g, w_in, w_out, pool_w, pool_scale, ssm_a_re, ssm_a_im,
           ssm_log_dt, ssm_b_re, ssm_b_im, ssm_c_re, ssm_c_im, ssm_d, glu_w, glu_b, na_rpb, final_g):
    inv_cnt = _pool_inv_count()
    cos_t, sin_t = _rope_tables()
    mask_t = _dilated_mask_table()
    gw = GROUP_W
    norm_all = norm_g.reshape(DEPTH, 1, D_MODEL).astype(F32)
    w_abd = jnp.concatenate([w_out[:, :2 * gw], w_out[:, 3 * gw:]], axis=1).astype(BF16)
    w_c = w_out[:, 2 * gw:3 * gw].astype(BF16)
    pool_bd = jax.vmap(_block_diag)(pool_w).astype(BF16)
    pool_sc = pool_scale.reshape(DEPTH, 1, gw).astype(F32)
    k_lag, ssm_ws, ssm_wc, ssm_al = jax.vmap(_ssm_weights)(
        ssm_a_re, ssm_a_im, ssm_log_dt, ssm_b_re, ssm_b_im, ssm_c_re, ssm_c_im)
    merge = lambda a: a.reshape((DEPTH * SSM_GROUPS,) + a.shape[2:])
    ssm = (_ssm_toeplitz(merge(k_lag)), merge(ssm_ws), merge(ssm_wc), merge(ssm_al))
    d_all = ssm_d.reshape(DEPTH, gw, 1).astype(F32)
    glu_wt = jnp.swapaxes(glu_w, 1, 2).astype(BF16)
    glu_b_all = glu_b.reshape(DEPTH, gw, 1).astype(F32)
    na_bias = jax.vmap(_na_bias_table)(na_rpb)
    fg = final_g.reshape(1, D_MODEL).astype(F32)
    nxt = (norm_all, w_in.astype(BF16))

    def trunk(x):
        b = x.shape[0]
        x2 = x.reshape(b * SEQ, D_MODEL)
        z2, zt = _proj(x2, nxt=nxt, nxt_layer=0)
        for l in range(DEPTH):
            z3 = z2.reshape(b, SEQ, TOK_W)
            y_a = _pool(z3, inv_cnt, pool_bd, pool_sc, l)
            y_b = _dilated(z3, cos_t, sin_t, mask_t)
            y_ct = _ssm_core(zt, *ssm, l)
            y_d = _neighbourhood(z3, na_bias, l)
            flat = lambda a: a.reshape(b * SEQ, gw)
            mix = (flat(y_a), flat(y_b), flat(y_d), y_ct, zt, d_all, glu_wt, glu_b_all, w_abd, w_c, fg)
            if l + 1 < DEPTH:
                x2, z2, zt = _proj(x2, mix=mix, mix_layer=l, nxt=nxt, nxt_layer=l + 1)
            else:
                (x2,) = _proj(x2, mix=mix, mix_layer=l, final=True)
        return x2.reshape(b, SEQ, D_MODEL)

    return (trunk(x_prompt), trunk(x_sample))
```

```python
import functools
import math

import numpy as np
import jax
import jax.numpy as jnp
from jax import lax
from jax.experimental import pallas as pl
from jax.experimental.pallas import tpu as pltpu

D_MODEL = 1024
SEQ = 2048
DEPTH = 4
GROUP_W = 256
HEAD_DIM = 64
POOL_WINDOWS = (2, 4, 8, 16)
POOL_GROUP = 64
DILATED_PATTERNS = ((128, 1), (512, 4), (2048, 16))
SSM_CH = 16
SSM_GROUPS = 16
SSM_STATE = 64
GRID_W = 64
NA_ROWS = 8
NA_COLS = 16
ROPE_THETA = 10000.0
EPS = 1e-6
PROJ_W = 12 * GROUP_W
NEG = -1e30

F32 = jnp.float32
BF16 = jnp.bfloat16
HI = lax.Precision.HIGHEST

ROW_TILE = 512
POOL_PAD = 16
DIL_Q = 128
DIL_R = 64
DIL_BLOCKS = SEQ // DIL_Q
DIL_REGROUP = 4
STRIP = 32
ATT_UNROLL = 8
NA_UNROLL = 16
LOG2E = math.log2(math.e)
SSM_L = 128
SSM_NB_MAX = 16
SSM_PITCH = 24
TOK_W = 10 * GROUP_W
VMEM_LIMIT = 56 * 1024 * 1024


def _sigmoid(x):
    return 1.0 / (1.0 + jnp.exp(-x))


def _silu(x):
    return x * _sigmoid(x)


def _gelu_tanh(x):
    return 0.5 * x * (1.0 + jnp.tanh(math.sqrt(2.0 / math.pi) * (x + 0.044715 * (x * x * x))))


def _toeplitz(k, axis, a, b):
    k = jnp.moveaxis(k, axis, -1)
    w = a + b - 1
    lead = k.shape[:-1]
    kp = jnp.concatenate([k, jnp.zeros(lead + (1,), k.dtype)], axis=-1)
    t = jnp.broadcast_to(kp[..., None, :], lead + (a, w + 1)).reshape(lead + (a * (w + 1),))
    t = t[..., :a * w].reshape(lead + (a, w))[..., a - 1:a - 1 + b]
    return jnp.moveaxis(t, (-2, -1), (axis, axis + 1))


def _params(n_parallel=1, n_arbitrary=0):
    sem = ("parallel",) * n_parallel + ("arbitrary",) * n_arbitrary
    return pltpu.CompilerParams(dimension_semantics=sem, vmem_limit_bytes=VMEM_LIMIT)


def _proj_kernel(*refs, mix_out, proj_in, final):
    refs = list(refs)
    x_ref = refs.pop(0)
    if mix_out:
        (ya_ref, yb_ref, yd_ref, yct_ref, ztp_ref, d_ref, gwt_ref, gb_ref,
         wabd_ref, wc_ref, fg_ref) = refs[:11]
        del refs[:11]
    if proj_in:
        g_ref, w_ref = refs[:2]
        del refs[:2]
    if mix_out:
        xo_ref = refs.pop(0)
    if proj_in:
        z_ref, zt_ref = refs

    x = x_ref[...]
    if mix_out:
        u = ztp_ref[0:GROUP_W, :].astype(F32)
        y = u * d_ref[...] + yct_ref[...]
        g = _gelu_tanh(y)
        lin = jnp.dot(gwt_ref[...], g.astype(BF16), preferred_element_type=F32) + gb_ref[...]
        g = g * _sigmoid(lin)
        y_c = (g * _silu(ztp_ref[GROUP_W:2 * GROUP_W, :].astype(F32))).astype(BF16)
        y_abd = jnp.concatenate([ya_ref[...], yb_ref[...], yd_ref[...]], axis=1)
        acc = jnp.dot(y_abd, wabd_ref[...], preferred_element_type=F32)
        acc = acc + lax.dot_general(y_c, wc_ref[...], (((0,), (0,)), ((), ())),
                                    preferred_element_type=F32)
        x = x + acc
        if final:
            ms = jnp.mean(x * x, axis=-1, keepdims=True)
            x = x * lax.rsqrt(ms + EPS) * fg_ref[...]
        xo_ref[...] = x
    if proj_in:
        ms = jnp.mean(x * x, axis=-1, keepdims=True)
        h = (x * lax.rsqrt(ms + EPS) * g_ref[...]).astype(BF16)
        nch = 2 * GROUP_W
        for n, src in enumerate((0, 1, 2, 4, 5)):
            z_ref[:, n * nch:(n + 1) * nch] = jnp.dot(
                h, w_ref[:, src * nch:(src + 1) * nch], preferred_element_type=F32).astype(BF16)
        zt_ref[...] = lax.dot_general(w_ref[:, 3 * nch:4 * nch], h, (((0,), (1,)), ((), ())),
                                      preferred_element_type=F32).astype(BF16)


def _proj(x2, mix=None, mix_layer=None, nxt=None, nxt_layer=None, final=False):
    rows = x2.shape[0]
    per_seq = SEQ // ROW_TILE
    tile = lambda w: pl.BlockSpec((ROW_TILE, w), lambda i: (i, 0))
    chan = lambda c: pl.BlockSpec((None, c, ROW_TILE), lambda i: (i // per_seq, 0, i % per_seq))
    layer = lambda l, r, c: pl.BlockSpec((None, r, c), lambda i: (l, 0, 0))
    args, in_specs, out_specs, out_shape = [x2], [tile(D_MODEL)], [], []
    if mix is not None:
        args += list(mix)
        ml = mix_layer
        in_specs += [tile(GROUP_W), tile(GROUP_W), tile(GROUP_W), chan(GROUP_W), chan(2 * GROUP_W),
                     layer(ml, GROUP_W, 1), layer(ml, GROUP_W, GROUP_W), layer(ml, GROUP_W, 1),
                     layer(ml, 3 * GROUP_W, D_MODEL), layer(ml, GROUP_W, D_MODEL),
                     pl.BlockSpec((1, D_MODEL), lambda i: (0, 0))]
        out_specs.append(tile(D_MODEL))
        out_shape.append(jax.ShapeDtypeStruct((rows, D_MODEL), F32))
    if nxt is not None:
        args += list(nxt)
        nl = nxt_layer
        in_specs += [layer(nl, 1, D_MODEL), layer(nl, D_MODEL, PROJ_W)]
        out_specs += [tile(TOK_W), chan(2 * GROUP_W)]
        out_shape += [jax.ShapeDtypeStruct((rows, TOK_W), BF16),
                      jax.ShapeDtypeStruct((rows // SEQ, 2 * GROUP_W, SEQ), BF16)]
    return pl.pallas_call(
        functools.partial(_proj_kernel, mix_out=mix is not None, proj_in=nxt is not None, final=final),
        grid=(rows // ROW_TILE,),
        in_specs=in_specs, out_specs=out_specs, out_shape=out_shape,
        compiler_params=_params(),
        name="proj",
    )(*args)


def _pool_kernel(z_ref, inv_ref, w_ref, scale_ref, o_ref, ub, s2b, s4b, s8b):
    s = SEQ
    p = POOL_PAD
    n = s + p
    u = z_ref[:, 0:GROUP_W].astype(F32)
    zero_p = jnp.zeros((p, GROUP_W), F32)
    zero_8 = jnp.zeros((8, GROUP_W), F32)
    ub[0:p, :] = zero_p
    ub[p:p + s, :] = u
    ub[p + s:p + s + p, :] = zero_p
    half = GROUP_W // 2
    for buf in (s2b, s4b, s8b):
        buf[0:8, :] = zero_8[:, 0:buf.shape[1]]
        buf[s + 24:s + 32, :] = zero_8[:, 0:buf.shape[1]]
    s2b[8:8 + n, :] = ub[7:7 + n, :] + ub[8:8 + n, :]
    s4b[8:8 + n, :] = s2b[7:7 + n, :] + s2b[9:9 + n, :]
    s8b[8:8 + n, :] = s4b[6:6 + n, half:] + s4b[10:10 + n, half:]
    s16 = s8b[p - 4:p - 4 + s, :] + s8b[p + 4:p + 4 + s, :]
    lane = lax.broadcasted_iota(jnp.int32, (s, half), 1)
    win = jnp.concatenate([
        jnp.where(lane < POOL_GROUP, s2b[p:p + s, 0:half], s4b[p:p + s, 0:half]),
        jnp.where(lane < POOL_GROUP, s8b[p:p + s, :], s16)], axis=1)
    diff = (win * inv_ref[...] - u).astype(BF16)
    y = jnp.dot(diff, w_ref[...], preferred_element_type=F32) * scale_ref[...]
    gate = z_ref[:, GROUP_W:2 * GROUP_W].astype(F32)
    o_ref[...] = (y * _silu(gate)).astype(BF16)


def _pool_inv_count():
    t = np.arange(SEQ)
    cols = []
    for w in POOL_WINDOWS:
        lo = np.clip(t - w // 2, 0, SEQ)
        hi = np.clip(t + w // 2, 0, SEQ)
        cols.append(np.repeat((1.0 / (hi - lo))[:, None], POOL_GROUP, axis=1))
    return jnp.asarray(np.concatenate(cols, axis=1), F32)


def _pool(z3, inv_cnt, w_bd, scale, l):
    b = z3.shape[0]
    return pl.pallas_call(
        _pool_kernel,
        grid=(b,),
        in_specs=[
            pl.BlockSpec((None, SEQ, 2 * GROUP_W), lambda i: (i, 0, 0)),
            pl.BlockSpec((SEQ, GROUP_W), lambda i: (0, 0)),
            pl.BlockSpec((None, GROUP_W, GROUP_W), lambda i: (l, 0, 0)),
            pl.BlockSpec((None, 1, GROUP_W), lambda i: (l, 0, 0)),
        ],
        out_specs=pl.BlockSpec((None, SEQ, GROUP_W), lambda i: (i, 0, 0)),
        out_shape=jax.ShapeDtypeStruct((b, SEQ, GROUP_W), BF16),
        scratch_shapes=[pltpu.VMEM((SEQ + 2 * POOL_PAD, GROUP_W), F32) for _ in range(3)]
        + [pltpu.VMEM((SEQ + 2 * POOL_PAD, GROUP_W // 2), F32)],
        compiler_params=_params(),
        name="pool_mixer",
    )(z3, inv_cnt, w_bd, scale)


def _rope_tables():
    inv = ROPE_THETA ** (-np.arange(0, HEAD_DIM, 2, dtype=np.float64) / HEAD_DIM)
    ang = np.arange(SEQ, dtype=np.float64)[:, None] * inv[None, :]
    cos = np.concatenate([np.cos(ang), np.cos(ang)], axis=1)
    sin = np.concatenate([-np.sin(ang), np.sin(ang)], axis=1)
    return (jnp.asarray(np.tile(cos, (1, 2)), F32), jnp.asarray(np.tile(sin, (1, 2)), F32))


def _stack_heads(q):
    lane = lax.broadcasted_iota(jnp.int32, q.shape, 1)
    zero = jnp.zeros_like(q)
    return jnp.concatenate([jnp.where(lane < HEAD_DIM, q, zero),
                            jnp.where(lane < HEAD_DIM, zero, q)], axis=0)


def _unstack_heads(o, n):
    lane = lax.broadcasted_iota(jnp.int32, (n, 2 * HEAD_DIM), 1)
    return jnp.where(lane < HEAD_DIM, o[0:n], o[n:2 * n])


def _dilated_mask_table():
    q = np.arange(DIL_Q)[None, :, None]
    k = np.arange(2 * DIL_Q)[None, None, :]
    shift = (np.arange(3) * DIL_R)[:, None, None]
    return jnp.asarray(np.where(np.abs(k - q - shift) <= DIL_R, 0.0, NEG), F32)


def _dilated_kernel(qk_ref, vg_ref, cos_ref, sin_ref, mask_ref, o_ref, qf, kf, vf, q4, k4, v4,
                    acc_o, acc_l, s_scr, p_scr, m_scr):
    s = SEQ
    hd2 = 2 * HEAD_DIM
    lane2 = lax.broadcasted_iota(jnp.int32, (s, 2 * HEAD_DIM), 1)
    first_half = (lane2 % HEAD_DIM) < (HEAD_DIM // 2)
    cos = cos_ref[...]
    sin = sin_ref[...]
    for pair in range(2):
        lo = pair * 2 * HEAD_DIM
        for src, dst, scale in ((0, qf, HEAD_DIM ** -0.5 * LOG2E), (GROUP_W, kf, 1.0)):
            x = qk_ref[:, src + lo:src + lo + 2 * HEAD_DIM].astype(F32)
            swapped = jnp.where(first_half,
                                pltpu.roll(x, 2 * HEAD_DIM - HEAD_DIM // 2, 1),
                                pltpu.roll(x, HEAD_DIM // 2, 1))
            roped = x * cos + swapped * sin
            dst[pair] = roped if scale == 1.0 else roped * scale
        vf[pair] = vg_ref[:, lo:lo + 2 * HEAD_DIM].astype(F32)

    quarter = s // DIL_REGROUP
    for pair in range(2):
        for nat, grp in ((qf, q4), (kf, k4), (vf, v4)):
            for rho in range(DIL_REGROUP):
                grp[pair, rho * quarter:(rho + 1) * quarter, :] = nat[
                    pair, pl.ds(rho, quarter, stride=DIL_REGROUP), :]

    pattern_blocks = []
    for pi, (window, d) in enumerate(DILATED_PATTERNS):
        n_sub = s // d
        kw = min(2 * DIL_Q, n_sub)
        nb = n_sub // DIL_Q
        q_src, k_src, v_src = (qf, kf, vf) if d == 1 else (q4, k4, v4)

        def rows(rho, start, size, d=d):
            if d == 1:
                return pl.ds(pl.multiple_of(start, DIL_R), size)
            dd = d // DIL_REGROUP
            base = (rho % DIL_REGROUP) * quarter + rho // DIL_REGROUP + dd * start
            if dd == 1:
                return pl.ds(pl.multiple_of(base, DIL_R), size)
            return pl.ds(base, size, stride=dd)

        def block(trip, carry, d=d, n_sub=n_sub, kw=kw, nb=nb, pi=pi, rows=rows,
                  q_src=q_src, k_src=k_src, v_src=v_src):
            for sub in range(ATT_UNROLL):
                it = trip * ATT_UNROLL + sub
                rho = it // nb
                q0 = (it % nb) * DIL_Q
                ks = jnp.clip(q0 - DIL_R, 0, n_sub - kw)
                case = (q0 - ks) // DIL_R
                q_rows = rows(rho, q0, DIL_Q)
                k_rows = rows(rho, ks, kw)
                for pair in range(2):
                    slot = 2 * sub + pair
                    q = _stack_heads(q_src[pair, q_rows, :].astype(BF16))
                    k = k_src[pair, k_rows, :].astype(BF16)
                    v = _with_ones(v_src[pair, k_rows, :].astype(BF16))
                    s_scr[slot, :, 0:kw] = lax.dot_general(q, k, (((1,), (1,)), ((), ())),
                                                           preferred_element_type=F32)
                    _softmax_strips(
                        s_scr.at[slot], p_scr.at[slot],
                        lambda r0, r1, kw=kw, case=case: mask_ref[
                            case, r0 % DIL_Q:r0 % DIL_Q + (r1 - r0), 0:kw],
                        2 * DIL_Q, kw, m_scr.at[slot])
                    oa = jnp.dot(p_scr[slot, :, 0:kw], v, preferred_element_type=F32)
                    den = oa[:, hd2:2 * hd2]
                    o = oa[:, 0:hd2] / den
                    lse = m_scr[slot] + jnp.log2(den)
                    acc_o[pi, pair, q_rows, :] = _unstack_heads(o, DIL_Q)
                    acc_l[pi, pair, q_rows, :] = _unstack_heads(lse, DIL_Q)
            return carry

        assert d * nb == DIL_BLOCKS
        pattern_blocks.append(block)

    def trip_body(trip, carry):
        for blocks_of_pattern in pattern_blocks:
            blocks_of_pattern(trip, carry)
        return carry

    lax.fori_loop(0, DIL_BLOCKS // ATT_UNROLL, trip_body, 0)

    ch = 256
    for rho in range(DIL_REGROUP):
        for c in range(quarter // ch):
            g_rows = slice(rho * quarter + c * ch, rho * quarter + (c + 1) * ch)
            n_rows = pl.ds(rho + DIL_REGROUP * c * ch, ch, stride=DIL_REGROUP)
            for pair in range(2):
                l0, l1, l2 = acc_l[0, pair, n_rows, :], acc_l[1, pair, g_rows, :], acc_l[2, pair, g_rows, :]
                top = jnp.maximum(jnp.maximum(l0, l1), l2)
                w0, w1, w2 = jnp.exp2(l0 - top), jnp.exp2(l1 - top), jnp.exp2(l2 - top)
                acc_o[0, pair, n_rows, :] = (
                    (w0 * acc_o[0, pair, n_rows, :] + w1 * acc_o[1, pair, g_rows, :]
                     + w2 * acc_o[2, pair, g_rows, :]) / (w0 + w1 + w2))
    for c in range(s // ch):
        r = slice(c * ch, (c + 1) * ch)
        for pair in range(2):
            lo = pair * 2 * HEAD_DIM
            gate = vg_ref[r, GROUP_W + lo:GROUP_W + lo + 2 * HEAD_DIM].astype(F32)
            o_ref[r, lo:lo + 2 * HEAD_DIM] = (acc_o[0, pair, r, :] * _silu(gate)).astype(BF16)


def _dilated(z3, cos_t, sin_t, mask_t):
    b = z3.shape[0]
    return pl.pallas_call(
        _dilated_kernel,
        grid=(b,),
        in_specs=[
            pl.BlockSpec((None, SEQ, 2 * GROUP_W), lambda i: (i, 0, 1)),
            pl.BlockSpec((None, SEQ, 2 * GROUP_W), lambda i: (i, 0, 2)),
            pl.BlockSpec((SEQ, 2 * HEAD_DIM), lambda i: (0, 0)),
            pl.BlockSpec((SEQ, 2 * HEAD_DIM), lambda i: (0, 0)),
            pl.BlockSpec(mask_t.shape, lambda i: (0, 0, 0)),
        ],
        out_specs=pl.BlockSpec((None, SEQ, GROUP_W), lambda i: (i, 0, 0)),
        out_shape=jax.ShapeDtypeStruct((b, SEQ, GROUP_W), BF16),
        scratch_shapes=[pltpu.VMEM((2, SEQ, 2 * HEAD_DIM), F32) for _ in range(6)]
        + [pltpu.VMEM((len(DILATED_PATTERNS), 2, SEQ, 2 * HEAD_DIM), F32) for _ in range(2)]
        + [pltpu.VMEM((2 * ATT_UNROLL, 2 * DIL_Q, 2 * DIL_Q), F32),
           pltpu.VMEM((2 * ATT_UNROLL, 2 * DIL_Q, 2 * DIL_Q), BF16),
           pltpu.VMEM((2 * ATT_UNROLL, 2 * DIL_Q, 2 * HEAD_DIM), F32)],
        compiler_params=_params(),
        name="dilated_attention",
    )(z3, z3, cos_t, sin_t, mask_t)


def _na_bias_table(rpb):
    c = np.arange(GRID_W)
    cs = np.clip(c - NA_COLS // 2, 0, GRID_W - NA_COLS)
    col_ok = (c[None, :] >= cs[:, None]) & (c[None, :] < cs[:, None] + NA_COLS)
    pad = GRID_W - NA_COLS
    tab = jnp.pad(rpb.astype(F32), ((0, 0), (0, 0), (pad, pad)))
    tab = _toeplitz(tab, 2, GRID_W, GRID_W)
    tab = jnp.where(col_ok[None, None, :, :], tab * LOG2E, NEG)
    h, nd = rpb.shape[0], 2 * NA_ROWS - 1
    tab = tab.reshape(h // 2, 2, nd, GRID_W, GRID_W).transpose(0, 2, 1, 3, 4)
    tab = tab.reshape(h // 2, nd, 2 * GRID_W, GRID_W)
    return jnp.concatenate([tab[:, :-1], tab[:, 1:]], axis=-1)


def _softmax_strips(s_scr, p_scr, bias, n_rows, n_keys, m_scr=None):
    for r0 in range(0, n_rows, STRIP):
        t = s_scr[r0:r0 + STRIP, 0:n_keys] + bias(r0, r0 + STRIP)
        m = jnp.max(t, axis=1, keepdims=True)
        p_scr[r0:r0 + STRIP, 0:n_keys] = jnp.exp2(t - m).astype(BF16)
        if m_scr is not None:
            m_scr[r0:r0 + STRIP, :] = jnp.broadcast_to(m, (STRIP, 2 * HEAD_DIM))


def _with_ones(v):
    return jnp.concatenate([v, jnp.ones_like(v)], axis=1)


def _na_kernel(qk_ref, vg_ref, bias_ref, o_ref, s_scr, p_scr, bias_scr):
    n_rows = SEQ // GRID_W
    nk = NA_ROWS * GRID_W
    hd2 = 2 * HEAD_DIM

    @pl.when(pl.program_id(0) == 0)
    def _():
        for pair in range(2):
            for off in range(NA_ROWS):
                for j in range(NA_ROWS // 2):
                    bias_scr[pair, off, :, 2 * j * GRID_W:(2 * j + 2) * GRID_W] = bias_ref[
                        pair, 2 * j - off + NA_ROWS - 1]

    def rows_trip(trip, carry):
        for sub in range(NA_UNROLL):
            r = trip * NA_UNROLL + sub
            rs = jnp.clip(r - NA_ROWS // 2, 0, n_rows - NA_ROWS)
            off = r - rs
            q_rows = pl.ds(pl.multiple_of(r * GRID_W, GRID_W), GRID_W)
            k_rows = pl.ds(pl.multiple_of(rs * GRID_W, GRID_W), nk)
            for pair in range(2):
                slot = 2 * sub + pair
                lo = pair * hd2
                q = qk_ref[q_rows, lo:lo + hd2].astype(F32) * (HEAD_DIM ** -0.5 * LOG2E)
                q = _stack_heads(q.astype(BF16))
                k = qk_ref[k_rows, GROUP_W + lo:GROUP_W + lo + hd2]
                v = _with_ones(vg_ref[k_rows, lo:lo + hd2])
                s_scr[slot] = lax.dot_general(q, k, (((1,), (1,)), ((), ())), preferred_element_type=F32)
                _softmax_strips(s_scr.at[slot], p_scr.at[slot],
                                lambda r0, r1, pair=pair, off=off: bias_scr[pair, off, r0:r1, :],
                                2 * GRID_W, nk)
                oa = jnp.dot(p_scr[slot], v, preferred_element_type=F32)
                o = oa[:, 0:hd2] / oa[:, hd2:2 * hd2]
                gate = vg_ref[q_rows, GROUP_W + lo:GROUP_W + lo + hd2].astype(F32)
                o_ref[q_rows, lo:lo + hd2] = (_unstack_heads(o, GRID_W) * _silu(gate)).astype(BF16)
        return carry

    lax.fori_loop(0, n_rows // NA_UNROLL, rows_trip, 0)


def _neighbourhood(z3, bias_tab, l):
    b = z3.shape[0]
    return pl.pallas_call(
        _na_kernel,
        grid=(b,),
        in_specs=[
            pl.BlockSpec((None, SEQ, 2 * GROUP_W), lambda i: (i, 0, 3)),
            pl.BlockSpec((None, SEQ, 2 * GROUP_W), lambda i: (i, 0, 4)),
            pl.BlockSpec((None,) + bias_tab.shape[1:], lambda i: (l, 0, 0, 0, 0)),
        ],
        out_specs=pl.BlockSpec((None, SEQ, GROUP_W), lambda i: (i, 0, 0)),
        out_shape=jax.ShapeDtypeStruct((b, SEQ, GROUP_W), BF16),
        scratch_shapes=[pltpu.VMEM((2 * NA_UNROLL, 2 * GRID_W, NA_ROWS * GRID_W), F32),
                        pltpu.VMEM((2 * NA_UNROLL, 2 * GRID_W, NA_ROWS * GRID_W), BF16),
                        pltpu.VMEM((2, NA_ROWS, 2 * GRID_W, NA_ROWS * GRID_W), F32)],
        compiler_params=_params(0, 1),
        name="neighbourhood_attention",
    )(z3, z3, bias_tab)


def _ssm_weights(a_re, a_im, log_dt, b_re, b_im, c_re, c_im):
    L, G, P, C = SSM_L, SSM_GROUPS, SSM_STATE, SSM_CH
    a_re, a_im, log_dt = a_re.astype(F32), a_im.astype(F32), log_dt.astype(F32)
    dt = jnp.exp(log_dt)[..., None]
    ks = jnp.arange(L + 1, dtype=F32)
    mag = jnp.exp((a_re * dt)[..., None] * ks)
    ang = (a_im * dt)[..., None] * ks
    pr, pim = mag * jnp.cos(ang), mag * jnp.sin(ang)
    abr, abi = pr[..., 1], pim[..., 1]
    den = a_re * a_re + a_im * a_im
    gr = ((abr - 1.0) * a_re + abi * a_im) / den
    gi = (abi * a_re - (abr - 1.0) * a_im) / den
    br, bi = b_re.astype(F32), b_im.astype(F32)
    bbr = gr[..., None] * br - gi[..., None] * bi
    bbi = gr[..., None] * bi + gi[..., None] * br
    cr, ci = c_re.astype(F32), c_im.astype(F32)
    cb_r = cr[:, :, None] * bbr.transpose(0, 1, 3, 2)[:, :, :, None] \
        - ci[:, :, None] * bbi.transpose(0, 1, 3, 2)[:, :, :, None]
    cb_i = cr[:, :, None] * bbi.transpose(0, 1, 3, 2)[:, :, :, None] \
        + ci[:, :, None] * bbr.transpose(0, 1, 3, 2)[:, :, :, None]
    kern = (jnp.einsum('dgxcp,dgpk->dgxck', cb_r, pr, precision=HI)
            - jnp.einsum('dgxcp,dgpk->dgxck', cb_i, pim, precision=HI))
    kf, kb = kern[0], kern[1]
    k_lag = jnp.concatenate([kb[..., 1:L][..., ::-1], (kf[..., 0] + kb[..., 0])[..., None],
                             kf[..., 1:L], jnp.zeros(kf.shape[:-1] + (1,), F32)], axis=-1)
    k_lag = k_lag.reshape(G, C * C, 2 * L)

    def state_in(d, idx):
        a_r = pr[d][:, :, idx].transpose(0, 2, 1)[:, None]
        a_i = pim[d][:, :, idx].transpose(0, 2, 1)[:, None]
        b_r = bbr[d].transpose(0, 2, 1)[:, :, None]
        b_i = bbi[d].transpose(0, 2, 1)[:, :, None]
        return a_r * b_r - a_i * b_i, a_r * b_i + a_i * b_r

    sf_r, sf_i = state_in(0, slice(L - 1, None, -1))
    sb_r, sb_i = state_in(1, slice(0, L))
    w_s = jnp.concatenate([sf_r, sb_r, sf_i, sb_i], axis=-1).reshape(G, C * L, 4 * P)

    def state_out(d, idx):
        c_r = cr[d].transpose(0, 2, 1)[..., None]
        c_i = ci[d].transpose(0, 2, 1)[..., None]
        a_r = pr[d][:, :, idx][:, :, None, :]
        a_i = pim[d][:, :, idx][:, :, None, :]
        return c_r * a_r - c_i * a_i, c_r * a_i + c_i * a_r

    ff_r, ff_i = state_out(0, slice(1, L + 1))
    fb_r, fb_i = state_out(1, slice(L, 0, -1))
    w_c = jnp.concatenate([ff_r, fb_r, -ff_i, -fb_i], axis=1).reshape(G, 4 * P, C * L)
    a_l = jnp.concatenate([pr[0, ..., L], pr[1, ..., L], pim[0, ..., L], pim[1, ..., L]], axis=-1)
    a_l = jnp.broadcast_to(a_l[:, None, :], (G, SSM_NB_MAX, 4 * P))
    return k_lag, w_s.astype(BF16), w_c.astype(BF16), a_l


def _ssm_toeplitz_kernel(k_ref, t_ref):
    L, C = SSM_L, SSM_CH

    def build(cp, carry):
        for c in range(C):
            lag = jnp.broadcast_to(k_ref[pl.ds(cp * C + c, 1), :], (L, 2 * L))
            tile = pltpu.roll(lag, L + 1, 1, stride=1, stride_axis=0)[:, 0:L]
            t_ref[pl.ds(pl.multiple_of(cp * L, L), L), c * L:(c + 1) * L] = tile.astype(BF16)
        return carry

    lax.fori_loop(0, C, build, 0)


def _ssm_toeplitz(k_lag):
    n, C, L = k_lag.shape[0], SSM_CH, SSM_L
    return pl.pallas_call(
        _ssm_toeplitz_kernel,
        grid=(n,),
        in_specs=[pl.BlockSpec((None, C * C, 2 * L), lambda g: (g, 0, 0))],
        out_specs=pl.BlockSpec((None, C * L, C * L), lambda g: (g, 0, 0)),
        out_shape=jax.ShapeDtypeStruct((n, C * L, C * L), BF16),
        compiler_params=_params(),
        name="ssm_toeplitz",
    )(k_lag)


def _ssm_kernel(u_ref, t_ref, ws_ref, wc_ref, al_ref, y_ref, x_scr, y_scr,
                s_re, s_im, ha_re, ha_im, hb_re, hb_im, *, nb):
    L, C, P = SSM_L, SSM_CH, SSM_STATE
    nc = SEQ // L
    m = nb * nc
    half = 2 * P

    for b in range(nb):
        ub = u_ref[b].astype(F32)
        for j in range(nc):
            r0 = (b * nc + j) * SSM_PITCH
            x_scr[r0:r0 + C, :] = ub[:, j * L:(j + 1) * L]
    lhs = jnp.concatenate([x_scr[pl.ds(c, m, stride=SSM_PITCH), :].astype(BF16) for c in range(C)],
                          axis=1)

    s = jnp.dot(lhs, ws_ref[...], preferred_element_type=F32)
    s_re[...] = s[:, 0:half]
    s_im[...] = s[:, half:2 * half]
    a_r = al_ref[0:nb, 0:half]
    a_i = al_ref[0:nb, half:2 * half]
    fwd = lax.broadcasted_iota(jnp.int32, (nb, half), 1) < P
    h_r = jnp.zeros((nb, half), F32)
    h_i = jnp.zeros((nb, half), F32)
    for k in range(nc):
        rf = pl.ds(k, nb, stride=nc)
        rb = pl.ds(nc - 1 - k, nb, stride=nc)
        ha_re[rf, :] = h_r
        ha_im[rf, :] = h_i
        hb_re[rb, :] = h_r
        hb_im[rb, :] = h_i
        s_r = jnp.where(fwd, s_re[rf, :], s_re[rb, :])
        s_i = jnp.where(fwd, s_im[rf, :], s_im[rb, :])
        h_r, h_i = a_r * h_r - a_i * h_i + s_r, a_r * h_i + a_i * h_r + s_i
    fwd_m = lax.broadcasted_iota(jnp.int32, (m, half), 1) < P
    h_all = jnp.concatenate([jnp.where(fwd_m, ha_re[...], hb_re[...]),
                             jnp.where(fwd_m, ha_im[...], hb_im[...])], axis=1).astype(BF16)
    y = jnp.dot(lhs, t_ref[...], preferred_element_type=F32)
    y = y + jnp.dot(h_all, wc_ref[...], preferred_element_type=F32)
    for c in range(C):
        y_scr[pl.ds(c, m, stride=SSM_PITCH), :] = y[:, c * L:(c + 1) * L]
    for b in range(nb):
        y_ref[b] = jnp.concatenate(
            [y_scr[(b * nc + j) * SSM_PITCH:(b * nc + j) * SSM_PITCH + C, :] for j in range(nc)], axis=1)


def _ssm_core(zt, w_t, w_s, w_c, a_l, l):
    b = zt.shape[0]
    L, G, C, P = SSM_L, SSM_GROUPS, SSM_CH, SSM_STATE
    nb = min(b, SSM_NB_MAX)
    m = nb * (SEQ // L)
    weight = lambda r, c: pl.BlockSpec((None, r, c), lambda g, i: (l * G + g, 0, 0))
    return pl.pallas_call(
        functools.partial(_ssm_kernel, nb=nb),
        grid=(G, b // nb),
        in_specs=[pl.BlockSpec((nb, C, SEQ), lambda g, i: (i, g, 0)),
                  weight(C * L, C * L), weight(C * L, 4 * P), weight(4 * P, C * L),
                  weight(SSM_NB_MAX, 4 * P)],
        out_specs=pl.BlockSpec((nb, C, SEQ), lambda g, i: (i, g, 0)),
        out_shape=jax.ShapeDtypeStruct((b, G * C, SEQ), F32),
        scratch_shapes=[pltpu.VMEM((m * SSM_PITCH, L), F32), pltpu.VMEM((m * SSM_PITCH, L), F32)]
        + [pltpu.VMEM((m, 2 * P), F32) for _ in range(6)],
        compiler_params=_params(2),
        name="ssm_core",
    )(zt, w_t, w_s, w_c, a_l)


def _block_diag(w):
    n, k = w.shape[0], w.shape[1]
    out = jnp.zeros((n * k, n * k), w.dtype)
    for i in range(n):
        out = lax.dynamic_update_slice(out, w[i], (i * k, i * k))
    return out


def kernel(x_prompt, x_sample, norm_g, w_in, w_out, pool_w, pool_scale, ssm_a_re, ssm_a_im,
           ssm_log_dt, ssm_b_re, ssm_b_im, ssm_c_re, ssm_c_im, ssm_d, glu_w, glu_b, na_rpb, final_g):
    inv_cnt = _pool_inv_count()
    cos_t, sin_t = _rope_tables()
    mask_t = _dilated_mask_table()
    gw = GROUP_W
    norm_all = norm_g.reshape(DEPTH, 1, D_MODEL).astype(F32)
    w_abd = jnp.concatenate([w_out[:, :2 * gw], w_out[:, 3 * gw:]], axis=1).astype(BF16)
    w_c = w_out[:, 2 * gw:3 * gw].astype(BF16)
    pool_bd = jax.vmap(_block_diag)(pool_w).astype(BF16)
    pool_sc = pool_scale.reshape(DEPTH, 1, gw).astype(F32)
    k_lag, ssm_ws, ssm_wc, ssm_al = jax.vmap(_ssm_weights)(
        ssm_a_re, ssm_a_im, ssm_log_dt, ssm_b_re, ssm_b_im, ssm_c_re, ssm_c_im)
    merge = lambda a: a.reshape((DEPTH * SSM_GROUPS,) + a.shape[2:])
    ssm = (_ssm_toeplitz(merge(k_lag)), merge(ssm_ws), merge(ssm_wc), merge(ssm_al))
    d_all = ssm_d.reshape(DEPTH, gw, 1).astype(F32)
    glu_wt = jnp.swapaxes(glu_w, 1, 2).astype(BF16)
    glu_b_all = glu_b.reshape(DEPTH, gw, 1).astype(F32)
    na_bias = jax.vmap(_na_bias_table)(na_rpb)
    fg = final_g.reshape(1, D_MODEL).astype(F32)
    nxt = (norm_all, w_in.astype(BF16))

    def trunk(x):
        b = x.shape[0]
        x2 = x.reshape(b * SEQ, D_MODEL)
        z2, zt = _proj(x2, nxt=nxt, nxt_layer=0)
        for l in range(DEPTH):
            z3 = z2.reshape(b, SEQ, TOK_W)
            y_a = _pool(z3, inv_cnt, pool_bd, pool_sc, l)
            y_b = _dilated(z3, cos_t, sin_t, mask_t)
            y_ct = _ssm_core(zt, *ssm, l)
            y_d = _neighbourhood(z3, na_bias, l)
            flat = lambda a: a.reshape(b * SEQ, gw)
            mix = (flat(y_a), flat(y_b), flat(y_d), y_ct, zt, d_all, glu_wt, glu_b_all, w_abd, w_c, fg)
            if l + 1 < DEPTH:
                x2, z2, zt = _proj(x2, mix=mix, mix_layer=l, nxt=nxt, nxt_layer=l + 1)
            else:
                (x2,) = _proj(x2, mix=mix, mix_layer=l, final=True)
        return x2.reshape(b, SEQ, D_MODEL)

    return (trunk(x_prompt), trunk(x_sample))
```

```python
import functools
import math

import numpy as np
import jax
import jax.numpy as jnp
from jax import lax
from jax.experimental import pallas as pl
from jax.experimental.pallas import tpu as pltpu

D_MODEL = 1024
SEQ = 2048
DEPTH = 4
GROUP_W = 256
HEAD_DIM = 64
POOL_WINDOWS = (2, 4, 8, 16)
POOL_GROUP = 64
DILATED_PATTERNS = ((128, 1), (512, 4), (2048, 16))
SSM_CH = 16
SSM_GROUPS = 16
SSM_STATE = 64
GRID_W = 64
NA_ROWS = 8
NA_COLS = 16
ROPE_THETA = 10000.0
EPS = 1e-6
PROJ_W = 12 * GROUP_W
NEG = -1e30

F32 = jnp.float32
BF16 = jnp.bfloat16
HI = lax.Precision.HIGHEST

ROW_TILE = 512
POOL_PAD = 16
DIL_Q = 128
DIL_R = 64
DIL_BLOCKS = SEQ // DIL_Q
DIL_REGROUP = 4
STRIP = 32
ATT_UNROLL = 8
NA_UNROLL = 16
LOG2E = math.log2(math.e)
SSM_L = 128
SSM_NB_MAX = 16
SSM_PITCH = 24
TOK_W = 10 * GROUP_W
VMEM_LIMIT = 56 * 1024 * 1024


def _sigmoid(x):
    return 1.0 / (1.0 + jnp.exp(-x))


def _silu(x):
    return x * _sigmoid(x)


def _gelu_tanh(x):
    return 0.5 * x * (1.0 + jnp.tanh(math.sqrt(2.0 / math.pi) * (x + 0.044715 * (x * x * x))))


def _toeplitz(k, axis, a, b):
    k = jnp.moveaxis(k, axis, -1)
    w = a + b - 1
    lead = k.shape[:-1]
    kp = jnp.concatenate([k, jnp.zeros(lead + (1,), k.dtype)], axis=-1)
    t = jnp.broadcast_to(kp[..., None, :], lead + (a, w + 1)).reshape(lead + (a * (w + 1),))
    t = t[..., :a * w].reshape(lead + (a, w))[..., a - 1:a - 1 + b]
    return jnp.moveaxis(t, (-2, -1), (axis, axis + 1))


def _params(n_parallel=1, n_arbitrary=0):
    sem = ("parallel",) * n_parallel + ("arbitrary",) * n_arbitrary
    return pltpu.CompilerParams(dimension_semantics=sem, vmem_limit_bytes=VMEM_LIMIT)


def _proj_kernel(*refs, mix_out, proj_in, final):
    refs = list(refs)
    x_ref = refs.pop(0)
    if mix_out:
        (ya_ref, yb_ref, yd_ref, yct_ref, ztp_ref, d_ref, gwt_ref, gb_ref,
         wabd_ref, wc_ref, fg_ref) = refs[:11]
        del refs[:11]
    if proj_in:
        g_ref, w_ref = refs[:2]
        del refs[:2]
    if mix_out:
        xo_ref = refs.pop(0)
    if proj_in:
        z_ref, zt_ref = refs

    x = x_ref[...]
    if mix_out:
        u = ztp_ref[0:GROUP_W, :].astype(F32)
        y = u * d_ref[...] + yct_ref[...]
        g = _gelu_tanh(y)
        lin = jnp.dot(gwt_ref[...], g.astype(BF16), preferred_element_type=F32) + gb_ref[...]
        g = g * _sigmoid(lin)
        y_c = (g * _silu(ztp_ref[GROUP_W:2 * GROUP_W, :].astype(F32))).astype(BF16)
        y_abd = jnp.concatenate([ya_ref[...], yb_ref[...], yd_ref[...]], axis=1)
        acc = jnp.dot(y_abd, wabd_ref[...], preferred_element_type=F32)
        acc = acc + lax.dot_general(y_c, wc_ref[...], (((0,), (0,)), ((), ())),
                                    preferred_element_type=F32)
        x = x + acc
        if final:
            ms = jnp.mean(x * x, axis=-1, keepdims=True)
            x = x * lax.rsqrt(ms + EPS) * fg_ref[...]
        xo_ref[...] = x
    if proj_in:
        ms = jnp.mean(x * x, axis=-1, keepdims=True)
        h = (x * lax.rsqrt(ms + EPS) * g_ref[...]).astype(BF16)
        nch = 2 * GROUP_W
        for n, src in enumerate((0, 1, 2, 4, 5)):
            z_ref[:, n * nch:(n + 1) * nch] = jnp.dot(
                h, w_ref[:, src * nch:(src + 1) * nch], preferred_element_type=F32).astype(BF16)
        zt_ref[...] = lax.dot_general(w_ref[:, 3 * nch:4 * nch], h, (((0,), (1,)), ((), ())),
                                      preferred_element_type=F32).astype(BF16)


def _proj(x2, mix=None, mix_layer=None, nxt=None, nxt_layer=None, final=False):
    rows = x2.shape[0]
    per_seq = SEQ // ROW_TILE
    tile = lambda w: pl.BlockSpec((ROW_TILE, w), lambda i: (i, 0))
    chan = lambda c: pl.BlockSpec((None, c, ROW_TILE), lambda i: (i // per_seq, 0, i % per_seq))
    layer = lambda l, r, c: pl.BlockSpec((None, r, c), lambda i: (l, 0, 0))
    args, in_specs, out_specs, out_shape = [x2], [tile(D_MODEL)], [], []
    if mix is not None:
        args += list(mix)
        ml = mix_layer
        in_specs += [tile(GROUP_W), tile(GROUP_W), tile(GROUP_W), chan(GROUP_W), chan(2 * GROUP_W),
                     layer(ml, GROUP_W, 1), layer(ml, GROUP_W, GROUP_W), layer(ml, GROUP_W, 1),
                     layer(ml, 3 * GROUP_W, D_MODEL), layer(ml, GROUP_W, D_MODEL),
                     pl.BlockSpec((1, D_MODEL), lambda i: (0, 0))]
        out_specs.append(tile(D_MODEL))
        out_shape.append(jax.ShapeDtypeStruct((rows, D_MODEL), F32))
    if nxt is not None:
        args += list(nxt)
        nl = nxt_layer
        in_specs += [layer(nl, 1, D_MODEL), layer(nl, D_MODEL, PROJ_W)]
        out_specs += [tile(TOK_W), chan(2 * GROUP_W)]
        out_shape += [jax.ShapeDtypeStruct((rows, TOK_W), BF16),
                      jax.ShapeDtypeStruct((rows // SEQ, 2 * GROUP_W, SEQ), BF16)]
    return pl.pallas_call(
        functools.partial(_proj_kernel, mix_out=mix is not None, proj_in=nxt is not None, final=final),
        grid=(rows // ROW_TILE,),
        in_specs=in_specs, out_specs=out_specs, out_shape=out_shape,
        compiler_params=_params(),
        name="proj",
    )(*args)


def _pool_kernel(z_ref, inv_ref, w_ref, scale_ref, o_ref, ub, s2b, s4b, s8b):
    s = SEQ
    p = POOL_PAD
    n = s + p
    u = z_ref[:, 0:GROUP_W].astype(F32)
    zero_p = jnp.zeros((p, GROUP_W), F32)
    zero_8 = jnp.zeros((8, GROUP_W), F32)
    ub[0:p, :] = zero_p
    ub[p:p + s, :] = u
    ub[p + s:p + s + p, :] = zero_p
    half = GROUP_W // 2
    for buf in (s2b, s4b, s8b):
        buf[0:8, :] = zero_8[:, 0:buf.shape[1]]
        buf[s + 24:s + 32, :] = zero_8[:, 0:buf.shape[1]]
    s2b[8:8 + n, :] = ub[7:7 + n, :] + ub[8:8 + n, :]
    s4b[8:8 + n, :] = s2b[7:7 + n, :] + s2b[9:9 + n, :]
    s8b[8:8 + n, :] = s4b[6:6 + n, half:] + s4b[10:10 + n, half:]
    s16 = s8b[p - 4:p - 4 + s, :] + s8b[p + 4:p + 4 + s, :]
    lane = lax.broadcasted_iota(jnp.int32, (s, half), 1)
    win = jnp.concatenate([
        jnp.where(lane < POOL_GROUP, s2b[p:p + s, 0:half], s4b[p:p + s, 0:half]),
        jnp.where(lane < POOL_GROUP, s8b[p:p + s, :], s16)], axis=1)
    diff = (win * inv_ref[...] - u).astype(BF16)
    y = jnp.dot(diff, w_ref[...], preferred_element_type=F32) * scale_ref[...]
    gate = z_ref[:, GROUP_W:2 * GROUP_W].astype(F32)
    o_ref[...] = (y * _silu(gate)).astype(BF16)


def _pool_inv_count():
    t = np.arange(SEQ)
    cols = []
    for w in POOL_WINDOWS:
        lo = np.clip(t - w // 2, 0, SEQ)
        hi = np.clip(t + w // 2, 0, SEQ)
        cols.append(np.repeat((1.0 / (hi - lo))[:, None], POOL_GROUP, axis=1))
    return jnp.asarray(np.concatenate(cols, axis=1), F32)


def _pool(z3, inv_cnt, w_bd, scale, l):
    b = z3.shape[0]
    return pl.pallas_call(
        _pool_kernel,
        grid=(b,),
        in_specs=[
            pl.BlockSpec((None, SEQ, 2 * GROUP_W), lambda i: (i, 0, 0)),
            pl.BlockSpec((SEQ, GROUP_W), lambda i: (0, 0)),
            pl.BlockSpec((None, GROUP_W, GROUP_W), lambda i: (l, 0, 0)),
            pl.BlockSpec((None, 1, GROUP_W), lambda i: (l, 0, 0)),
        ],
        out_specs=pl.BlockSpec((None, SEQ, GROUP_W), lambda i: (i, 0, 0)),
        out_shape=jax.ShapeDtypeStruct((b, SEQ, GROUP_W), BF16),
        scratch_shapes=[pltpu.VMEM((SEQ + 2 * POOL_PAD, GROUP_W), F32) for _ in range(3)]
        + [pltpu.VMEM((SEQ + 2 * POOL_PAD, GROUP_W // 2), F32)],
        compiler_params=_params(),
        name="pool_mixer",
    )(z3, inv_cnt, w_bd, scale)


def _rope_tables():
    inv = ROPE_THETA ** (-np.arange(0, HEAD_DIM, 2, dtype=np.float64) / HEAD_DIM)
    ang = np.arange(SEQ, dtype=np.float64)[:, None] * inv[None, :]
    cos = np.concatenate([np.cos(ang), np.cos(ang)], axis=1)
    sin = np.concatenate([-np.sin(ang), np.sin(ang)], axis=1)
    return (jnp.asarray(np.tile(cos, (1, 2)), F32), jnp.asarray(np.tile(sin, (1, 2)), F32))


def _stack_heads(q):
    lane = lax.broadcasted_iota(jnp.int32, q.shape, 1)
    zero = jnp.zeros_like(q)
    return jnp.concatenate([jnp.where(lane < HEAD_DIM, q, zero),
                            jnp.where(lane < HEAD_DIM, zero, q)], axis=0)


def _unstack_heads(o, n):
    lane = lax.broadcasted_iota(jnp.int32, (n, 2 * HEAD_DIM), 1)
    return jnp.where(lane < HEAD_DIM, o[0:n], o[n:2 * n])


def _dilated_mask_table():
    q = np.arange(DIL_Q)[None, :, None]
    k = np.arange(2 * DIL_Q)[None, None, :]
    shift = (np.arange(3) * DIL_R)[:, None, None]
    return jnp.asarray(np.where(np.abs(k - q - shift) <= DIL_R, 0.0, NEG), F32)


def _dilated_kernel(qk_ref, vg_ref, cos_ref, sin_ref, mask_ref, o_ref, qf, kf, vf, q4, k4, v4,
                    acc_o, acc_l, s_scr, p_scr, m_scr):
    s = SEQ
    hd2 = 2 * HEAD_DIM
    lane2 = lax.broadcasted_iota(jnp.int32, (s, 2 * HEAD_DIM), 1)
    first_half = (lane2 % HEAD_DIM) < (HEAD_DIM // 2)
    cos = cos_ref[...]
    sin = sin_ref[...]
    for pair in range(2):
        lo = pair * 2 * HEAD_DIM
        for src, dst, scale in ((0, qf, HEAD_DIM ** -0.5 * LOG2E), (GROUP_W, kf, 1.0)):
            x = qk_ref[:, src + lo:src + lo + 2 * HEAD_DIM].astype(F32)
            swapped = jnp.where(first_half,
                                pltpu.roll(x, 2 * HEAD_DIM - HEAD_DIM // 2, 1),
                                pltpu.roll(x, HEAD_DIM // 2, 1))
            roped = x * cos + swapped * sin
            dst[pair] = roped if scale == 1.0 else roped * scale
        vf[pair] = vg_ref[:, lo:lo + 2 * HEAD_DIM].astype(F32)

    quarter = s // DIL_REGROUP
    for pair in range(2):
        for nat, grp in ((qf, q4), (kf, k4), (vf, v4)):
            for rho in range(DIL_REGROUP):
                grp[pair, rho * quarter:(rho + 1) * quarter, :] = nat[
                    pair, pl.ds(rho, quarter, stride=DIL_REGROUP), :]

    pattern_blocks = []
    for pi, (window, d) in enumerate(DILATED_PATTERNS):
        n_sub = s // d
        kw = min(2 * DIL_Q, n_sub)
        nb = n_sub // DIL_Q
        q_src, k_src, v_src = (qf, kf, vf) if d == 1 else (q4, k4, v4)

        def rows(rho, start, size, d=d):
            if d == 1:
                return pl.ds(pl.multiple_of(start, DIL_R), size)
            dd = d // DIL_REGROUP
            base = (rho % DIL_REGROUP) * quarter + rho // DIL_REGROUP + dd * start
            if dd == 1:
                return pl.ds(pl.multiple_of(base, DIL_R), size)
            return pl.ds(base, size, stride=dd)

        def block(trip, carry, d=d, n_sub=n_sub, kw=kw, nb=nb, pi=pi, rows=rows,
                  q_src=q_src, k_src=k_src, v_src=v_src):
            for sub in range(ATT_UNROLL):
                it = trip * ATT_UNROLL + sub
                rho = it // nb
                q0 = (it % nb) * DIL_Q
                ks = jnp.clip(q0 - DIL_R, 0, n_sub - kw)
                case = (q0 - ks) // DIL_R
                q_rows = rows(rho, q0, DIL_Q)
                k_rows = rows(rho, ks, kw)
                for pair in range(2):
                    slot = 2 * sub + pair
                    q = _stack_heads(q_src[pair, q_rows, :].astype(BF16))
                    k = k_src[pair, k_rows, :].astype(BF16)
                    v = _with_ones(v_src[pair, k_rows, :].astype(BF16))
                    s_scr[slot, :, 0:kw] = lax.dot_general(q, k, (((1,), (1,)), ((), ())),
                                                           preferred_element_type=F32)
                    _softmax_strips(
                        s_scr.at[slot], p_scr.at[slot],
                        lambda r0, r1, kw=kw, case=case: mask_ref[
                            case, r0 % DIL_Q:r0 % DIL_Q + (r1 - r0), 0:kw],
                        2 * DIL_Q, kw, m_scr.at[slot])
                    oa = jnp.dot(p_scr[slot, :, 0:kw], v, preferred_element_type=F32)
                    den = oa[:, hd2:2 * hd2]
                    o = oa[:, 0:hd2] / den
                    lse = m_scr[slot] + jnp.log2(den)
                    acc_o[pi, pair, q_rows, :] = _unstack_heads(o, DIL_Q)
                    acc_l[pi, pair, q_rows, :] = _unstack_heads(lse, DIL_Q)
            return carry

        assert d * nb == DIL_BLOCKS
        pattern_blocks.append(block)

    def trip_body(trip, carry):
        for blocks_of_pattern in pattern_blocks:
            blocks_of_pattern(trip, carry)
        return carry

    lax.fori_loop(0, DIL_BLOCKS // ATT_UNROLL, trip_body, 0)

    ch = 256
    for rho in range(DIL_REGROUP):
        for c in range(quarter // ch):
            g_rows = slice(rho * quarter + c * ch, rho * quarter + (c + 1) * ch)
            n_rows = pl.ds(rho + DIL_REGROUP * c * ch, ch, stride=DIL_REGROUP)
            for pair in range(2):
                l0, l1, l2 = acc_l[0, pair, n_rows, :], acc_l[1, pair, g_rows, :], acc_l[2, pair, g_rows, :]
                top = jnp.maximum(jnp.maximum(l0, l1), l2)
                w0, w1, w2 = jnp.exp2(l0 - top), jnp.exp2(l1 - top), jnp.exp2(l2 - top)
                acc_o[0, pair, n_rows, :] = (
                    (w0 * acc_o[0, pair, n_rows, :] + w1 * acc_o[1, pair, g_rows, :]
                     + w2 * acc_o[2, pair, g_rows, :]) / (w0 + w1 + w2))
    for c in range(s // ch):
        r = slice(c * ch, (c + 1) * ch)
        for pair in range(2):
            lo = pair * 2 * HEAD_DIM
            gate = vg_ref[r, GROUP_W + lo:GROUP_W + lo + 2 * HEAD_DIM].astype(F32)
            o_ref[r, lo:lo + 2 * HEAD_DIM] = (acc_o[0, pair, r, :] * _silu(gate)).astype(BF16)


def _dilated(z3, cos_t, sin_t, mask_t):
    b = z3.shape[0]
    return pl.pallas_call(
        _dilated_kernel,
        grid=(b,),
        in_specs=[
            pl.BlockSpec((None, SEQ, 2 * GROUP_W), lambda i: (i, 0, 1)),
            pl.BlockSpec((None, SEQ, 2 * GROUP_W), lambda i: (i, 0, 2)),
            pl.BlockSpec((SEQ, 2 * HEAD_DIM), lambda i: (0, 0)),
            pl.BlockSpec((SEQ, 2 * HEAD_DIM), lambda i: (0, 0)),
            pl.BlockSpec(mask_t.shape, lambda i: (0, 0, 0)),
        ],
        out_specs=pl.BlockSpec((None, SEQ, GROUP_W), lambda i: (i, 0, 0)),
        out_shape=jax.ShapeDtypeStruct((b, SEQ, GROUP_W), BF16),
        scratch_shapes=[pltpu.VMEM((2, SEQ, 2 * HEAD_DIM), F32) for _ in range(6)]
        + [pltpu.VMEM((len(DILATED_PATTERNS), 2, SEQ, 2 * HEAD_DIM), F32) for _ in range(2)]
        + [pltpu.VMEM((2 * ATT_UNROLL, 2 * DIL_Q, 2 * DIL_Q), F32),
           pltpu.VMEM((2 * ATT_UNROLL, 2 * DIL_Q, 2 * DIL_Q), BF16),
           pltpu.VMEM((2 * ATT_UNROLL, 2 * DIL_Q, 2 * HEAD_DIM), F32)],
        compiler_params=_params(),
        name="dilated_attention",
    )(z3, z3, cos_t, sin_t, mask_t)


def _na_bias_table(rpb):
    c = np.arange(GRID_W)
    cs = np.clip(c - NA_COLS // 2, 0, GRID_W - NA_COLS)
    col_ok = (c[None, :] >= cs[:, None]) & (c[None, :] < cs[:, None] + NA_COLS)
    pad = GRID_W - NA_COLS
    tab = jnp.pad(rpb.astype(F32), ((0, 0), (0, 0), (pad, pad)))
    tab = _toeplitz(tab, 2, GRID_W, GRID_W)
    tab = jnp.where(col_ok[None, None, :, :], tab * LOG2E, NEG)
    h, nd = rpb.shape[0], 2 * NA_ROWS - 1
    tab = tab.reshape(h // 2, 2, nd, GRID_W, GRID_W).transpose(0, 2, 1, 3, 4)
    tab = tab.reshape(h // 2, nd, 2 * GRID_W, GRID_W)
    return jnp.concatenate([tab[:, :-1], tab[:, 1:]], axis=-1)


def _softmax_strips(s_scr, p_scr, bias, n_rows, n_keys, m_scr=None):
    for r0 in range(0, n_rows, STRIP):
        t = s_scr[r0:r0 + STRIP, 0:n_keys] + bias(r0, r0 + STRIP)
        m = jnp.max(t, axis=1, keepdims=True)
        p_scr[r0:r0 + STRIP, 0:n_keys] = jnp.exp2(t - m).astype(BF16)
        if m_scr is not None:
            m_scr[r0:r0 + STRIP, :] = jnp.broadcast_to(m, (STRIP, 2 * HEAD_DIM))


def _with_ones(v):
    return jnp.concatenate([v, jnp.ones_like(v)], axis=1)


def _na_kernel(qk_ref, vg_ref, bias_ref, o_ref, s_scr, p_scr, bias_scr):
    n_rows = SEQ // GRID_W
    nk = NA_ROWS * GRID_W
    hd2 = 2 * HEAD_DIM

    @pl.when(pl.program_id(0) == 0)
    def _():
        for pair in range(2):
            for off in range(NA_ROWS):
                for j in range(NA_ROWS // 2):
                    bias_scr[pair, off, :, 2 * j * GRID_W:(2 * j + 2) * GRID_W] = bias_ref[
                        pair, 2 * j - off + NA_ROWS - 1]

    def rows_trip(trip, carry):
        for sub in range(NA_UNROLL):
            r = trip * NA_UNROLL + sub
            rs = jnp.clip(r - NA_ROWS // 2, 0, n_rows - NA_ROWS)
            off = r - rs
            q_rows = pl.ds(pl.multiple_of(r * GRID_W, GRID_W), GRID_W)
            k_rows = pl.ds(pl.multiple_of(rs * GRID_W, GRID_W), nk)
            for pair in range(2):
                slot = 2 * sub + pair
                lo = pair * hd2
                q = qk_ref[q_rows, lo:lo + hd2].astype(F32) * (HEAD_DIM ** -0.5 * LOG2E)
                q = _stack_heads(q.astype(BF16))
                k = qk_ref[k_rows, GROUP_W + lo:GROUP_W + lo + hd2]
                v = _with_ones(vg_ref[k_rows, lo:lo + hd2])
                s_scr[slot] = lax.dot_general(q, k, (((1,), (1,)), ((), ())), preferred_element_type=F32)
                _softmax_strips(s_scr.at[slot], p_scr.at[slot],
                                lambda r0, r1, pair=pair, off=off: bias_scr[pair, off, r0:r1, :],
                                2 * GRID_W, nk)
                oa = jnp.dot(p_scr[slot], v, preferred_element_type=F32)
                o = oa[:, 0:hd2] / oa[:, hd2:2 * hd2]
                gate = vg_ref[q_rows, GROUP_W + lo:GROUP_W + lo + hd2].astype(F32)
                o_ref[q_rows, lo:lo + hd2] = (_unstack_heads(o, GRID_W) * _silu(gate)).astype(BF16)
        return carry

    lax.fori_loop(0, n_rows // NA_UNROLL, rows_trip, 0)


def _neighbourhood(z3, bias_tab, l):
    b = z3.shape[0]
    return pl.pallas_call(
        _na_kernel,
        grid=(b,),
        in_specs=[
            pl.BlockSpec((None, SEQ, 2 * GROUP_W), lambda i: (i, 0, 3)),
            pl.BlockSpec((None, SEQ, 2 * GROUP_W), lambda i: (i, 0, 4)),
            pl.BlockSpec((None,) + bias_tab.shape[1:], lambda i: (l, 0, 0, 0, 0)),
        ],
        out_specs=pl.BlockSpec((None, SEQ, GROUP_W), lambda i: (i, 0, 0)),
        out_shape=jax.ShapeDtypeStruct((b, SEQ, GROUP_W), BF16),
        scratch_shapes=[pltpu.VMEM((2 * NA_UNROLL, 2 * GRID_W, NA_ROWS * GRID_W), F32),
                        pltpu.VMEM((2 * NA_UNROLL, 2 * GRID_W, NA_ROWS * GRID_W), BF16),
                        pltpu.VMEM((2, NA_ROWS, 2 * GRID_W, NA_ROWS * GRID_W), F32)],
        compiler_params=_params(0, 1),
        name="neighbourhood_attention",
    )(z3, z3, bias_tab)


def _ssm_weights(a_re, a_im, log_dt, b_re, b_im, c_re, c_im):
    L, G, P, C = SSM_L, SSM_GROUPS, SSM_STATE, SSM_CH
    a_re, a_im, log_dt = a_re.astype(F32), a_im.astype(F32), log_dt.astype(F32)
    dt = jnp.exp(log_dt)[..., None]
    ks = jnp.arange(L + 1, dtype=F32)
    mag = jnp.exp((a_re * dt)[..., None] * ks)
    ang = (a_im * dt)[..., None] * ks
    pr, pim = mag * jnp.cos(ang), mag * jnp.sin(ang)
    abr, abi = pr[..., 1], pim[..., 1]
    den = a_re * a_re + a_im * a_im
    gr = ((abr - 1.0) * a_re + abi * a_im) / den
    gi = (abi * a_re - (abr - 1.0) * a_im) / den
    br, bi = b_re.astype(F32), b_im.astype(F32)
    bbr = gr[..., None] * br - gi[..., None] * bi
    bbi = gr[..., None] * bi + gi[..., None] * br
    cr, ci = c_re.astype(F32), c_im.astype(F32)
    cb_r = cr[:, :, None] * bbr.transpose(0, 1, 3, 2)[:, :, :, None] \
        - ci[:, :, None] * bbi.transpose(0, 1, 3, 2)[:, :, :, None]
    cb_i = cr[:, :, None] * bbi.transpose(0, 1, 3, 2)[:, :, :, None] \
        + ci[:, :, None] * bbr.transpose(0, 1, 3, 2)[:, :, :, None]
    kern = (jnp.einsum('dgxcp,dgpk->dgxck', cb_r, pr, precision=HI)
            - jnp.einsum('dgxcp,dgpk->dgxck', cb_i, pim, precision=HI))
    kf, kb = kern[0], kern[1]
    k_lag = jnp.concatenate([kb[..., 1:L][..., ::-1], (kf[..., 0] + kb[..., 0])[..., None],
                             kf[..., 1:L], jnp.zeros(kf.shape[:-1] + (1,), F32)], axis=-1)
    k_lag = k_lag.reshape(G, C * C, 2 * L)

    def state_in(d, idx):
        a_r = pr[d][:, :, idx].transpose(0, 2, 1)[:, None]
        a_i = pim[d][:, :, idx].transpose(0, 2, 1)[:, None]
        b_r = bbr[d].transpose(0, 2, 1)[:, :, None]
        b_i = bbi[d].transpose(0, 2, 1)[:, :, None]
        return a_r * b_r - a_i * b_i, a_r * b_i + a_i * b_r

    sf_r, sf_i = state_in(0, slice(L - 1, None, -1))
    sb_r, sb_i = state_in(1, slice(0, L))
    w_s = jnp.concatenate([sf_r, sb_r, sf_i, sb_i], axis=-1).reshape(G, C * L, 4 * P)

    def state_out(d, idx):
        c_r = jnp.repeat(cr[d].transpose(0, 2, 1), L, axis=2)
        c_i = jnp.repeat(ci[d].transpose(0, 2, 1), L, axis=2)
        a_r = jnp.tile(pr[d][:, :, idx], (1, 1, C))
        a_i = jnp.tile(pim[d][:, :, idx], (1, 1, C))
        return c_r * a_r - c_i * a_i, c_r * a_i + c_i * a_r

    ff_r, ff_i = state_out(0, slice(1, L + 1))
    fb_r, fb_i = state_out(1, slice(L, 0, -1))
    w_c = jnp.concatenate([ff_r, fb_r, -ff_i, -fb_i], axis=1)
    a_l = jnp.concatenate([pr[0, ..., L], pr[1, ..., L], pim[0, ..., L], pim[1, ..., L]], axis=-1)
    a_l = jnp.broadcast_to(a_l[:, None, :], (G, SSM_NB_MAX, 4 * P))
    return k_lag, w_s.astype(BF16), w_c.astype(BF16), a_l


def _ssm_toeplitz_kernel(k_ref, t_ref):
    L, C = SSM_L, SSM_CH

    def build(cp, carry):
        for c in range(C):
            lag = jnp.broadcast_to(k_ref[pl.ds(cp * C + c, 1), :], (L, 2 * L))
            tile = pltpu.roll(lag, L + 1, 1, stride=1, stride_axis=0)[:, 0:L]
            t_ref[pl.ds(pl.multiple_of(cp * L, L), L), c * L:(c + 1) * L] = tile.astype(BF16)
        return carry

    lax.fori_loop(0, C, build, 0)


def _ssm_toeplitz(k_lag):
    n, C, L = k_lag.shape[0], SSM_CH, SSM_L
    return pl.pallas_call(
        _ssm_toeplitz_kernel,
        grid=(n,),
        in_specs=[pl.BlockSpec((None, C * C, 2 * L), lambda g: (g, 0, 0))],
        out_specs=pl.BlockSpec((None, C * L, C * L), lambda g: (g, 0, 0)),
        out_shape=jax.ShapeDtypeStruct((n, C * L, C * L), BF16),
        compiler_params=_params(),
        name="ssm_toeplitz",
    )(k_lag)


def _ssm_kernel(u_ref, t_ref, ws_ref, wc_ref, al_ref, y_ref, x_scr, y_scr,
                s_re, s_im, ha_re, ha_im, hb_re, hb_im, *, nb):
    L, C, P = SSM_L, SSM_CH, SSM_STATE
    nc = SEQ // L
    m = nb * nc
    half = 2 * P

    for b in range(nb):
        ub = u_ref[b].astype(F32)
        for j in range(nc):
            r0 = (b * nc + j) * SSM_PITCH
            x_scr[r0:r0 + C, :] = ub[:, j * L:(j + 1) * L]
    lhs = jnp.concatenate([x_scr[pl.ds(c, m, stride=SSM_PITCH), :].astype(BF16) for c in range(C)],
                          axis=1)

    s = jnp.dot(lhs, ws_ref[...], preferred_element_type=F32)
    s_re[...] = s[:, 0:half]
    s_im[...] = s[:, half:2 * half]
    a_r = al_ref[0:nb, 0:half]
    a_i = al_ref[0:nb, half:2 * half]
    fwd = lax.broadcasted_iota(jnp.int32, (nb, half), 1) < P
    h_r = jnp.zeros((nb, half), F32)
    h_i = jnp.zeros((nb, half), F32)
    for k in range(nc):
        rf = pl.ds(k, nb, stride=nc)
        rb = pl.ds(nc - 1 - k, nb, stride=nc)
        ha_re[rf, :] = h_r
        ha_im[rf, :] = h_i
        hb_re[rb, :] = h_r
        hb_im[rb, :] = h_i
        s_r = jnp.where(fwd, s_re[rf, :], s_re[rb, :])
        s_i = jnp.where(fwd, s_im[rf, :], s_im[rb, :])
        h_r, h_i = a_r * h_r - a_i * h_i + s_r, a_r * h_i + a_i * h_r + s_i
    fwd_m = lax.broadcasted_iota(jnp.int32, (m, half), 1) < P
    h_all = jnp.concatenate([jnp.where(fwd_m, ha_re[...], hb_re[...]),
                             jnp.where(fwd_m, ha_im[...], hb_im[...])], axis=1).astype(BF16)
    y = jnp.dot(lhs, t_ref[...], preferred_element_type=F32)
    y = y + jnp.dot(h_all, wc_ref[...], preferred_element_type=F32)
    for c in range(C):
        y_scr[pl.ds(c, m, stride=SSM_PITCH), :] = y[:, c * L:(c + 1) * L]
    for b in range(nb):
        y_ref[b] = jnp.concatenate(
            [y_scr[(b * nc + j) * SSM_PITCH:(b * nc + j) * SSM_PITCH + C, :] for j in range(nc)], axis=1)


def _ssm_core(zt, w_t, w_s, w_c, a_l, l):
    b = zt.shape[0]
    L, G, C, P = SSM_L, SSM_GROUPS, SSM_CH, SSM_STATE
    nb = min(b, SSM_NB_MAX)
    m = nb * (SEQ // L)
    weight = lambda r, c: pl.BlockSpec((None, r, c), lambda g, i: (l * G + g, 0, 0))
    return pl.pallas_call(
        functools.partial(_ssm_kernel, nb=nb),
        grid=(G, b // nb),
        in_specs=[pl.BlockSpec((nb, C, SEQ), lambda g, i: (i, g, 0)),
                  weight(C * L, C * L), weight(C * L, 4 * P), weight(4 * P, C * L),
                  weight(SSM_NB_MAX, 4 * P)],
        out_specs=pl.BlockSpec((nb, C, SEQ), lambda g, i: (i, g, 0)),
        out_shape=jax.ShapeDtypeStruct((b, G * C, SEQ), F32),
        scratch_shapes=[pltpu.VMEM((m * SSM_PITCH, L), F32), pltpu.VMEM((m * SSM_PITCH, L), F32)]
        + [pltpu.VMEM((m, 2 * P), F32) for _ in range(6)],
        compiler_params=_params(2),
        name="ssm_core",
    )(zt, w_t, w_s, w_c, a_l)


def _block_diag(w):
    n, k = w.shape[0], w.shape[1]
    out = jnp.zeros((n * k, n * k), w.dtype)
    for i in range(n):
        out = lax.dynamic_update_slice(out, w[i], (i * k, i * k))
    return out


def kernel(x_prompt, x_sample, norm_g, w_in, w_out, pool_w, pool_scale, ssm_a_re, ssm_a_im,
           ssm_log_dt, ssm_b_re, ssm_b_im, ssm_c_re, ssm_c_im, ssm_d, glu_w, glu_b, na_rpb, final_g):
    inv_cnt = _pool_inv_count()
    cos_t, sin_t = _rope_tables()
    mask_t = _dilated_mask_table()
    gw = GROUP_W
    norm_all = norm_g.reshape(DEPTH, 1, D_MODEL).astype(F32)
    w_abd = jnp.concatenate([w_out[:, :2 * gw], w_out[:, 3 * gw:]], axis=1).astype(BF16)
    w_c = w_out[:, 2 * gw:3 * gw].astype(BF16)
    pool_bd = jax.vmap(_block_diag)(pool_w).astype(BF16)
    pool_sc = pool_scale.reshape(DEPTH, 1, gw).astype(F32)
    k_lag, ssm_ws, ssm_wc, ssm_al = jax.vmap(_ssm_weights)(
        ssm_a_re, ssm_a_im, ssm_log_dt, ssm_b_re, ssm_b_im, ssm_c_re, ssm_c_im)
    merge = lambda a: a.reshape((DEPTH * SSM_GROUPS,) + a.shape[2:])
    ssm = (_ssm_toeplitz(merge(k_lag)), merge(ssm_ws), merge(ssm_wc), merge(ssm_al))
    d_all = ssm_d.reshape(DEPTH, gw, 1).astype(F32)
    glu_wt = jnp.swapaxes(glu_w, 1, 2).astype(BF16)
    glu_b_all = glu_b.reshape(DEPTH, gw, 1).astype(F32)
    na_bias = jax.vmap(_na_bias_table)(na_rpb)
    fg = final_g.reshape(1, D_MODEL).astype(F32)
    nxt = (norm_all, w_in.astype(BF16))

    def trunk(x):
        b = x.shape[0]
        x2 = x.reshape(b * SEQ, D_MODEL)
        z2, zt = _proj(x2, nxt=nxt, nxt_layer=0)
        for l in range(DEPTH):
            z3 = z2.reshape(b, SEQ, TOK_W)
            y_a = _pool(z3, inv_cnt, pool_bd, pool_sc, l)
            y_b = _dilated(z3, cos_t, sin_t, mask_t)
            y_ct = _ssm_core(zt, *ssm, l)
            y_d = _neighbourhood(z3, na_bias, l)
            flat = lambda a: a.reshape(b * SEQ, gw)
            mix = (flat(y_a), flat(y_b), flat(y_d), y_ct, zt, d_all, glu_wt, glu_b_all, w_abd, w_c, fg)
            if l + 1 < DEPTH:
                x2, z2, zt = _proj(x2, mix=mix, mix_layer=l, nxt=nxt, nxt_layer=l + 1)
            else:
                (x2,) = _proj(x2, mix=mix, mix_layer=l, final=True)
        return x2.reshape(b, SEQ, D_MODEL)

    return (trunk(x_prompt), trunk(x_sample))
```

```python
import functools
import math

import numpy as np
import jax
import jax.numpy as jnp
from jax import lax
from jax.experimental import pallas as pl
from jax.experimental.pallas import tpu as pltpu

D_MODEL = 1024
SEQ = 2048
DEPTH = 4
GROUP_W = 256
HEAD_DIM = 64
POOL_WINDOWS = (2, 4, 8, 16)
POOL_GROUP = 64
DILATED_PATTERNS = ((128, 1), (512, 4), (2048, 16))
SSM_CH = 16
SSM_GROUPS = 16
SSM_STATE = 64
GRID_W = 64
NA_ROWS = 8
NA_COLS = 16
ROPE_THETA = 10000.0
EPS = 1e-6
PROJ_W = 12 * GROUP_W
NEG = -1e30

F32 = jnp.float32
BF16 = jnp.bfloat16
HI = lax.Precision.HIGHEST

ROW_TILE = 512
POOL_PAD = 16
DIL_Q = 128
DIL_R = 64
DIL_BLOCKS = SEQ // DIL_Q
DIL_REGROUP = 4
STRIP = 32
ATT_UNROLL = 8
NA_UNROLL = 16
LOG2E = math.log2(math.e)
SSM_L = 128
SSM_NB_MAX = 16
SSM_PITCH = 24
TOK_W = 10 * GROUP_W
VMEM_LIMIT = 56 * 1024 * 1024


def _sigmoid(x):
    return 1.0 / (1.0 + jnp.exp(-x))


def _silu(x):
    return x * _sigmoid(x)


def _gelu_tanh(x):
    return 0.5 * x * (1.0 + jnp.tanh(math.sqrt(2.0 / math.pi) * (x + 0.044715 * (x * x * x))))


def _toeplitz(k, axis, a, b):
    k = jnp.moveaxis(k, axis, -1)
    w = a + b - 1
    lead = k.shape[:-1]
    kp = jnp.concatenate([k, jnp.zeros(lead + (1,), k.dtype)], axis=-1)
    t = jnp.broadcast_to(kp[..., None, :], lead + (a, w + 1)).reshape(lead + (a * (w + 1),))
    t = t[..., :a * w].reshape(lead + (a, w))[..., a - 1:a - 1 + b]
    return jnp.moveaxis(t, (-2, -1), (axis, axis + 1))


def _params(n_parallel=1, n_arbitrary=0):
    sem = ("parallel",) * n_parallel + ("arbitrary",) * n_arbitrary
    return pltpu.CompilerParams(dimension_semantics=sem, vmem_limit_bytes=VMEM_LIMIT)


def _proj_kernel(*refs, mix_out, proj_in, final):
    refs = list(refs)
    x_ref = refs.pop(0)
    if mix_out:
        (ya_ref, yb_ref, yd_ref, yct_ref, ztp_ref, d_ref, gwt_ref, gb_ref,
         wabd_ref, wc_ref, fg_ref) = refs[:11]
        del refs[:11]
    if proj_in:
        g_ref, w_ref = refs[:2]
        del refs[:2]
    if mix_out:
        xo_ref = refs.pop(0)
    if proj_in:
        z_ref, zt_ref = refs

    x = x_ref[...]
    if mix_out:
        u = ztp_ref[0:GROUP_W, :].astype(F32)
        y = u * d_ref[...] + yct_ref[...].astype(F32)
        g = _gelu_tanh(y)
        lin = jnp.dot(gwt_ref[...], g.astype(BF16), preferred_element_type=F32) + gb_ref[...]
        g = g * _sigmoid(lin)
        y_c = (g * _silu(ztp_ref[GROUP_W:2 * GROUP_W, :].astype(F32))).astype(BF16)
        y_abd = jnp.concatenate([ya_ref[...], yb_ref[...], yd_ref[...]], axis=1)
        acc = jnp.dot(y_abd, wabd_ref[...], preferred_element_type=F32)
        acc = acc + lax.dot_general(y_c, wc_ref[...], (((0,), (0,)), ((), ())),
                                    preferred_element_type=F32)
        x = x + acc
        if final:
            ms = jnp.mean(x * x, axis=-1, keepdims=True)
            x = x * lax.rsqrt(ms + EPS) * fg_ref[...]
        xo_ref[...] = x
    if proj_in:
        ms = jnp.mean(x * x, axis=-1, keepdims=True)
        h = (x * lax.rsqrt(ms + EPS) * g_ref[...]).astype(BF16)
        nch = 2 * GROUP_W
        for n, src in enumerate((0, 1, 2, 4, 5)):
            z_ref[:, n * nch:(n + 1) * nch] = jnp.dot(
                h, w_ref[:, src * nch:(src + 1) * nch], preferred_element_type=F32).astype(BF16)
        zt_ref[...] = lax.dot_general(w_ref[:, 3 * nch:4 * nch], h, (((0,), (1,)), ((), ())),
                                      preferred_element_type=F32).astype(BF16)


def _proj(x2, mix=None, mix_layer=None, nxt=None, nxt_layer=None, final=False):
    rows = x2.shape[0]
    per_seq = SEQ // ROW_TILE
    tile = lambda w: pl.BlockSpec((ROW_TILE, w), lambda i: (i, 0))
    chan = lambda c: pl.BlockSpec((None, c, ROW_TILE), lambda i: (i // per_seq, 0, i % per_seq))
    layer = lambda l, r, c: pl.BlockSpec((None, r, c), lambda i: (l, 0, 0))
    args, in_specs, out_specs, out_shape = [x2], [tile(D_MODEL)], [], []
    if mix is not None:
        args += list(mix)
        ml = mix_layer
        in_specs += [tile(GROUP_W), tile(GROUP_W), tile(GROUP_W), chan(GROUP_W), chan(2 * GROUP_W),
                     layer(ml, GROUP_W, 1), layer(ml, GROUP_W, GROUP_W), layer(ml, GROUP_W, 1),
                     layer(ml, 3 * GROUP_W, D_MODEL), layer(ml, GROUP_W, D_MODEL),
                     pl.BlockSpec((1, D_MODEL), lambda i: (0, 0))]
        out_specs.append(tile(D_MODEL))
        out_shape.append(jax.ShapeDtypeStruct((rows, D_MODEL), F32))
    if nxt is not None:
        args += list(nxt)
        nl = nxt_layer
        in_specs += [layer(nl, 1, D_MODEL), layer(nl, D_MODEL, PROJ_W)]
        out_specs += [tile(TOK_W), chan(2 * GROUP_W)]
        out_shape += [jax.ShapeDtypeStruct((rows, TOK_W), BF16),
                      jax.ShapeDtypeStruct((rows // SEQ, 2 * GROUP_W, SEQ), BF16)]
    return pl.pallas_call(
        functools.partial(_proj_kernel, mix_out=mix is not None, proj_in=nxt is not None, final=final),
        grid=(rows // ROW_TILE,),
        in_specs=in_specs, out_specs=out_specs, out_shape=out_shape,
        compiler_params=_params(),
        name="proj",
    )(*args)


def _pool_kernel(z_ref, inv_ref, w_ref, scale_ref, o_ref, ub, s2b, s4b, s8b):
    s = SEQ
    p = POOL_PAD
    n = s + p
    u = z_ref[:, 0:GROUP_W].astype(F32)
    zero_p = jnp.zeros((p, GROUP_W), F32)
    zero_8 = jnp.zeros((8, GROUP_W), F32)
    ub[0:p, :] = zero_p
    ub[p:p + s, :] = u
    ub[p + s:p + s + p, :] = zero_p
    half = GROUP_W // 2
    for buf in (s2b, s4b, s8b):
        buf[0:8, :] = zero_8[:, 0:buf.shape[1]]
        buf[s + 24:s + 32, :] = zero_8[:, 0:buf.shape[1]]
    s2b[8:8 + n, :] = ub[7:7 + n, :] + ub[8:8 + n, :]
    s4b[8:8 + n, :] = s2b[7:7 + n, :] + s2b[9:9 + n, :]
    s8b[8:8 + n, :] = s4b[6:6 + n, half:] + s4b[10:10 + n, half:]
    s16 = s8b[p - 4:p - 4 + s, :] + s8b[p + 4:p + 4 + s, :]
    lane = lax.broadcasted_iota(jnp.int32, (s, half), 1)
    win = jnp.concatenate([
        jnp.where(lane < POOL_GROUP, s2b[p:p + s, 0:half], s4b[p:p + s, 0:half]),
        jnp.where(lane < POOL_GROUP, s8b[p:p + s, :], s16)], axis=1)
    diff = (win * inv_ref[...] - u).astype(BF16)
    y = jnp.dot(diff, w_ref[...], preferred_element_type=F32) * scale_ref[...]
    gate = z_ref[:, GROUP_W:2 * GROUP_W].astype(F32)
    o_ref[...] = (y * _silu(gate)).astype(BF16)


def _pool_inv_count():
    t = np.arange(SEQ)
    cols = []
    for w in POOL_WINDOWS:
        lo = np.clip(t - w // 2, 0, SEQ)
        hi = np.clip(t + w // 2, 0, SEQ)
        cols.append(np.repeat((1.0 / (hi - lo))[:, None], POOL_GROUP, axis=1))
    return jnp.asarray(np.concatenate(cols, axis=1), F32)


def _pool(z3, inv_cnt, w_bd, scale, l):
    b = z3.shape[0]
    return pl.pallas_call(
        _pool_kernel,
        grid=(b,),
        in_specs=[
            pl.BlockSpec((None, SEQ, 2 * GROUP_W), lambda i: (i, 0, 0)),
            pl.BlockSpec((SEQ, GROUP_W), lambda i: (0, 0)),
            pl.BlockSpec((None, GROUP_W, GROUP_W), lambda i: (l, 0, 0)),
            pl.BlockSpec((None, 1, GROUP_W), lambda i: (l, 0, 0)),
        ],
        out_specs=pl.BlockSpec((None, SEQ, GROUP_W), lambda i: (i, 0, 0)),
        out_shape=jax.ShapeDtypeStruct((b, SEQ, GROUP_W), BF16),
        scratch_shapes=[pltpu.VMEM((SEQ + 2 * POOL_PAD, GROUP_W), F32) for _ in range(3)]
        + [pltpu.VMEM((SEQ + 2 * POOL_PAD, GROUP_W // 2), F32)],
        compiler_params=_params(),
        name="pool_mixer",
    )(z3, inv_cnt, w_bd, scale)


def _rope_tables():
    inv = ROPE_THETA ** (-np.arange(0, HEAD_DIM, 2, dtype=np.float64) / HEAD_DIM)
    ang = np.arange(SEQ, dtype=np.float64)[:, None] * inv[None, :]
    cos = np.concatenate([np.cos(ang), np.cos(ang)], axis=1)
    sin = np.concatenate([-np.sin(ang), np.sin(ang)], axis=1)
    return (jnp.asarray(np.tile(cos, (1, 2)), F32), jnp.asarray(np.tile(sin, (1, 2)), F32))


def _stack_heads(q):
    lane = lax.broadcasted_iota(jnp.int32, q.shape, 1)
    zero = jnp.zeros_like(q)
    return jnp.concatenate([jnp.where(lane < HEAD_DIM, q, zero),
                            jnp.where(lane < HEAD_DIM, zero, q)], axis=0)


def _unstack_heads(o, n):
    lane = lax.broadcasted_iota(jnp.int32, (n, 2 * HEAD_DIM), 1)
    return jnp.where(lane < HEAD_DIM, o[0:n], o[n:2 * n])


def _dilated_mask_table():
    q = np.arange(DIL_Q)[None, :, None]
    k = np.arange(2 * DIL_Q)[None, None, :]
    shift = (np.arange(3) * DIL_R)[:, None, None]
    return jnp.asarray(np.where(np.abs(k - q - shift) <= DIL_R, 0.0, NEG), F32)


def _dilated_kernel(qk_ref, vg_ref, cos_ref, sin_ref, mask_ref, o_ref, qf, kf, vf, q4, k4, v4,
                    acc_o, acc_l, s_scr, p_scr, m_scr):
    s = SEQ
    hd2 = 2 * HEAD_DIM
    lane2 = lax.broadcasted_iota(jnp.int32, (s, 2 * HEAD_DIM), 1)
    first_half = (lane2 % HEAD_DIM) < (HEAD_DIM // 2)
    cos = cos_ref[...]
    sin = sin_ref[...]
    for pair in range(2):
        lo = pair * 2 * HEAD_DIM
        for src, dst, scale in ((0, qf, HEAD_DIM ** -0.5 * LOG2E), (GROUP_W, kf, 1.0)):
            x = qk_ref[:, src + lo:src + lo + 2 * HEAD_DIM].astype(F32)
            swapped = jnp.where(first_half,
                                pltpu.roll(x, 2 * HEAD_DIM - HEAD_DIM // 2, 1),
                                pltpu.roll(x, HEAD_DIM // 2, 1))
            roped = x * cos + swapped * sin
            dst[pair] = roped if scale == 1.0 else roped * scale
        vf[pair] = vg_ref[:, lo:lo + 2 * HEAD_DIM].astype(F32)

    quarter = s // DIL_REGROUP
    for pair in range(2):
        for nat, grp in ((qf, q4), (kf, k4), (vf, v4)):
            for rho in range(DIL_REGROUP):
                grp[pair, rho * quarter:(rho + 1) * quarter, :] = nat[
                    pair, pl.ds(rho, quarter, stride=DIL_REGROUP), :]

    pattern_blocks = []
    for pi, (window, d) in enumerate(DILATED_PATTERNS):
        n_sub = s // d
        kw = min(2 * DIL_Q, n_sub)
        nb = n_sub // DIL_Q
        q_src, k_src, v_src = (qf, kf, vf) if d == 1 else (q4, k4, v4)

        def rows(rho, start, size, d=d):
            if d == 1:
                return pl.ds(pl.multiple_of(start, DIL_R), size)
            dd = d // DIL_REGROUP
            base = (rho % DIL_REGROUP) * quarter + rho // DIL_REGROUP + dd * start
            if dd == 1:
                return pl.ds(pl.multiple_of(base, DIL_R), size)
            return pl.ds(base, size, stride=dd)

        def block(trip, carry, d=d, n_sub=n_sub, kw=kw, nb=nb, pi=pi, rows=rows,
                  q_src=q_src, k_src=k_src, v_src=v_src):
            for sub in range(ATT_UNROLL):
                it = trip * ATT_UNROLL + sub
                rho = it // nb
                q0 = (it % nb) * DIL_Q
                ks = jnp.clip(q0 - DIL_R, 0, n_sub - kw)
                case = (q0 - ks) // DIL_R
                q_rows = rows(rho, q0, DIL_Q)
                k_rows = rows(rho, ks, kw)
                for pair in range(2):
                    slot = 2 * sub + pair
                    q = _stack_heads(q_src[pair, q_rows, :].astype(BF16))
                    k = k_src[pair, k_rows, :].astype(BF16)
                    v = _with_ones(v_src[pair, k_rows, :].astype(BF16))
                    s_scr[slot, :, 0:kw] = lax.dot_general(q, k, (((1,), (1,)), ((), ())),
                                                           preferred_element_type=F32)
                    _softmax_strips(
                        s_scr.at[slot], p_scr.at[slot],
                        lambda r0, r1, kw=kw, case=case: mask_ref[
                            case, r0 % DIL_Q:r0 % DIL_Q + (r1 - r0), 0:kw],
                        2 * DIL_Q, kw, m_scr.at[slot])
                    oa = jnp.dot(p_scr[slot, :, 0:kw], v, preferred_element_type=F32)
                    den = oa[:, hd2:2 * hd2]
                    o = oa[:, 0:hd2] / den
                    lse = m_scr[slot] + jnp.log2(den)
                    acc_o[pi, pair, q_rows, :] = _unstack_heads(o, DIL_Q)
                    acc_l[pi, pair, q_rows, :] = _unstack_heads(lse, DIL_Q)
            return carry

        assert d * nb == DIL_BLOCKS
        pattern_blocks.append(block)

    def trip_body(trip, carry):
        for blocks_of_pattern in pattern_blocks:
            blocks_of_pattern(trip, carry)
        return carry

    lax.fori_loop(0, DIL_BLOCKS // ATT_UNROLL, trip_body, 0)

    ch = 256
    for rho in range(DIL_REGROUP):
        for c in range(quarter // ch):
            g_rows = slice(rho * quarter + c * ch, rho * quarter + (c + 1) * ch)
            n_rows = pl.ds(rho + DIL_REGROUP * c * ch, ch, stride=DIL_REGROUP)
            for pair in range(2):
                l0, l1, l2 = acc_l[0, pair, n_rows, :], acc_l[1, pair, g_rows, :], acc_l[2, pair, g_rows, :]
                top = jnp.maximum(jnp.maximum(l0, l1), l2)
                w0, w1, w2 = jnp.exp2(l0 - top), jnp.exp2(l1 - top), jnp.exp2(l2 - top)
                acc_o[0, pair, n_rows, :] = (
                    (w0 * acc_o[0, pair, n_rows, :] + w1 * acc_o[1, pair, g_rows, :]
                     + w2 * acc_o[2, pair, g_rows, :]) / (w0 + w1 + w2))
    for c in range(s // ch):
        r = slice(c * ch, (c + 1) * ch)
        for pair in range(2):
            lo = pair * 2 * HEAD_DIM
            gate = vg_ref[r, GROUP_W + lo:GROUP_W + lo + 2 * HEAD_DIM].astype(F32)
            o_ref[r, lo:lo + 2 * HEAD_DIM] = (acc_o[0, pair, r, :] * _silu(gate)).astype(BF16)


def _dilated(z3, cos_t, sin_t, mask_t):
    b = z3.shape[0]
    return pl.pallas_call(
        _dilated_kernel,
        grid=(b,),
        in_specs=[
            pl.BlockSpec((None, SEQ, 2 * GROUP_W), lambda i: (i, 0, 1)),
            pl.BlockSpec((None, SEQ, 2 * GROUP_W), lambda i: (i, 0, 2)),
            pl.BlockSpec((SEQ, 2 * HEAD_DIM), lambda i: (0, 0)),
            pl.BlockSpec((SEQ, 2 * HEAD_DIM), lambda i: (0, 0)),
            pl.BlockSpec(mask_t.shape, lambda i: (0, 0, 0)),
        ],
        out_specs=pl.BlockSpec((None, SEQ, GROUP_W), lambda i: (i, 0, 0)),
        out_shape=jax.ShapeDtypeStruct((b, SEQ, GROUP_W), BF16),
        scratch_shapes=[pltpu.VMEM((2, SEQ, 2 * HEAD_DIM), F32) for _ in range(6)]
        + [pltpu.VMEM((len(DILATED_PATTERNS), 2, SEQ, 2 * HEAD_DIM), F32) for _ in range(2)]
        + [pltpu.VMEM((2 * ATT_UNROLL, 2 * DIL_Q, 2 * DIL_Q), F32),
           pltpu.VMEM((2 * ATT_UNROLL, 2 * DIL_Q, 2 * DIL_Q), BF16),
           pltpu.VMEM((2 * ATT_UNROLL, 2 * DIL_Q, 2 * HEAD_DIM), F32)],
        compiler_params=_params(),
        name="dilated_attention",
    )(z3, z3, cos_t, sin_t, mask_t)


def _na_bias_table(rpb):
    c = np.arange(GRID_W)
    cs = np.clip(c - NA_COLS // 2, 0, GRID_W - NA_COLS)
    col_ok = (c[None, :] >= cs[:, None]) & (c[None, :] < cs[:, None] + NA_COLS)
    pad = GRID_W - NA_COLS
    tab = jnp.pad(rpb.astype(F32), ((0, 0), (0, 0), (pad, pad)))
    tab = _toeplitz(tab, 2, GRID_W, GRID_W)
    tab = jnp.where(col_ok[None, None, :, :], tab * LOG2E, NEG)
    h, nd = rpb.shape[0], 2 * NA_ROWS - 1
    tab = tab.reshape(h // 2, 2, nd, GRID_W, GRID_W).transpose(0, 2, 1, 3, 4)
    tab = tab.reshape(h // 2, nd, 2 * GRID_W, GRID_W)
    return jnp.concatenate([tab[:, :-1], tab[:, 1:]], axis=-1)


def _softmax_strips(s_scr, p_scr, bias, n_rows, n_keys, m_scr=None):
    for r0 in range(0, n_rows, STRIP):
        t = s_scr[r0:r0 + STRIP, 0:n_keys] + bias(r0, r0 + STRIP)
        m = jnp.max(t, axis=1, keepdims=True)
        p_scr[r0:r0 + STRIP, 0:n_keys] = jnp.exp2(t - m).astype(BF16)
        if m_scr is not None:
            m_scr[r0:r0 + STRIP, :] = jnp.broadcast_to(m, (STRIP, 2 * HEAD_DIM))


def _with_ones(v):
    return jnp.concatenate([v, jnp.ones_like(v)], axis=1)


def _na_kernel(qk_ref, vg_ref, bias_ref, o_ref, s_scr, p_scr, bias_scr):
    n_rows = SEQ // GRID_W
    nk = NA_ROWS * GRID_W
    hd2 = 2 * HEAD_DIM

    @pl.when(pl.program_id(0) == 0)
    def _():
        for pair in range(2):
            for off in range(NA_ROWS):
                for j in range(NA_ROWS // 2):
                    bias_scr[pair, off, :, 2 * j * GRID_W:(2 * j + 2) * GRID_W] = bias_ref[
                        pair, 2 * j - off + NA_ROWS - 1]

    def rows_trip(trip, carry):
        for sub in range(NA_UNROLL):
            r = trip * NA_UNROLL + sub
            rs = jnp.clip(r - NA_ROWS // 2, 0, n_rows - NA_ROWS)
            off = r - rs
            q_rows = pl.ds(pl.multiple_of(r * GRID_W, GRID_W), GRID_W)
            k_rows = pl.ds(pl.multiple_of(rs * GRID_W, GRID_W), nk)
            for pair in range(2):
                slot = 2 * sub + pair
                lo = pair * hd2
                q = qk_ref[q_rows, lo:lo + hd2].astype(F32) * (HEAD_DIM ** -0.5 * LOG2E)
                q = _stack_heads(q.astype(BF16))
                k = qk_ref[k_rows, GROUP_W + lo:GROUP_W + lo + hd2]
                v = _with_ones(vg_ref[k_rows, lo:lo + hd2])
                s_scr[slot] = lax.dot_general(q, k, (((1,), (1,)), ((), ())), preferred_element_type=F32)
                _softmax_strips(s_scr.at[slot], p_scr.at[slot],
                                lambda r0, r1, pair=pair, off=off: bias_scr[pair, off, r0:r1, :],
                                2 * GRID_W, nk)
                oa = jnp.dot(p_scr[slot], v, preferred_element_type=F32)
                o = oa[:, 0:hd2] / oa[:, hd2:2 * hd2]
                gate = vg_ref[q_rows, GROUP_W + lo:GROUP_W + lo + hd2].astype(F32)
                o_ref[q_rows, lo:lo + hd2] = (_unstack_heads(o, GRID_W) * _silu(gate)).astype(BF16)
        return carry

    lax.fori_loop(0, n_rows // NA_UNROLL, rows_trip, 0)


def _neighbourhood(z3, bias_tab, l):
    b = z3.shape[0]
    return pl.pallas_call(
        _na_kernel,
        grid=(b,),
        in_specs=[
            pl.BlockSpec((None, SEQ, 2 * GROUP_W), lambda i: (i, 0, 3)),
            pl.BlockSpec((None, SEQ, 2 * GROUP_W), lambda i: (i, 0, 4)),
            pl.BlockSpec((None,) + bias_tab.shape[1:], lambda i: (l, 0, 0, 0, 0)),
        ],
        out_specs=pl.BlockSpec((None, SEQ, GROUP_W), lambda i: (i, 0, 0)),
        out_shape=jax.ShapeDtypeStruct((b, SEQ, GROUP_W), BF16),
        scratch_shapes=[pltpu.VMEM((2 * NA_UNROLL, 2 * GRID_W, NA_ROWS * GRID_W), F32),
                        pltpu.VMEM((2 * NA_UNROLL, 2 * GRID_W, NA_ROWS * GRID_W), BF16),
                        pltpu.VMEM((2, NA_ROWS, 2 * GRID_W, NA_ROWS * GRID_W), F32)],
        compiler_params=_params(0, 1),
        name="neighbourhood_attention",
    )(z3, z3, bias_tab)


def _ssm_weights(a_re, a_im, log_dt, b_re, b_im, c_re, c_im):
    L, G, P, C = SSM_L, SSM_GROUPS, SSM_STATE, SSM_CH
    a_re, a_im, log_dt = a_re.astype(F32), a_im.astype(F32), log_dt.astype(F32)
    dt = jnp.exp(log_dt)[..., None]
    ks = jnp.arange(L + 1, dtype=F32)
    mag = jnp.exp((a_re * dt)[..., None] * ks)
    ang = (a_im * dt)[..., None] * ks
    pr, pim = mag * jnp.cos(ang), mag * jnp.sin(ang)
    abr, abi = pr[..., 1], pim[..., 1]
    den = a_re * a_re + a_im * a_im
    gr = ((abr - 1.0) * a_re + abi * a_im) / den
    gi = (abi * a_re - (abr - 1.0) * a_im) / den
    br, bi = b_re.astype(F32), b_im.astype(F32)
    bbr = gr[..., None] * br - gi[..., None] * bi
    bbi = gr[..., None] * bi + gi[..., None] * br
    cr, ci = c_re.astype(F32), c_im.astype(F32)
    cb_r = cr[:, :, None] * bbr.transpose(0, 1, 3, 2)[:, :, :, None] \
        - ci[:, :, None] * bbi.transpose(0, 1, 3, 2)[:, :, :, None]
    cb_i = cr[:, :, None] * bbi.transpose(0, 1, 3, 2)[:, :, :, None] \
        + ci[:, :, None] * bbr.transpose(0, 1, 3, 2)[:, :, :, None]
    kern = (jnp.einsum('dgxcp,dgpk->dgxck', cb_r, pr, precision=HI)
            - jnp.einsum('dgxcp,dgpk->dgxck', cb_i, pim, precision=HI))
    kf, kb = kern[0], kern[1]
    k_lag = jnp.concatenate([kb[..., 1:L][..., ::-1], (kf[..., 0] + kb[..., 0])[..., None],
                             kf[..., 1:L], jnp.zeros(kf.shape[:-1] + (1,), F32)], axis=-1)
    k_lag = k_lag.reshape(G, C * C, 2 * L)

    def state_in(d, idx):
        a_r = pr[d][:, :, idx].transpose(0, 2, 1)[:, None]
        a_i = pim[d][:, :, idx].transpose(0, 2, 1)[:, None]
        b_r = bbr[d].transpose(0, 2, 1)[:, :, None]
        b_i = bbi[d].transpose(0, 2, 1)[:, :, None]
        return a_r * b_r - a_i * b_i, a_r * b_i + a_i * b_r

    sf_r, sf_i = state_in(0, slice(L - 1, None, -1))
    sb_r, sb_i = state_in(1, slice(0, L))
    w_s = jnp.concatenate([sf_r, sb_r, sf_i, sb_i], axis=-1).reshape(G, C * L, 4 * P)

    def state_out(d, idx):
        c_r = cr[d].transpose(0, 2, 1)[..., None]
        c_i = ci[d].transpose(0, 2, 1)[..., None]
        a_r = pr[d][:, :, idx][:, :, None, :]
        a_i = pim[d][:, :, idx][:, :, None, :]
        return c_r * a_r - c_i * a_i, c_r * a_i + c_i * a_r

    ff_r, ff_i = state_out(0, slice(1, L + 1))
    fb_r, fb_i = state_out(1, slice(L, 0, -1))
    w_c = jnp.concatenate([ff_r, fb_r, -ff_i, -fb_i], axis=1).reshape(G, 4 * P, C * L)
    a_l = jnp.concatenate([pr[0, ..., L], pr[1, ..., L], pim[0, ..., L], pim[1, ..., L]], axis=-1)
    a_l = jnp.broadcast_to(a_l[:, None, :], (G, SSM_NB_MAX, 4 * P))
    return k_lag, w_s.astype(BF16), w_c.astype(BF16), a_l


def _ssm_toeplitz_kernel(k_ref, t_ref):
    L, C = SSM_L, SSM_CH

    def build(cp, carry):
        for c in range(C):
            lag = jnp.broadcast_to(k_ref[pl.ds(cp * C + c, 1), :], (L, 2 * L))
            tile = pltpu.roll(lag, L + 1, 1, stride=1, stride_axis=0)[:, 0:L]
            t_ref[pl.ds(pl.multiple_of(cp * L, L), L), c * L:(c + 1) * L] = tile.astype(BF16)
        return carry

    lax.fori_loop(0, C, build, 0)


def _ssm_toeplitz(k_lag):
    n, C, L = k_lag.shape[0], SSM_CH, SSM_L
    return pl.pallas_call(
        _ssm_toeplitz_kernel,
        grid=(n,),
        in_specs=[pl.BlockSpec((None, C * C, 2 * L), lambda g: (g, 0, 0))],
        out_specs=pl.BlockSpec((None, C * L, C * L), lambda g: (g, 0, 0)),
        out_shape=jax.ShapeDtypeStruct((n, C * L, C * L), BF16),
        compiler_params=_params(),
        name="ssm_toeplitz",
    )(k_lag)


def _ssm_kernel(u_ref, t_ref, ws_ref, wc_ref, al_ref, y_ref, x_scr, y_scr,
                s_re, s_im, ha_re, ha_im, hb_re, hb_im, *, nb):
    L, C, P = SSM_L, SSM_CH, SSM_STATE
    nc = SEQ // L
    m = nb * nc
    half = 2 * P

    for b in range(nb):
        ub = u_ref[b].astype(F32)
        for j in range(nc):
            r0 = (b * nc + j) * SSM_PITCH
            x_scr[r0:r0 + C, :] = ub[:, j * L:(j + 1) * L]
    lhs = jnp.concatenate([x_scr[pl.ds(c, m, stride=SSM_PITCH), :].astype(BF16) for c in range(C)],
                          axis=1)

    s = jnp.dot(lhs, ws_ref[...], preferred_element_type=F32)
    s_re[...] = s[:, 0:half]
    s_im[...] = s[:, half:2 * half]
    a_r = al_ref[0:nb, 0:half]
    a_i = al_ref[0:nb, half:2 * half]
    fwd = lax.broadcasted_iota(jnp.int32, (nb, half), 1) < P
    h_r = jnp.zeros((nb, half), F32)
    h_i = jnp.zeros((nb, half), F32)
    for k in range(nc):
        rf = pl.ds(k, nb, stride=nc)
        rb = pl.ds(nc - 1 - k, nb, stride=nc)
        ha_re[rf, :] = h_r
        ha_im[rf, :] = h_i
        hb_re[rb, :] = h_r
        hb_im[rb, :] = h_i
        s_r = jnp.where(fwd, s_re[rf, :], s_re[rb, :])
        s_i = jnp.where(fwd, s_im[rf, :], s_im[rb, :])
        h_r, h_i = a_r * h_r - a_i * h_i + s_r, a_r * h_i + a_i * h_r + s_i
    fwd_m = lax.broadcasted_iota(jnp.int32, (m, half), 1) < P
    h_all = jnp.concatenate([jnp.where(fwd_m, ha_re[...], hb_re[...]),
                             jnp.where(fwd_m, ha_im[...], hb_im[...])], axis=1).astype(BF16)
    y = jnp.dot(lhs, t_ref[...], preferred_element_type=F32)
    y = y + jnp.dot(h_all, wc_ref[...], preferred_element_type=F32)
    for c in range(C):
        y_scr[pl.ds(c, m, stride=SSM_PITCH), :] = y[:, c * L:(c + 1) * L]
    for b in range(nb):
        y_ref[b] = jnp.concatenate(
            [y_scr[(b * nc + j) * SSM_PITCH:(b * nc + j) * SSM_PITCH + C, :] for j in range(nc)],
            axis=1).astype(BF16)


def _ssm_core(zt, w_t, w_s, w_c, a_l, l):
    b = zt.shape[0]
    L, G, C, P = SSM_L, SSM_GROUPS, SSM_CH, SSM_STATE
    nb = min(b, SSM_NB_MAX)
    m = nb * (SEQ // L)
    weight = lambda r, c: pl.BlockSpec((None, r, c), lambda g, i: (l * G + g, 0, 0))
    return pl.pallas_call(
        functools.partial(_ssm_kernel, nb=nb),
        grid=(G, b // nb),
        in_specs=[pl.BlockSpec((nb, C, SEQ), lambda g, i: (i, g, 0)),
                  weight(C * L, C * L), weight(C * L, 4 * P), weight(4 * P, C * L),
                  weight(SSM_NB_MAX, 4 * P)],
        out_specs=pl.BlockSpec((nb, C, SEQ), lambda g, i: (i, g, 0)),
        out_shape=jax.ShapeDtypeStruct((b, G * C, SEQ), BF16),
        scratch_shapes=[pltpu.VMEM((m * SSM_PITCH, L), F32), pltpu.VMEM((m * SSM_PITCH, L), F32)]
        + [pltpu.VMEM((m, 2 * P), F32) for _ in range(6)],
        compiler_params=_params(2),
        name="ssm_core",
    )(zt, w_t, w_s, w_c, a_l)


def _block_diag(w):
    n, k = w.shape[0], w.shape[1]
    out = jnp.zeros((n * k, n * k), w.dtype)
    for i in range(n):
        out = lax.dynamic_update_slice(out, w[i], (i * k, i * k))
    return out


def kernel(x_prompt, x_sample, norm_g, w_in, w_out, pool_w, pool_scale, ssm_a_re, ssm_a_im,
           ssm_log_dt, ssm_b_re, ssm_b_im, ssm_c_re, ssm_c_im, ssm_d, glu_w, glu_b, na_rpb, final_g):
    inv_cnt = _pool_inv_count()
    cos_t, sin_t = _rope_tables()
    mask_t = _dilated_mask_table()
    gw = GROUP_W
    norm_all = norm_g.reshape(DEPTH, 1, D_MODEL).astype(F32)
    w_abd = jnp.concatenate([w_out[:, :2 * gw], w_out[:, 3 * gw:]], axis=1).astype(BF16)
    w_c = w_out[:, 2 * gw:3 * gw].astype(BF16)
    pool_bd = jax.vmap(_block_diag)(pool_w).astype(BF16)
    pool_sc = pool_scale.reshape(DEPTH, 1, gw).astype(F32)
    k_lag, ssm_ws, ssm_wc, ssm_al = jax.vmap(_ssm_weights)(
        ssm_a_re, ssm_a_im, ssm_log_dt, ssm_b_re, ssm_b_im, ssm_c_re, ssm_c_im)
    merge = lambda a: a.reshape((DEPTH * SSM_GROUPS,) + a.shape[2:])
    ssm = (_ssm_toeplitz(merge(k_lag)), merge(ssm_ws), merge(ssm_wc), merge(ssm_al))
    d_all = ssm_d.reshape(DEPTH, gw, 1).astype(F32)
    glu_wt = jnp.swapaxes(glu_w, 1, 2).astype(BF16)
    glu_b_all = glu_b.reshape(DEPTH, gw, 1).astype(F32)
    na_bias = jax.vmap(_na_bias_table)(na_rpb)
    fg = final_g.reshape(1, D_MODEL).astype(F32)
    nxt = (norm_all, w_in.astype(BF16))

    def trunk(x):
        b = x.shape[0]
        x2 = x.reshape(b * SEQ, D_MODEL)
        z2, zt = _proj(x2, nxt=nxt, nxt_layer=0)
        for l in range(DEPTH):
            z3 = z2.reshape(b, SEQ, TOK_W)
            y_a = _pool(z3, inv_cnt, pool_bd, pool_sc, l)
            y_b = _dilated(z3, cos_t, sin_t, mask_t)
            y_ct = _ssm_core(zt, *ssm, l)
            y_d = _neighbourhood(z3, na_bias, l)
            flat = lambda a: a.reshape(b * SEQ, gw)
            mix = (flat(y_a), flat(y_b), flat(y_d), y_ct, zt, d_all, glu_wt, glu_b_all, w_abd, w_c, fg)
            if l + 1 < DEPTH:
                x2, z2, zt = _proj(x2, mix=mix, mix_layer=l, nxt=nxt, nxt_layer=l + 1)
            else:
                (x2,) = _proj(x2, mix=mix, mix_layer=l, final=True)
        return x2.reshape(b, SEQ, D_MODEL)

    return (trunk(x_prompt), trunk(x_sample))
```

```python
import functools
import math

import numpy as np
import jax
import jax.numpy as jnp
from jax import lax
from jax.experimental import pallas as pl
from jax.experimental.pallas import tpu as pltpu

D_MODEL = 1024
SEQ = 2048
DEPTH = 4
GROUP_W = 256
HEAD_DIM = 64
POOL_WINDOWS = (2, 4, 8, 16)
POOL_GROUP = 64
DILATED_PATTERNS = ((128, 1), (512, 4), (2048, 16))
SSM_CH = 16
SSM_GROUPS = 16
SSM_STATE = 64
GRID_W = 64
NA_ROWS = 8
NA_COLS = 16
ROPE_THETA = 10000.0
EPS = 1e-6
PROJ_W = 12 * GROUP_W
NEG = -1e30

F32 = jnp.float32
BF16 = jnp.bfloat16
HI = lax.Precision.HIGHEST

ROW_TILE = 1024
POOL_PAD = 16
DIL_Q = 128
DIL_R = 64
DIL_BLOCKS = SEQ // DIL_Q
DIL_REGROUP = 4
STRIP = 32
ATT_UNROLL = 8
NA_UNROLL = 16
LOG2E = math.log2(math.e)
SSM_L = 128
SSM_NB_MAX = 16
SSM_PITCH = 24
TOK_W = 10 * GROUP_W
VMEM_LIMIT = 56 * 1024 * 1024


def _sigmoid(x):
    return 1.0 / (1.0 + jnp.exp(-x))


def _silu(x):
    return x * _sigmoid(x)


def _gelu_tanh(x):
    return 0.5 * x * (1.0 + jnp.tanh(math.sqrt(2.0 / math.pi) * (x + 0.044715 * (x * x * x))))


def _toeplitz(k, axis, a, b):
    k = jnp.moveaxis(k, axis, -1)
    w = a + b - 1
    lead = k.shape[:-1]
    kp = jnp.concatenate([k, jnp.zeros(lead + (1,), k.dtype)], axis=-1)
    t = jnp.broadcast_to(kp[..., None, :], lead + (a, w + 1)).reshape(lead + (a * (w + 1),))
    t = t[..., :a * w].reshape(lead + (a, w))[..., a - 1:a - 1 + b]
    return jnp.moveaxis(t, (-2, -1), (axis, axis + 1))


def _params(n_parallel=1, n_arbitrary=0):
    sem = ("parallel",) * n_parallel + ("arbitrary",) * n_arbitrary
    return pltpu.CompilerParams(dimension_semantics=sem, vmem_limit_bytes=VMEM_LIMIT)


def _proj_kernel(*refs, mix_out, proj_in, final):
    refs = list(refs)
    x_ref = refs.pop(0)
    if mix_out:
        (ya_ref, yb_ref, yd_ref, yct_ref, ztp_ref, d_ref, gwt_ref, gb_ref,
         wabd_ref, wc_ref, fg_ref) = refs[:11]
        del refs[:11]
    if proj_in:
        g_ref, w_ref = refs[:2]
        del refs[:2]
    if mix_out:
        xo_ref = refs.pop(0)
    if proj_in:
        z_ref, zt_ref = refs

    x = x_ref[...]
    if mix_out:
        u = ztp_ref[0:GROUP_W, :].astype(F32)
        y = u * d_ref[...] + yct_ref[...]
        g = _gelu_tanh(y)
        lin = jnp.dot(gwt_ref[...], g.astype(BF16), preferred_element_type=F32) + gb_ref[...]
        g = g * _sigmoid(lin)
        y_c = (g * _silu(ztp_ref[GROUP_W:2 * GROUP_W, :].astype(F32))).astype(BF16)
        y_abd = jnp.concatenate([ya_ref[...], yb_ref[...], yd_ref[...]], axis=1)
        acc = jnp.dot(y_abd, wabd_ref[...], preferred_element_type=F32)
        acc = acc + lax.dot_general(y_c, wc_ref[...], (((0,), (0,)), ((), ())),
                                    preferred_element_type=F32)
        x = x + acc
        if final:
            ms = jnp.mean(x * x, axis=-1, keepdims=True)
            x = x * lax.rsqrt(ms + EPS) * fg_ref[...]
        xo_ref[...] = x
    if proj_in:
        ms = jnp.mean(x * x, axis=-1, keepdims=True)
        h = (x * lax.rsqrt(ms + EPS) * g_ref[...]).astype(BF16)
        nch = 2 * GROUP_W
        for n, src in enumerate((0, 1, 2, 4, 5)):
            z_ref[:, n * nch:(n + 1) * nch] = jnp.dot(
                h, w_ref[:, src * nch:(src + 1) * nch], preferred_element_type=F32).astype(BF16)
        zt_ref[...] = lax.dot_general(w_ref[:, 3 * nch:4 * nch], h, (((0,), (1,)), ((), ())),
                                      preferred_element_type=F32).astype(BF16)


def _proj(x2, mix=None, mix_layer=None, nxt=None, nxt_layer=None, final=False):
    rows = x2.shape[0]
    per_seq = SEQ // ROW_TILE
    tile = lambda w: pl.BlockSpec((ROW_TILE, w), lambda i: (i, 0))
    chan = lambda c: pl.BlockSpec((None, c, ROW_TILE), lambda i: (i // per_seq, 0, i % per_seq))
    layer = lambda l, r, c: pl.BlockSpec((None, r, c), lambda i: (l, 0, 0), pipeline_mode=pl.Buffered(1))
    args, in_specs, out_specs, out_shape = [x2], [tile(D_MODEL)], [], []
    if mix is not None:
        args += list(mix)
        ml = mix_layer
        in_specs += [tile(GROUP_W), tile(GROUP_W), tile(GROUP_W), chan(GROUP_W), chan(2 * GROUP_W),
                     layer(ml, GROUP_W, 1), layer(ml, GROUP_W, GROUP_W), layer(ml, GROUP_W, 1),
                     layer(ml, 3 * GROUP_W, D_MODEL), layer(ml, GROUP_W, D_MODEL),
                     pl.BlockSpec((1, D_MODEL), lambda i: (0, 0))]
        out_specs.append(tile(D_MODEL))
        out_shape.append(jax.ShapeDtypeStruct((rows, D_MODEL), F32))
    if nxt is not None:
        args += list(nxt)
        nl = nxt_layer
        in_specs += [layer(nl, 1, D_MODEL), layer(nl, D_MODEL, PROJ_W)]
        out_specs += [tile(TOK_W), chan(2 * GROUP_W)]
        out_shape += [jax.ShapeDtypeStruct((rows, TOK_W), BF16),
                      jax.ShapeDtypeStruct((rows // SEQ, 2 * GROUP_W, SEQ), BF16)]
    return pl.pallas_call(
        functools.partial(_proj_kernel, mix_out=mix is not None, proj_in=nxt is not None, final=final),
        grid=(rows // ROW_TILE,),
        in_specs=in_specs, out_specs=out_specs, out_shape=out_shape,
        compiler_params=_params(),
        name="proj",
    )(*args)


def _pool_kernel(z_ref, inv_ref, w_ref, scale_ref, o_ref, ub, s2b, s4b, s8b):
    s = SEQ
    p = POOL_PAD
    n = s + p
    u = z_ref[:, 0:GROUP_W].astype(F32)
    zero_p = jnp.zeros((p, GROUP_W), F32)
    zero_8 = jnp.zeros((8, GROUP_W), F32)
    ub[0:p, :] = zero_p
    ub[p:p + s, :] = u
    ub[p + s:p + s + p, :] = zero_p
    half = GROUP_W // 2
    for buf in (s2b, s4b, s8b):
        buf[0:8, :] = zero_8[:, 0:buf.shape[1]]
        buf[s + 24:s + 32, :] = zero_8[:, 0:buf.shape[1]]
    s2b[8:8 + n, :] = ub[7:7 + n, :] + ub[8:8 + n, :]
    s4b[8:8 + n, :] = s2b[7:7 + n, :] + s2b[9:9 + n, :]
    s8b[8:8 + n, :] = s4b[6:6 + n, half:] + s4b[10:10 + n, half:]
    s16 = s8b[p - 4:p - 4 + s, :] + s8b[p + 4:p + 4 + s, :]
    lane = lax.broadcasted_iota(jnp.int32, (s, half), 1)
    win = jnp.concatenate([
        jnp.where(lane < POOL_GROUP, s2b[p:p + s, 0:half], s4b[p:p + s, 0:half]),
        jnp.where(lane < POOL_GROUP, s8b[p:p + s, :], s16)], axis=1)
    diff = (win * inv_ref[...] - u).astype(BF16)
    y = jnp.dot(diff, w_ref[...], preferred_element_type=F32) * scale_ref[...]
    gate = z_ref[:, GROUP_W:2 * GROUP_W].astype(F32)
    o_ref[...] = (y * _silu(gate)).astype(BF16)


def _pool_inv_count():
    t = np.arange(SEQ)
    cols = []
    for w in POOL_WINDOWS:
        lo = np.clip(t - w // 2, 0, SEQ)
        hi = np.clip(t + w // 2, 0, SEQ)
        cols.append(np.repeat((1.0 / (hi - lo))[:, None], POOL_GROUP, axis=1))
    return jnp.asarray(np.concatenate(cols, axis=1), F32)


def _pool(z3, inv_cnt, w_bd, scale, l):
    b = z3.shape[0]
    return pl.pallas_call(
        _pool_kernel,
        grid=(b,),
        in_specs=[
            pl.BlockSpec((None, SEQ, 2 * GROUP_W), lambda i: (i, 0, 0)),
            pl.BlockSpec((SEQ, GROUP_W), lambda i: (0, 0)),
            pl.BlockSpec((None, GROUP_W, GROUP_W), lambda i: (l, 0, 0)),
            pl.BlockSpec((None, 1, GROUP_W), lambda i: (l, 0, 0)),
        ],
        out_specs=pl.BlockSpec((None, SEQ, GROUP_W), lambda i: (i, 0, 0)),
        out_shape=jax.ShapeDtypeStruct((b, SEQ, GROUP_W), BF16),
        scratch_shapes=[pltpu.VMEM((SEQ + 2 * POOL_PAD, GROUP_W), F32) for _ in range(3)]
        + [pltpu.VMEM((SEQ + 2 * POOL_PAD, GROUP_W // 2), F32)],
        compiler_params=_params(),
        name="pool_mixer",
    )(z3, inv_cnt, w_bd, scale)


def _rope_tables():
    inv = ROPE_THETA ** (-np.arange(0, HEAD_DIM, 2, dtype=np.float64) / HEAD_DIM)
    ang = np.arange(SEQ, dtype=np.float64)[:, None] * inv[None, :]
    cos = np.concatenate([np.cos(ang), np.cos(ang)], axis=1)
    sin = np.concatenate([-np.sin(ang), np.sin(ang)], axis=1)
    return (jnp.asarray(np.tile(cos, (1, 2)), F32), jnp.asarray(np.tile(sin, (1, 2)), F32))


def _stack_heads(q):
    lane = lax.broadcasted_iota(jnp.int32, q.shape, 1)
    zero = jnp.zeros_like(q)
    return jnp.concatenate([jnp.where(lane < HEAD_DIM, q, zero),
                            jnp.where(lane < HEAD_DIM, zero, q)], axis=0)


def _unstack_heads(o, n):
    lane = lax.broadcasted_iota(jnp.int32, (n, 2 * HEAD_DIM), 1)
    return jnp.where(lane < HEAD_DIM, o[0:n], o[n:2 * n])


def _dilated_mask_table():
    q = np.arange(DIL_Q)[None, :, None]
    k = np.arange(2 * DIL_Q)[None, None, :]
    shift = (np.arange(3) * DIL_R)[:, None, None]
    return jnp.asarray(np.where(np.abs(k - q - shift) <= DIL_R, 0.0, NEG), F32)


def _dilated_kernel(qk_ref, vg_ref, cos_ref, sin_ref, mask_ref, o_ref, qf, kf, vf, q4, k4, v4,
                    acc_o, acc_l, s_scr, p_scr, m_scr):
    s = SEQ
    hd2 = 2 * HEAD_DIM
    lane2 = lax.broadcasted_iota(jnp.int32, (s, 2 * HEAD_DIM), 1)
    first_half = (lane2 % HEAD_DIM) < (HEAD_DIM // 2)
    cos = cos_ref[...]
    sin = sin_ref[...]
    for pair in range(2):
        lo = pair * 2 * HEAD_DIM
        for src, dst, scale in ((0, qf, HEAD_DIM ** -0.5 * LOG2E), (GROUP_W, kf, 1.0)):
            x = qk_ref[:, src + lo:src + lo + 2 * HEAD_DIM].astype(F32)
            swapped = jnp.where(first_half,
                                pltpu.roll(x, 2 * HEAD_DIM - HEAD_DIM // 2, 1),
                                pltpu.roll(x, HEAD_DIM // 2, 1))
            roped = x * cos + swapped * sin
            dst[pair] = roped if scale == 1.0 else roped * scale
        vf[pair] = vg_ref[:, lo:lo + 2 * HEAD_DIM].astype(F32)

    quarter = s // DIL_REGROUP
    for pair in range(2):
        for nat, grp in ((qf, q4), (kf, k4), (vf, v4)):
            for rho in range(DIL_REGROUP):
                grp[pair, rho * quarter:(rho + 1) * quarter, :] = nat[
                    pair, pl.ds(rho, quarter, stride=DIL_REGROUP), :]

    pattern_blocks = []
    for pi, (window, d) in enumerate(DILATED_PATTERNS):
        n_sub = s // d
        kw = min(2 * DIL_Q, n_sub)
        nb = n_sub // DIL_Q
        q_src, k_src, v_src = (qf, kf, vf) if d == 1 else (q4, k4, v4)

        def rows(rho, start, size, d=d):
            if d == 1:
                return pl.ds(pl.multiple_of(start, DIL_R), size)
            dd = d // DIL_REGROUP
            base = (rho % DIL_REGROUP) * quarter + rho // DIL_REGROUP + dd * start
            if dd == 1:
                return pl.ds(pl.multiple_of(base, DIL_R), size)
            return pl.ds(base, size, stride=dd)

        def block(trip, carry, d=d, n_sub=n_sub, kw=kw, nb=nb, pi=pi, rows=rows,
                  q_src=q_src, k_src=k_src, v_src=v_src):
            for sub in range(ATT_UNROLL):
                it = trip * ATT_UNROLL + sub
                rho = it // nb
                q0 = (it % nb) * DIL_Q
                ks = jnp.clip(q0 - DIL_R, 0, n_sub - kw)
                case = (q0 - ks) // DIL_R
                q_rows = rows(rho, q0, DIL_Q)
                k_rows = rows(rho, ks, kw)
                for pair in range(2):
                    slot = 2 * sub + pair
                    q = _stack_heads(q_src[pair, q_rows, :].astype(BF16))
                    k = k_src[pair, k_rows, :].astype(BF16)
                    v = _with_ones(v_src[pair, k_rows, :].astype(BF16))
                    s_scr[slot, :, 0:kw] = lax.dot_general(q, k, (((1,), (1,)), ((), ())),
                                                           preferred_element_type=F32)
                    _softmax_strips(
                        s_scr.at[slot], p_scr.at[slot],
                        lambda r0, r1, kw=kw, case=case: mask_ref[
                            case, r0 % DIL_Q:r0 % DIL_Q + (r1 - r0), 0:kw],
                        2 * DIL_Q, kw, m_scr.at[slot])
                    oa = jnp.dot(p_scr[slot, :, 0:kw], v, preferred_element_type=F32)
                    den = oa[:, hd2:2 * hd2]
                    o = oa[:, 0:hd2] / den
                    lse = m_scr[slot] + jnp.log2(den)
                    acc_o[pi, pair, q_rows, :] = _unstack_heads(o, DIL_Q)
                    acc_l[pi, pair, q_rows, :] = _unstack_heads(lse, DIL_Q)
            return carry

        assert d * nb == DIL_BLOCKS
        pattern_blocks.append(block)

    def trip_body(trip, carry):
        for blocks_of_pattern in pattern_blocks:
            blocks_of_pattern(trip, carry)
        return carry

    lax.fori_loop(0, DIL_BLOCKS // ATT_UNROLL, trip_body, 0)

    ch = 256
    for rho in range(DIL_REGROUP):
        for c in range(quarter // ch):
            g_rows = slice(rho * quarter + c * ch, rho * quarter + (c + 1) * ch)
            n_rows = pl.ds(rho + DIL_REGROUP * c * ch, ch, stride=DIL_REGROUP)
            for pair in range(2):
                l0, l1, l2 = acc_l[0, pair, n_rows, :], acc_l[1, pair, g_rows, :], acc_l[2, pair, g_rows, :]
                top = jnp.maximum(jnp.maximum(l0, l1), l2)
                w0, w1, w2 = jnp.exp2(l0 - top), jnp.exp2(l1 - top), jnp.exp2(l2 - top)
                acc_o[0, pair, n_rows, :] = (
                    (w0 * acc_o[0, pair, n_rows, :] + w1 * acc_o[1, pair, g_rows, :]
                     + w2 * acc_o[2, pair, g_rows, :]) / (w0 + w1 + w2))
    for c in range(s // ch):
        r = slice(c * ch, (c + 1) * ch)
        for pair in range(2):
            lo = pair * 2 * HEAD_DIM
            gate = vg_ref[r, GROUP_W + lo:GROUP_W + lo + 2 * HEAD_DIM].astype(F32)
            o_ref[r, lo:lo + 2 * HEAD_DIM] = (acc_o[0, pair, r, :] * _silu(gate)).astype(BF16)


def _dilated(z3, cos_t, sin_t, mask_t):
    b = z3.shape[0]
    return pl.pallas_call(
        _dilated_kernel,
        grid=(b,),
        in_specs=[
            pl.BlockSpec((None, SEQ, 2 * GROUP_W), lambda i: (i, 0, 1)),
            pl.BlockSpec((None, SEQ, 2 * GROUP_W), lambda i: (i, 0, 2)),
            pl.BlockSpec((SEQ, 2 * HEAD_DIM), lambda i: (0, 0)),
            pl.BlockSpec((SEQ, 2 * HEAD_DIM), lambda i: (0, 0)),
            pl.BlockSpec(mask_t.shape, lambda i: (0, 0, 0)),
        ],
        out_specs=pl.BlockSpec((None, SEQ, GROUP_W), lambda i: (i, 0, 0)),
        out_shape=jax.ShapeDtypeStruct((b, SEQ, GROUP_W), BF16),
        scratch_shapes=[pltpu.VMEM((2, SEQ, 2 * HEAD_DIM), F32) for _ in range(6)]
        + [pltpu.VMEM((len(DILATED_PATTERNS), 2, SEQ, 2 * HEAD_DIM), F32) for _ in range(2)]
        + [pltpu.VMEM((2 * ATT_UNROLL, 2 * DIL_Q, 2 * DIL_Q), F32),
           pltpu.VMEM((2 * ATT_UNROLL, 2 * DIL_Q, 2 * DIL_Q), BF16),
           pltpu.VMEM((2 * ATT_UNROLL, 2 * DIL_Q, 2 * HEAD_DIM), F32)],
        compiler_params=_params(),
        name="dilated_attention",
    )(z3, z3, cos_t, sin_t, mask_t)


def _na_bias_table(rpb):
    c = np.arange(GRID_W)
    cs = np.clip(c - NA_COLS // 2, 0, GRID_W - NA_COLS)
    col_ok = (c[None, :] >= cs[:, None]) & (c[None, :] < cs[:, None] + NA_COLS)
    pad = GRID_W - NA_COLS
    tab = jnp.pad(rpb.astype(F32), ((0, 0), (0, 0), (pad, pad)))
    tab = _toeplitz(tab, 2, GRID_W, GRID_W)
    tab = jnp.where(col_ok[None, None, :, :], tab * LOG2E, NEG)
    h, nd = rpb.shape[0], 2 * NA_ROWS - 1
    tab = tab.reshape(h // 2, 2, nd, GRID_W, GRID_W).transpose(0, 2, 1, 3, 4)
    tab = tab.reshape(h // 2, nd, 2 * GRID_W, GRID_W)
    return jnp.concatenate([tab[:, :-1], tab[:, 1:]], axis=-1)


def _softmax_strips(s_scr, p_scr, bias, n_rows, n_keys, m_scr=None):
    for r0 in range(0, n_rows, STRIP):
        t = s_scr[r0:r0 + STRIP, 0:n_keys] + bias(r0, r0 + STRIP)
        m = jnp.max(t, axis=1, keepdims=True)
        p_scr[r0:r0 + STRIP, 0:n_keys] = jnp.exp2(t - m).astype(BF16)
        if m_scr is not None:
            m_scr[r0:r0 + STRIP, :] = jnp.broadcast_to(m, (STRIP, 2 * HEAD_DIM))


def _with_ones(v):
    return jnp.concatenate([v, jnp.ones_like(v)], axis=1)


def _na_kernel(qk_ref, vg_ref, bias_ref, o_ref, s_scr, p_scr, bias_scr):
    n_rows = SEQ // GRID_W
    nk = NA_ROWS * GRID_W
    hd2 = 2 * HEAD_DIM

    @pl.when(pl.program_id(0) == 0)
    def _():
        for pair in range(2):
            for off in range(NA_ROWS):
                for j in range(NA_ROWS // 2):
                    bias_scr[pair, off, :, 2 * j * GRID_W:(2 * j + 2) * GRID_W] = bias_ref[
                        pair, 2 * j - off + NA_ROWS - 1]

    def rows_trip(trip, carry):
        for sub in range(NA_UNROLL):
            r = trip * NA_UNROLL + sub
            rs = jnp.clip(r - NA_ROWS // 2, 0, n_rows - NA_ROWS)
            off = r - rs
            q_rows = pl.ds(pl.multiple_of(r * GRID_W, GRID_W), GRID_W)
            k_rows = pl.ds(pl.multiple_of(rs * GRID_W, GRID_W), nk)
            for pair in range(2):
                slot = 2 * sub + pair
                lo = pair * hd2
                q = qk_ref[q_rows, lo:lo + hd2].astype(F32) * (HEAD_DIM ** -0.5 * LOG2E)
                q = _stack_heads(q.astype(BF16))
                k = qk_ref[k_rows, GROUP_W + lo:GROUP_W + lo + hd2]
                v = _with_ones(vg_ref[k_rows, lo:lo + hd2])
                s_scr[slot] = lax.dot_general(q, k, (((1,), (1,)), ((), ())), preferred_element_type=F32)
                _softmax_strips(s_scr.at[slot], p_scr.at[slot],
                                lambda r0, r1, pair=pair, off=off: bias_scr[pair, off, r0:r1, :],
                                2 * GRID_W, nk)
                oa = jnp.dot(p_scr[slot], v, preferred_element_type=F32)
                o = oa[:, 0:hd2] / oa[:, hd2:2 * hd2]
                gate = vg_ref[q_rows, GROUP_W + lo:GROUP_W + lo + hd2].astype(F32)
                o_ref[q_rows, lo:lo + hd2] = (_unstack_heads(o, GRID_W) * _silu(gate)).astype(BF16)
        return carry

    lax.fori_loop(0, n_rows // NA_UNROLL, rows_trip, 0)


def _neighbourhood(z3, bias_tab, l):
    b = z3.shape[0]
    return pl.pallas_call(
        _na_kernel,
        grid=(b,),
        in_specs=[
            pl.BlockSpec((None, SEQ, 2 * GROUP_W), lambda i: (i, 0, 3)),
            pl.BlockSpec((None, SEQ, 2 * GROUP_W), lambda i: (i, 0, 4)),
            pl.BlockSpec((None,) + bias_tab.shape[1:], lambda i: (l, 0, 0, 0, 0)),
        ],
        out_specs=pl.BlockSpec((None, SEQ, GROUP_W), lambda i: (i, 0, 0)),
        out_shape=jax.ShapeDtypeStruct((b, SEQ, GROUP_W), BF16),
        scratch_shapes=[pltpu.VMEM((2 * NA_UNROLL, 2 * GRID_W, NA_ROWS * GRID_W), F32),
                        pltpu.VMEM((2 * NA_UNROLL, 2 * GRID_W, NA_ROWS * GRID_W), BF16),
                        pltpu.VMEM((2, NA_ROWS, 2 * GRID_W, NA_ROWS * GRID_W), F32)],
        compiler_params=_params(0, 1),
        name="neighbourhood_attention",
    )(z3, z3, bias_tab)


def _ssm_weights(a_re, a_im, log_dt, b_re, b_im, c_re, c_im):
    L, G, P, C = SSM_L, SSM_GROUPS, SSM_STATE, SSM_CH
    a_re, a_im, log_dt = a_re.astype(F32), a_im.astype(F32), log_dt.astype(F32)
    dt = jnp.exp(log_dt)[..., None]
    ks = jnp.arange(L + 1, dtype=F32)
    mag = jnp.exp((a_re * dt)[..., None] * ks)
    ang = (a_im * dt)[..., None] * ks
    pr, pim = mag * jnp.cos(ang), mag * jnp.sin(ang)
    abr, abi = pr[..., 1], pim[..., 1]
    den = a_re * a_re + a_im * a_im
    gr = ((abr - 1.0) * a_re + abi * a_im) / den
    gi = (abi * a_re - (abr - 1.0) * a_im) / den
    br, bi = b_re.astype(F32), b_im.astype(F32)
    bbr = gr[..., None] * br - gi[..., None] * bi
    bbi = gr[..., None] * bi + gi[..., None] * br
    cr, ci = c_re.astype(F32), c_im.astype(F32)
    cb_r = cr[:, :, None] * bbr.transpose(0, 1, 3, 2)[:, :, :, None] \
        - ci[:, :, None] * bbi.transpose(0, 1, 3, 2)[:, :, :, None]
    cb_i = cr[:, :, None] * bbi.transpose(0, 1, 3, 2)[:, :, :, None] \
        + ci[:, :, None] * bbr.transpose(0, 1, 3, 2)[:, :, :, None]
    kern = (jnp.einsum('dgxcp,dgpk->dgxck', cb_r, pr, precision=HI)
            - jnp.einsum('dgxcp,dgpk->dgxck', cb_i, pim, precision=HI))
    kf, kb = kern[0], kern[1]
    k_lag = jnp.concatenate([kb[..., 1:L][..., ::-1], (kf[..., 0] + kb[..., 0])[..., None],
                             kf[..., 1:L], jnp.zeros(kf.shape[:-1] + (1,), F32)], axis=-1)
    k_lag = k_lag.reshape(G, C * C, 2 * L)

    def state_in(d, idx):
        a_r = pr[d][:, :, idx].transpose(0, 2, 1)[:, None]
        a_i = pim[d][:, :, idx].transpose(0, 2, 1)[:, None]
        b_r = bbr[d].transpose(0, 2, 1)[:, :, None]
        b_i = bbi[d].transpose(0, 2, 1)[:, :, None]
        return a_r * b_r - a_i * b_i, a_r * b_i + a_i * b_r

    sf_r, sf_i = state_in(0, slice(L - 1, None, -1))
    sb_r, sb_i = state_in(1, slice(0, L))
    w_s = jnp.concatenate([sf_r, sb_r, sf_i, sb_i], axis=-1).reshape(G, C * L, 4 * P)

    def state_out(d, idx):
        c_r = cr[d].transpose(0, 2, 1)[..., None]
        c_i = ci[d].transpose(0, 2, 1)[..., None]
        a_r = pr[d][:, :, idx][:, :, None, :]
        a_i = pim[d][:, :, idx][:, :, None, :]
        return c_r * a_r - c_i * a_i, c_r * a_i + c_i * a_r

    ff_r, ff_i = state_out(0, slice(1, L + 1))
    fb_r, fb_i = state_out(1, slice(L, 0, -1))
    w_c = jnp.concatenate([ff_r, fb_r, -ff_i, -fb_i], axis=1).reshape(G, 4 * P, C * L)
    a_l = jnp.concatenate([pr[0, ..., L], pr[1, ..., L], pim[0, ..., L], pim[1, ..., L]], axis=-1)
    a_l = jnp.broadcast_to(a_l[:, None, :], (G, SSM_NB_MAX, 4 * P))
    return k_lag, w_s.astype(BF16), w_c.astype(BF16), a_l


def _ssm_toeplitz_kernel(k_ref, t_ref):
    L, C = SSM_L, SSM_CH

    def build(cp, carry):
        for c in range(C):
            lag = jnp.broadcast_to(k_ref[pl.ds(cp * C + c, 1), :], (L, 2 * L))
            tile = pltpu.roll(lag, L + 1, 1, stride=1, stride_axis=0)[:, 0:L]
            t_ref[pl.ds(pl.multiple_of(cp * L, L), L), c * L:(c + 1) * L] = tile.astype(BF16)
        return carry

    lax.fori_loop(0, C, build, 0)


def _ssm_toeplitz(k_lag):
    n, C, L = k_lag.shape[0], SSM_CH, SSM_L
    return pl.pallas_call(
        _ssm_toeplitz_kernel,
        grid=(n,),
        in_specs=[pl.BlockSpec((None, C * C, 2 * L), lambda g: (g, 0, 0))],
        out_specs=pl.BlockSpec((None, C * L, C * L), lambda g: (g, 0, 0)),
        out_shape=jax.ShapeDtypeStruct((n, C * L, C * L), BF16),
        compiler_params=_params(),
        name="ssm_toeplitz",
    )(k_lag)


def _ssm_kernel(u_ref, t_ref, ws_ref, wc_ref, al_ref, y_ref, x_scr, y_scr,
                s_re, s_im, ha_re, ha_im, hb_re, hb_im, *, nb):
    L, C, P = SSM_L, SSM_CH, SSM_STATE
    nc = SEQ // L
    m = nb * nc
    half = 2 * P

    for b in range(nb):
        ub = u_ref[b].astype(F32)
        for j in range(nc):
            r0 = (b * nc + j) * SSM_PITCH
            x_scr[r0:r0 + C, :] = ub[:, j * L:(j + 1) * L]
    lhs = jnp.concatenate([x_scr[pl.ds(c, m, stride=SSM_PITCH), :].astype(BF16) for c in range(C)],
                          axis=1)

    s = jnp.dot(lhs, ws_ref[...], preferred_element_type=F32)
    s_re[...] = s[:, 0:half]
    s_im[...] = s[:, half:2 * half]
    a_r = al_ref[0:nb, 0:half]
    a_i = al_ref[0:nb, half:2 * half]
    fwd = lax.broadcasted_iota(jnp.int32, (nb, half), 1) < P
    h_r = jnp.zeros((nb, half), F32)
    h_i = jnp.zeros((nb, half), F32)
    for k in range(nc):
        rf = pl.ds(k, nb, stride=nc)
        rb = pl.ds(nc - 1 - k, nb, stride=nc)
        ha_re[rf, :] = h_r
        ha_im[rf, :] = h_i
        hb_re[rb, :] = h_r
        hb_im[rb, :] = h_i
        s_r = jnp.where(fwd, s_re[rf, :], s_re[rb, :])
        s_i = jnp.where(fwd, s_im[rf, :], s_im[rb, :])
        h_r, h_i = a_r * h_r - a_i * h_i + s_r, a_r * h_i + a_i * h_r + s_i
    fwd_m = lax.broadcasted_iota(jnp.int32, (m, half), 1) < P
    h_all = jnp.concatenate([jnp.where(fwd_m, ha_re[...], hb_re[...]),
                             jnp.where(fwd_m, ha_im[...], hb_im[...])], axis=1).astype(BF16)
    y = jnp.dot(lhs, t_ref[...], preferred_element_type=F32)
    y = y + jnp.dot(h_all, wc_ref[...], preferred_element_type=F32)
    for c in range(C):
        y_scr[pl.ds(c, m, stride=SSM_PITCH), :] = y[:, c * L:(c + 1) * L]
    for b in range(nb):
        y_ref[b] = jnp.concatenate(
            [y_scr[(b * nc + j) * SSM_PITCH:(b * nc + j) * SSM_PITCH + C, :] for j in range(nc)], axis=1)


def _ssm_core(zt, w_t, w_s, w_c, a_l, l):
    b = zt.shape[0]
    L, G, C, P = SSM_L, SSM_GROUPS, SSM_CH, SSM_STATE
    nb = min(b, SSM_NB_MAX)
    m = nb * (SEQ // L)
    weight = lambda r, c: pl.BlockSpec((None, r, c), lambda g, i: (l * G + g, 0, 0))
    return pl.pallas_call(
        functools.partial(_ssm_kernel, nb=nb),
        grid=(G, b // nb),
        in_specs=[pl.BlockSpec((nb, C, SEQ), lambda g, i: (i, g, 0)),
                  weight(C * L, C * L), weight(C * L, 4 * P), weight(4 * P, C * L),
                  weight(SSM_NB_MAX, 4 * P)],
        out_specs=pl.BlockSpec((nb, C, SEQ), lambda g, i: (i, g, 0)),
        out_shape=jax.ShapeDtypeStruct((b, G * C, SEQ), F32),
        scratch_shapes=[pltpu.VMEM((m * SSM_PITCH, L), F32), pltpu.VMEM((m * SSM_PITCH, L), F32)]
        + [pltpu.VMEM((m, 2 * P), F32) for _ in range(6)],
        compiler_params=_params(2),
        name="ssm_core",
    )(zt, w_t, w_s, w_c, a_l)


def _block_diag(w):
    n, k = w.shape[0], w.shape[1]
    out = jnp.zeros((n * k, n * k), w.dtype)
    for i in range(n):
        out = lax.dynamic_update_slice(out, w[i], (i * k, i * k))
    return out


def kernel(x_prompt, x_sample, norm_g, w_in, w_out, pool_w, pool_scale, ssm_a_re, ssm_a_im,
           ssm_log_dt, ssm_b_re, ssm_b_im, ssm_c_re, ssm_c_im, ssm_d, glu_w, glu_b, na_rpb, final_g):
    inv_cnt = _pool_inv_count()
    cos_t, sin_t = _rope_tables()
    mask_t = _dilated_mask_table()
    gw = GROUP_W
    norm_all = norm_g.reshape(DEPTH, 1, D_MODEL).astype(F32)
    w_abd = jnp.concatenate([w_out[:, :2 * gw], w_out[:, 3 * gw:]], axis=1).astype(BF16)
    w_c = w_out[:, 2 * gw:3 * gw].astype(BF16)
    pool_bd = jax.vmap(_block_diag)(pool_w).astype(BF16)
    pool_sc = pool_scale.reshape(DEPTH, 1, gw).astype(F32)
    k_lag, ssm_ws, ssm_wc, ssm_al = jax.vmap(_ssm_weights)(
        ssm_a_re, ssm_a_im, ssm_log_dt, ssm_b_re, ssm_b_im, ssm_c_re, ssm_c_im)
    merge = lambda a: a.reshape((DEPTH * SSM_GROUPS,) + a.shape[2:])
    ssm = (_ssm_toeplitz(merge(k_lag)), merge(ssm_ws), merge(ssm_wc), merge(ssm_al))
    d_all = ssm_d.reshape(DEPTH, gw, 1).astype(F32)
    glu_wt = jnp.swapaxes(glu_w, 1, 2).astype(BF16)
    glu_b_all = glu_b.reshape(DEPTH, gw, 1).astype(F32)
    na_bias = jax.vmap(_na_bias_table)(na_rpb)
    fg = final_g.reshape(1, D_MODEL).astype(F32)
    nxt = (norm_all, w_in.astype(BF16))

    def trunk(x):
        b = x.shape[0]
        x2 = x.reshape(b * SEQ, D_MODEL)
        z2, zt = _proj(x2, nxt=nxt, nxt_layer=0)
        for l in range(DEPTH):
            z3 = z2.reshape(b, SEQ, TOK_W)
            y_a = _pool(z3, inv_cnt, pool_bd, pool_sc, l)
            y_b = _dilated(z3, cos_t, sin_t, mask_t)
            y_ct = _ssm_core(zt, *ssm, l)
            y_d = _neighbourhood(z3, na_bias, l)
            flat = lambda a: a.reshape(b * SEQ, gw)
            mix = (flat(y_a), flat(y_b), flat(y_d), y_ct, zt, d_all, glu_wt, glu_b_all, w_abd, w_c, fg)
            if l + 1 < DEPTH:
                x2, z2, zt = _proj(x2, mix=mix, mix_layer=l, nxt=nxt, nxt_layer=l + 1)
            else:
                (x2,) = _proj(x2, mix=mix, mix_layer=l, final=True)
        return x2.reshape(b, SEQ, D_MODEL)

    return (trunk(x_prompt), trunk(x_sample))
```
